```python
import math, functools
import jax, jax.numpy as jnp
from jax import lax
import numpy as np

D_MODEL = 1024
BATCH = 2
SEQ = 8192
DEPTH = 1
DEC_BATCH = 32
DEC_SEQ = 1
PAST_LEN = 16384
PAGE_SIZE = 128

H_A = 8
HD_A = 64
W_A = H_A * HD_A
ROT_DIM = HD_A // 4
ROPE_THETA = 500000.0
MOBA_BLOCK = 256
MOBA_TOPK = 3
Q_BLOCK = 64
PAGES_PER_BLOCK = MOBA_BLOCK // PAGE_SIZE
H_B = 8
DK_B = 128
DV_B = 128
W_B = H_B * DV_B
CONV_K = 4
C_CONV = H_B * (2 * DK_B + DV_B)
GDN_CHUNK = 64
PLE_DIM = 256
EPS = 1e-6
SPLIT_SIZES = (W_A, W_A, W_A, W_A, H_B * DK_B, H_B * DK_B, W_B, W_B, H_B, H_B, D_MODEL, D_MODEL)
N_IN = sum(SPLIT_SIZES)

kernel_name = 'moba_gdn_hybrid_step'

F32 = jnp.float32


def rms_norm(x, g):
    xf = x.astype(F32)
    y = xf * lax.rsqrt(jnp.mean(xf * xf, axis=-1, keepdims=True) + EPS)
    return (y * g.astype(F32)).astype(x.dtype)


def l2_norm(x):
    xf = x.astype(F32)
    return xf * lax.rsqrt(jnp.sum(xf * xf, axis=-1, keepdims=True) + EPS)


def partial_rope(x, pos):
    half = ROT_DIM // 2
    inv = jnp.power(ROPE_THETA, -jnp.arange(half, dtype=F32) / half)
    ang = pos.astype(F32)[:, None] * inv[None, :]
    cos = jnp.cos(ang)[None, :, None, :]
    sin = jnp.sin(ang)[None, :, None, :]
    xf = x.astype(F32)
    x1, x2 = xf[..., :half], xf[..., half:ROT_DIM]
    out = jnp.concatenate([x1 * cos - x2 * sin, x2 * cos + x1 * sin, xf[..., ROT_DIM:]], axis=-1)
    return out.astype(x.dtype)


def causal_conv_silu(u, buf, w):
    T = u.shape[1]
    up = jnp.concatenate([buf.astype(u.dtype), u], axis=1)
    out = sum(up[:, i:i + T] * w[i] for i in range(CONV_K))
    return jax.nn.silu(out), up[:, T:]


def gated_delta_chunked(q, k, v, g, beta, s0):
    B, T, H, DK = q.shape
    DV = v.shape[-1]
    C = GDN_CHUNK
    n = -(-T // C)
    pad = n * C - T

    def chunks(a):
        a = jnp.pad(a, [(0, 0), (0, pad)] + [(0, 0)] * (a.ndim - 2))
        a = a.reshape((B, n, C) + a.shape[2:])
        return jnp.moveaxis(a, 3, 1)

    q = chunks(q * (DK ** -0.5))
    k, v, g, beta = chunks(k), chunks(v), chunks(g), chunks(beta)
    gc = jnp.cumsum(g, axis=-1)
    causal = jnp.tril(jnp.ones((C, C), bool))
    strict = jnp.tril(jnp.ones((C, C), bool), -1)
    decay = jnp.exp(jnp.where(causal, gc[..., :, None] - gc[..., None, :], -jnp.inf))
    kbeta = k * beta[..., None]
    a_mat = jnp.where(strict, jnp.einsum('bhncd,bhnsd->bhncs', kbeta, k) * decay, 0.0)
    m = a_mat + jnp.eye(C, dtype=F32)
    rhs = jnp.concatenate([v * beta[..., None], kbeta * jnp.exp(gc)[..., None]], axis=-1)
    sol = lax.linalg.triangular_solve(m, rhs, left_side=True, lower=True, unit_diagonal=True)
    u, w = sol[..., :DV], sol[..., DV:]
    attn = jnp.where(causal, jnp.einsum('bhncd,bhnsd->bhncs', q, k) * decay, 0.0)
    q_dec = q * jnp.exp(gc)[..., None]
    k_tail = k * jnp.exp(gc[..., -1:] - gc)[..., None]
    c_dec = jnp.exp(gc[..., -1])
    xs = tuple(jnp.moveaxis(a, 2, 0) for a in (u, w, attn, q_dec, k_tail, c_dec))

    def step(s, xc):
        u_c, w_c, at_c, qd_c, kt_c, cd_c = xc
        v_new = u_c - jnp.einsum('bhcd,bhde->bhce', w_c, s)
        o = jnp.einsum('bhcd,bhde->bhce', qd_c, s) + jnp.einsum('bhcs,bhse->bhce', at_c, v_new)
        s = s * cd_c[..., None, None] + jnp.einsum('bhcd,bhce->bhde', kt_c, v_new)
        return s, o

    s_fin, o = lax.scan(step, s0, xs)
    o = o.transpose(1, 0, 3, 2, 4).reshape(B, n * C, H, DV)[:, :T]
    return o, s_fin


def moba_attend(q, q_pos, kbar, get_kv):
    B, T, H, D = q.shape
    nb = kbar.shape[1]
    if nb < MOBA_TOPK:
        kbar = jnp.pad(kbar, ((0, 0), (0, MOBA_TOPK - nb), (0, 0), (0, 0)))
    own = q_pos // MOBA_BLOCK
    gate = jnp.einsum('bthd,bnhd->bthn', q.astype(F32), kbar.astype(F32))
    fully_past = jnp.arange(kbar.shape[1])[None, :] < own[:, None]
    gate = jnp.where(fully_past[None, :, None, :], gate, -jnp.inf)
    _, top = lax.top_k(gate, MOBA_TOPK)
    sel_ok = top < own[None, :, None, None]
    own_idx = jnp.broadcast_to(own[None, :, None, None], (B, T, H, 1))
    idx = jnp.concatenate([jnp.minimum(top, nb - 1), own_idx], axis=-1).astype(jnp.int32)
    k_g, v_g = get_kv(idx)
    key_pos = idx[..., None] * MOBA_BLOCK + jnp.arange(MOBA_BLOCK)
    ok = jnp.concatenate([sel_ok, jnp.ones((B, T, H, 1), bool)], axis=-1)[..., None]
    ok = ok & (key_pos <= q_pos[None, :, None, None, None])
    s = jnp.einsum('bthd,bthnkd->bthnk', q, k_g).astype(F32) * (HD_A ** -0.5)
    s = jnp.where(ok, s, -jnp.inf).reshape(B, T, H, -1)
    p = jax.nn.softmax(s, axis=-1).reshape(k_g.shape[:-1]).astype(v_g.dtype)
    return jnp.einsum('bthnk,bthnkd->bthd', p, v_g)


def moba_prompt(q, k, v):
    B, T, H, D = q.shape
    nb = -(-T // MOBA_BLOCK)
    pad = nb * MOBA_BLOCK - T
    to_blocks = lambda a: jnp.pad(a, ((0, 0), (0, pad), (0, 0), (0, 0))).reshape(B, nb, MOBA_BLOCK, H, D)
    kblk, vblk = to_blocks(k), to_blocks(v)
    kbar = jnp.mean(kblk.astype(F32), axis=2)
    kbh = kblk.transpose(0, 3, 1, 2, 4)
    vbh = vblk.transpose(0, 3, 1, 2, 4)
    bi = jnp.arange(B)[:, None, None, None]
    hi = jnp.arange(H)[None, None, :, None]

    def get_kv(idx):
        return kbh[bi, hi, idx], vbh[bi, hi, idx]

    nq = T // Q_BLOCK
    qs = q.reshape(B, nq, Q_BLOCK, H, D).transpose(1, 0, 2, 3, 4)
    ps = jnp.arange(T, dtype=jnp.int32).reshape(nq, Q_BLOCK)
    o = lax.map(lambda a: moba_attend(a[0], a[1], kbar, get_kv), (qs, ps))
    return o.transpose(1, 0, 2, 3, 4).reshape(B, T, H, D)


def moba_paged(q, k, v, cache_k, cache_v, page_table):
    B, T, H, D = q.shape
    n_pages = page_table.shape[1]
    past_len = n_pages * PAGE_SIZE
    nb = -(-(past_len + T) // MOBA_BLOCK)
    n_new = nb * PAGES_PER_BLOCK - n_pages
    pad = n_new * PAGE_SIZE - T
    to_pages = lambda a, ref: jnp.pad(a.astype(ref.dtype), ((0, 0), (0, pad), (0, 0), (0, 0))).reshape(B, n_new, PAGE_SIZE, H, D)
    k_new, v_new = to_pages(k, cache_k), to_pages(v, cache_v)
    past_sums = jnp.sum(cache_k[page_table].astype(F32), axis=2)
    page_sums = jnp.concatenate([past_sums, jnp.sum(k_new.astype(F32), axis=2)], axis=1)
    kbar = page_sums.reshape(B, nb, PAGES_PER_BLOCK, H, D).sum(axis=2) / MOBA_BLOCK
    bi = jnp.arange(B)[:, None, None, None, None]
    hi = jnp.arange(H)[None, None, :, None, None]

    def get_kv(idx):
        n_sel = idx.shape[-1]
        pages = idx[..., None] * PAGES_PER_BLOCK + jnp.arange(PAGES_PER_BLOCK)
        in_past = (pages < n_pages)[..., None, None]
        phys = page_table[bi, jnp.minimum(pages, n_pages - 1)]
        new_i = jnp.clip(pages - n_pages, 0, n_new - 1)

        def fetch(pool, new):
            rows = jnp.where(in_past, pool[phys, :, hi], new[bi, new_i, :, hi])
            return rows.reshape(B, T, H, n_sel, MOBA_BLOCK, D)

        return fetch(cache_k, k_new), fetch(cache_v, v_new)

    q_pos = past_len + jnp.arange(T, dtype=jnp.int32)
    return moba_attend(q, q_pos, kbar, get_kv)


def decoder_layer(x, p, pos, s0, conv0, attend, g_mix, w_in, conv_w, a_log, dt_bias, g_onorm,
                  w_pa, w_pb, w_o, g_ple, w_ple_gate, w_ple):
    B, T, _ = x.shape
    h = rms_norm(x, g_mix)
    cuts = np.cumsum(SPLIT_SIZES)[:-1].tolist()
    qa, ka, va, za, qb, kb, vb, zb, b_lin, a_lin, ga, gb = jnp.split(h @ w_in, cuts, axis=-1)
    qa = partial_rope(qa.reshape(B, T, H_A, HD_A), pos)
    ka = partial_rope(ka.reshape(B, T, H_A, HD_A), pos)
    va = va.reshape(B, T, H_A, HD_A)
    oa = attend(qa, ka, va)
    ya = (oa.reshape(B, T, W_A) * jax.nn.silu(za)) @ w_pa
    conv_out, conv_new = causal_conv_silu(jnp.concatenate([qb, kb, vb], axis=-1), conv0, conv_w)
    qc, kc, vc = jnp.split(conv_out, [H_B * DK_B, 2 * H_B * DK_B], axis=-1)
    qc = l2_norm(qc.reshape(B, T, H_B, DK_B))
    kc = l2_norm(kc.reshape(B, T, H_B, DK_B))
    vc = vc.reshape(B, T, H_B, DV_B).astype(F32)
    beta = jax.nn.sigmoid(b_lin.astype(F32))
    g = -jnp.exp(a_log.astype(F32)) * jax.nn.softplus(a_lin.astype(F32) + dt_bias.astype(F32))
    ob, s_new = gated_delta_chunked(qc, kc, vc, g, beta, s0.astype(F32))
    ob = rms_norm(ob, g_onorm).astype(x.dtype) * jax.nn.silu(zb.reshape(B, T, H_B, DV_B))
    yb = ob.reshape(B, T, W_B) @ w_pb
    mixed = jax.nn.sigmoid(ga) * ya + jax.nn.sigmoid(gb) * yb
    x = x + mixed @ w_o
    x = x + jax.nn.sigmoid(rms_norm(x, g_ple) @ w_ple_gate) * (p @ w_ple)
    return x, ka, va, s_new, conv_new


def setup_inputs(seed: int = 0) -> dict:
    key = jax.random.key(seed)
    ks = jax.random.split(key, 24)
    n_pages = PAST_LEN // PAGE_SIZE
    n_used = DEC_BATCH * n_pages
    n_pool = n_used + n_used // 4

    def nrm(k, shape, scale=1.0):
        return jax.random.normal(k, shape, F32) * scale

    dt = jnp.exp(jax.random.uniform(ks[13], (DEPTH, H_B), F32, math.log(1e-3), math.log(1e-1)))
    return {
        'x_prompt': nrm(ks[0], (BATCH, SEQ, D_MODEL)),
        'x_sample': nrm(ks[1], (DEC_BATCH, DEC_SEQ, D_MODEL)),
        'p_prompt': nrm(ks[2], (DEPTH, BATCH, SEQ, PLE_DIM)),
        'p_sample': nrm(ks[3], (DEPTH, DEC_BATCH, DEC_SEQ, PLE_DIM)),
        'cache_k': nrm(ks[4], (DEPTH, n_pool, PAGE_SIZE, H_A, HD_A)),
        'cache_v': nrm(ks[5], (DEPTH, n_pool, PAGE_SIZE, H_A, HD_A)),
        'page_table': jax.random.permutation(ks[6], n_pool)[:n_used].reshape(DEC_BATCH, n_pages).astype(jnp.int32),
        'state_gdn_s': nrm(ks[7], (DEPTH, DEC_BATCH, H_B, DK_B, DV_B), 0.3),
        'state_gdn_conv': nrm(ks[8], (DEPTH, DEC_BATCH, CONV_K - 1, C_CONV)),
        'g_mix': 1.0 + nrm(ks[9], (DEPTH, D_MODEL), 0.02),
        'w_in': nrm(ks[10], (DEPTH, D_MODEL, N_IN), D_MODEL ** -0.5),
        'conv_w': nrm(ks[11], (DEPTH, CONV_K, C_CONV), CONV_K ** -0.5),
        'a_log': jnp.log(jax.random.uniform(ks[12], (DEPTH, H_B), F32, 1.0, 16.0)),
        'dt_bias': dt + jnp.log(-jnp.expm1(-dt)),
        'g_onorm': 1.0 + nrm(ks[14], (DEPTH, DV_B), 0.02),
        'w_pa': nrm(ks[15], (DEPTH, W_A, D_MODEL), W_A ** -0.5),
        'w_pb': nrm(ks[16], (DEPTH, W_B, D_MODEL), W_B ** -0.5),
        'w_o': nrm(ks[17], (DEPTH, D_MODEL, D_MODEL), D_MODEL ** -0.5),
        'g_ple': 1.0 + nrm(ks[18], (DEPTH, D_MODEL), 0.02),
        'w_ple_gate': nrm(ks[19], (DEPTH, D_MODEL, D_MODEL), D_MODEL ** -0.5),
        'w_ple': nrm(ks[20], (DEPTH, PLE_DIM, D_MODEL), PLE_DIM ** -0.5),
        'g_final': 1.0 + nrm(ks[21], (D_MODEL,), 0.02),
    }


def reference(x_prompt, x_sample, p_prompt, p_sample, cache_k, cache_v, page_table, state_gdn_s,
              state_gdn_conv, g_mix, w_in, conv_w, a_log, dt_bias, g_onorm, w_pa, w_pb, w_o, g_ple,
              w_ple_gate, w_ple, g_final):
    bp, tp, _ = x_prompt.shape
    ts = x_sample.shape[1]
    pos_p = jnp.arange(tp, dtype=jnp.int32)
    pos_s = page_table.shape[1] * PAGE_SIZE + jnp.arange(ts, dtype=jnp.int32)
    xp, xs = x_prompt, x_sample
    kp_l, vp_l, sp_l, cp_l, ks_l, vs_l, ss_l, cs_l = [], [], [], [], [], [], [], []
    for l in range(DEPTH):
        s0p = jnp.zeros((bp, H_B, DK_B, DV_B), F32)
        c0p = jnp.zeros((bp, CONV_K - 1, C_CONV), xp.dtype)
        xp, kp, vp, sp, cp = decoder_layer(
            xp, p_prompt[l], pos_p, s0p, c0p, moba_prompt,
            g_mix[l], w_in[l], conv_w[l], a_log[l], dt_bias[l], g_onorm[l],
            w_pa[l], w_pb[l], w_o[l], g_ple[l], w_ple_gate[l], w_ple[l])
        attend_s = functools.partial(moba_paged, cache_k=cache_k[l], cache_v=cache_v[l], page_table=page_table)
        xs, ksm, vsm, ssm, csm = decoder_layer(
            xs, p_sample[l], pos_s, state_gdn_s[l], state_gdn_conv[l], attend_s,
            g_mix[l], w_in[l], conv_w[l], a_log[l], dt_bias[l], g_onorm[l],
            w_pa[l], w_pb[l], w_o[l], g_ple[l], w_ple_gate[l], w_ple[l])
        kp_l.append(kp); vp_l.append(vp); sp_l.append(sp.astype(state_gdn_s.dtype)); cp_l.append(cp)
        ks_l.append(ksm); vs_l.append(vsm); ss_l.append(ssm.astype(state_gdn_s.dtype)); cs_l.append(csm)
    y_prompt = rms_norm(xp, g_final)
    y_sample = rms_norm(xs, g_final)
    k_prompt, v_prompt = jnp.stack(kp_l), jnp.stack(vp_l)
    s_prompt, conv_prompt = jnp.stack(sp_l), jnp.stack(cp_l)
    k_sample, v_sample = jnp.stack(ks_l), jnp.stack(vs_l)
    s_sample, conv_sample = jnp.stack(ss_l), jnp.stack(cs_l)
    return (y_prompt, y_sample, k_prompt, v_prompt, s_prompt, conv_prompt, k_sample, v_sample, s_sample, conv_sample)
```

```python
import functools
import math

import jax
import jax.numpy as jnp
from jax import lax
from jax.experimental import pallas as pl
from jax.experimental.pallas import tpu as pltpu

F32 = jnp.float32
BF16 = jnp.bfloat16
HIGHEST = lax.Precision.HIGHEST

H_A = 8
HD_A = 64
W_A = H_A * HD_A
ROT_DIM = HD_A // 4
ROPE_THETA = 500000.0
MOBA_BLOCK = 256
MOBA_TOPK = 3
H_B = 8
DK_B = 128
DV_B = 128
W_B = H_B * DV_B
CONV_K = 4
C_CONV = H_B * (2 * DK_B + DV_B)
GDN_CHUNK = 64
EPS = 1e-6

LANES = 128
SUBLANES = 8
VMEM_LIMIT_BYTES = 56 * 1024 * 1024

NEG_BIG = -1e30


def _cparams(sem):
    return pltpu.CompilerParams(dimension_semantics=sem, vmem_limit_bytes=VMEM_LIMIT_BYTES)


def _rms(x, g):
    return x * lax.rsqrt(jnp.mean(x * x, axis=-1, keepdims=True) + EPS) * g


def _silu(x):
    return x * jax.nn.sigmoid(x)


def _dot(a, b):
    return jnp.dot(a.astype(BF16), b.astype(BF16), preferred_element_type=F32)


def _dot_nt(a, b):
    return lax.dot_general(a.astype(BF16), b.astype(BF16), (((1,), (1,)), ((), ())),
                           preferred_element_type=F32)


def _dot_tn(a, b):
    return lax.dot_general(a.astype(BF16), b.astype(BF16), (((0,), (0,)), ((), ())),
                           preferred_element_type=F32)


def _dot_hi(a, b):
    return jnp.dot(a, b, precision=HIGHEST, preferred_element_type=F32)


def _dot_nt_hi(a, b):
    return lax.dot_general(a, b, (((1,), (1,)), ((), ())), precision=HIGHEST,
                           preferred_element_type=F32)


def _attn_proj_kernel(x_ref, g_ref, w_ref, c_ref, sa_ref, sb_ref, q_ref, k_ref, v_ref):
    h = _rms(x_ref[...], g_ref[...]).astype(BF16)
    y = jnp.dot(h, w_ref[...], preferred_element_type=F32)
    c, sa, sb = c_ref[...], sa_ref[...], sb_ref[...]
    half = ROT_DIM // 2
    for dst, base in ((q_ref, 0), (k_ref, W_A)):
        for s in range(W_A // LANES):
            z = y[:, base + s * LANES: base + (s + 1) * LANES]
            dst[:, s * LANES:(s + 1) * LANES] = (
                z * c + pltpu.roll(z, LANES - half, 1) * sa + pltpu.roll(z, half, 1) * sb)
    v_ref[...] = y[:, 2 * W_A:]


def _rope_tables(pos):
    half = ROT_DIM // 2
    inv = jnp.power(ROPE_THETA, -jnp.arange(half, dtype=F32) / half)
    ang = pos.astype(F32)[:, None] * inv[None, :]
    cos, sin = jnp.cos(ang), jnp.sin(ang)
    t = pos.shape[0]
    ones = jnp.ones((t, HD_A - ROT_DIM), F32)
    zeros = jnp.zeros((t, HD_A - ROT_DIM), F32)
    zh = jnp.zeros((t, half), F32)
    c = jnp.concatenate([cos, cos, ones], axis=1)
    sa = jnp.concatenate([-sin, zh, zeros], axis=1)
    sb = jnp.concatenate([zh, sin, zeros], axis=1)
    rep = LANES // HD_A
    return tuple(jnp.tile(a, (1, rep)) for a in (c, sa, sb))


def _attn_proj(x2d, g, w, tabs, tm):
    n, d = x2d.shape
    t_tab = tabs[0].shape[0]
    period = t_tab // tm
    tab_spec = pl.BlockSpec((tm, LANES), lambda i: (i % period, 0))
    out_spec = pl.BlockSpec((tm, W_A), lambda i: (i, 0))
    out = jax.ShapeDtypeStruct((n, W_A), F32)
    return pl.pallas_call(
        _attn_proj_kernel,
        grid=(n // tm,),
        in_specs=[pl.BlockSpec((tm, d), lambda i: (i, 0)),
                  pl.BlockSpec((1, d), lambda i: (0, 0)),
                  pl.BlockSpec((d, 3 * W_A), lambda i: (0, 0)),
                  tab_spec, tab_spec, tab_spec],
        out_specs=[out_spec, out_spec, out_spec],
        out_shape=[out, out, out],
        compiler_params=_cparams(("parallel",)),
        name="attn_proj",
    )(x2d, g, w, *tabs)


def _gdn_proj_kernel(x_ref, g_ref, w_ref, wba_ref, cw_ref, c0_ref, al_ref, dtb_ref,
                     q_ref, k_ref, v_ref, bg_ref, tail_ref, ubuf, *, tm, t_valid, tail_row):
    i = pl.program_id(1)
    h = _rms(x_ref[0], g_ref[...]).astype(BF16)
    u = jnp.dot(h, w_ref[...], preferred_element_type=F32)

    @pl.when(i == 0)
    def _():
        ubuf[0:SUBLANES, :] = c0_ref[0]

    @pl.when(i > 0)
    def _():
        ubuf[0:SUBLANES, :] = ubuf[tm:tm + SUBLANES, :]

    ubuf[SUBLANES:SUBLANES + tm, :] = u
    cw = cw_ref[...]
    conv = u * cw[CONV_K - 1:CONV_K, :]
    for j in range(1, CONV_K):
        conv = conv + ubuf[SUBLANES - j:SUBLANES - j + tm, :] * cw[CONV_K - 1 - j:CONV_K - j, :]
    act = _silu(conv)
    if t_valid % tm != 0:
        row = i * tm + lax.broadcasted_iota(jnp.int32, (tm, 1), 0)
        valid = (row < t_valid).astype(F32)
    else:
        valid = None

    def put(ref, s, val):
        ref[0, :, s * LANES:(s + 1) * LANES] = val if valid is None else val * valid

    for s in range(H_B):
        for ref, base in ((q_ref, 0), (k_ref, H_B * DK_B)):
            z = act[:, base + s * DK_B: base + (s + 1) * DK_B]
            put(ref, s, z * lax.rsqrt(jnp.sum(z * z, axis=-1, keepdims=True) + EPS))
        put(v_ref, s, act[:, 2 * H_B * DK_B + s * DV_B: 2 * H_B * DK_B + (s + 1) * DV_B])

    ba = jnp.dot(h, wba_ref[...], preferred_element_type=F32)
    lane = lax.broadcasted_iota(jnp.int32, (1, LANES), 1)
    z = ba + dtb_ref[...]
    softplus = jnp.maximum(z, 0.0) + jnp.log1p(jnp.exp(-jnp.abs(z)))
    bg = jnp.where(lane < H_B, jax.nn.sigmoid(ba), -jnp.exp(al_ref[...]) * softplus)
    bg_ref[0] = bg if valid is None else bg * valid
    tail_ref[0] = ubuf[tail_row:tail_row + SUBLANES, :]


def _gdn_proj(x3d, g, w_qkv, w_ba, conv_w, conv0_pad, al_row, dtb_row, tm, t_valid):
    b, t, d = x3d.shape
    n_tiles = -(-t_valid // tm)
    tail_row = t_valid - (n_tiles - 1) * tm
    kern = functools.partial(_gdn_proj_kernel, tm=tm, t_valid=t_valid, tail_row=tail_row)
    seq = lambda w: pl.BlockSpec((1, tm, w), lambda bi, i: (bi, i, 0))
    const = lambda r, c: pl.BlockSpec((r, c), lambda bi, i: (0, 0))
    return pl.pallas_call(
        kern,
        grid=(b, n_tiles),
        in_specs=[seq(d), const(1, d), const(d, C_CONV), const(d, LANES), const(CONV_K, C_CONV),
                  pl.BlockSpec((1, SUBLANES, C_CONV), lambda bi, i: (bi, 0, 0)),
                  const(1, LANES), const(1, LANES)],
        out_specs=[seq(H_B * DK_B), seq(H_B * DK_B), seq(W_B), seq(LANES),
                   pl.BlockSpec((1, SUBLANES, C_CONV), lambda bi, i: (bi, 0, 0))],
        out_shape=[jax.ShapeDtypeStruct((b, n_tiles * tm, H_B * DK_B), F32),
                   jax.ShapeDtypeStruct((b, n_tiles * tm, H_B * DK_B), F32),
                   jax.ShapeDtypeStruct((b, n_tiles * tm, W_B), F32),
                   jax.ShapeDtypeStruct((b, n_tiles * tm, LANES), F32),
                   jax.ShapeDtypeStruct((b, SUBLANES, C_CONV), F32)],
        scratch_shapes=[pltpu.VMEM((tm + 2 * SUBLANES, C_CONV), F32)],
        compiler_params=_cparams(("parallel", "arbitrary")),
        name="gdn_proj",
    )(x3d, g, w_qkv, w_ba, conv_w, conv0_pad, al_row, dtb_row)


def _block_mean_kernel(k_ref, o_ref, *, nblk):
    k = k_ref[0].reshape(nblk, MOBA_BLOCK, W_A)
    o_ref[0] = jnp.sum(k, axis=1) * (1.0 / MOBA_BLOCK)


def _block_means(k3d):
    b, t, w = k3d.shape
    nb = t // MOBA_BLOCK
    grp = SUBLANES if nb % SUBLANES == 0 else nb
    return pl.pallas_call(
        functools.partial(_block_mean_kernel, nblk=grp),
        grid=(b, nb // grp),
        in_specs=[pl.BlockSpec((1, grp * MOBA_BLOCK, w), lambda bi, i: (bi, i, 0))],
        out_specs=pl.BlockSpec((1, grp, w), lambda bi, i: (bi, i, 0)),
        out_shape=jax.ShapeDtypeStruct((b, nb, w), F32),
        compiler_params=_cparams(("parallel", "parallel")),
        name="block_means",
    )(k3d)


def _top3(gate, idx, axis, big):
    sels = []
    g = gate
    for _ in range(MOBA_TOPK):
        m = jnp.max(g, axis=axis, keepdims=True)
        ii = jnp.min(jnp.where(g == m, idx, big), axis=axis, keepdims=True)
        sels.append(ii)
        g = jnp.where(idx == ii, -jnp.inf, g)
    return sels


def _moba_prompt_kernel(q_ref, k_ref, v_ref, kbar_ref, o_ref, *, nb):
    i = pl.program_id(2)
    tq = MOBA_BLOCK
    q2 = q_ref[0]
    kbar = kbar_ref[0]
    lane = lax.broadcasted_iota(jnp.int32, (1, LANES), 1)
    blk = lax.broadcasted_iota(jnp.int32, (1, nb), 1)
    blk_f = blk.astype(F32)
    rows = lax.broadcasted_iota(jnp.int32, (tq, tq), 0)
    cols = lax.broadcasted_iota(jnp.int32, (tq, tq), 1)
    n_heads = LANES // HD_A

    def kv_tile(j):
        start = pl.multiple_of(j * MOBA_BLOCK, MOBA_BLOCK)
        return (k_ref[0, pl.ds(start, MOBA_BLOCK), :].astype(BF16),
                v_ref[0, pl.ds(start, MOBA_BLOCK), :].astype(BF16))

    qs, sels, carry0 = [], [], []
    kd, vd = kv_tile(i)
    for hh in range(n_heads):
        hmask = (lane // HD_A) == hh
        qh = jnp.where(hmask, q2, 0.0)
        gate = _dot_nt_hi(qh, kbar)
        gate = jnp.where(blk < i, gate, -jnp.inf)
        sels.append(_top3(gate, blk_f, 1, float(nb)))
        qb = (qh * (HD_A ** -0.5)).astype(BF16)
        qs.append(qb)
        s = lax.dot_general(qb, kd, (((1,), (1,)), ((), ())), preferred_element_type=F32)
        s = jnp.where(cols <= rows, s, NEG_BIG)
        m = jnp.max(s, axis=1, keepdims=True)
        p = jnp.exp(s - m)
        l = jnp.sum(p, axis=1, keepdims=True)
        acc = jnp.dot(p.astype(BF16), vd, preferred_element_type=F32)
        carry0 += [m, l, acc]

    def body(j, carry):
        kj, vj = kv_tile(j)
        jf = j.astype(F32)
        out = []
        for hh in range(n_heads):
            m, l, acc = carry[3 * hh: 3 * hh + 3]
            i1, i2, i3 = sels[hh]
            sel = (i1 == jf) | (i2 == jf) | (i3 == jf)
            s = lax.dot_general(qs[hh], kj, (((1,), (1,)), ((), ())), preferred_element_type=F32)
            s = jnp.where(sel, s, NEG_BIG)
            m_new = jnp.maximum(m, jnp.max(s, axis=1, keepdims=True))
            alpha = jnp.exp(m - m_new)
            p = jnp.exp(s - m_new)
            l = alpha * l + jnp.sum(p, axis=1, keepdims=True)
            acc = alpha * acc + jnp.dot(p.astype(BF16), vj, preferred_element_type=F32)
            out += [m_new, l, acc]
        return tuple(out)

    carry = lax.fori_loop(0, i, body, tuple(carry0))
    o = jnp.zeros((tq, LANES), F32)
    for hh in range(n_heads):
        m, l, acc = carry[3 * hh: 3 * hh + 3]
        o = jnp.where((lane // HD_A) == hh, acc / l, o)
    o_ref[0] = o


def _moba_prompt(q3d, k3d, v3d, kbar):
    b, t, w = q3d.shape
    nb = t // MOBA_BLOCK
    n_pairs = w // LANES
    qspec = pl.BlockSpec((1, MOBA_BLOCK, LANES), lambda bi, hp, i: (bi, i, hp))
    kvspec = pl.BlockSpec((1, t, LANES), lambda bi, hp, i: (bi, 0, hp))
    return pl.pallas_call(
        functools.partial(_moba_prompt_kernel, nb=nb),
        grid=(b, n_pairs, nb),
        in_specs=[qspec, kvspec, kvspec,
                  pl.BlockSpec((1, nb, LANES), lambda bi, hp, i: (bi, 0, hp))],
        out_specs=qspec,
        out_shape=jax.ShapeDtypeStruct((b, t, w), F32),
        compiler_params=_cparams(("parallel", "parallel", "arbitrary")),
        name="moba_prompt",
    )(q3d, k3d, v3d, kbar)


def _gdn_chunk_kernel(q_ref, k_ref, v_ref, bg_ref, s0_ref, gn_ref, o_ref, s_out_ref, s_scr):
    c = pl.program_id(1)
    n_c = pl.num_programs(1)
    cc = GDN_CHUNK

    @pl.when(c == 0)
    def _():
        s_scr[...] = s0_ref[0]

    bg = bg_ref[0]
    gn = gn_ref[...]
    ri = lax.broadcasted_iota(jnp.int32, (cc, cc), 0)
    ci = lax.broadcasted_iota(jnp.int32, (cc, cc), 1)
    eye = (ri == ci).astype(F32)
    causal = ri >= ci
    strict = ri > ci
    for h in range(H_B):
        b_col = bg[:, h:h + 1]
        g_col = bg[:, H_B + h:H_B + h + 1]
        qh = q_ref[0, :, h * DK_B:(h + 1) * DK_B] * (DK_B ** -0.5)
        kh = k_ref[0, :, h * DK_B:(h + 1) * DK_B]
        vh = v_ref[0, :, h * DV_B:(h + 1) * DV_B]
        g_row = jnp.sum(eye * g_col, axis=0, keepdims=True)
        gc_col = jnp.sum(jnp.where(causal, g_row, 0.0), axis=1, keepdims=True)
        gc_row = jnp.sum(jnp.where(ri <= ci, g_col, 0.0), axis=0, keepdims=True)
        gc_last = gc_col[cc - 1:cc, :]
        decay = jnp.exp(jnp.where(causal, gc_col - gc_row, -jnp.inf))
        egc = jnp.exp(gc_col)
        kbeta = kh * b_col
        a_mat = jnp.where(strict, _dot_nt(kbeta, kh) * decay, 0.0)
        n_pow = -a_mat
        minv = eye + n_pow
        for _ in range(int(math.log2(cc)) - 1):
            n_pow = _dot_hi(n_pow, n_pow)
            minv = minv + _dot_hi(minv, n_pow)
        u = _dot_hi(minv, vh * b_col)
        w = _dot_hi(minv, kbeta * egc)
        attn = jnp.where(causal, _dot_nt(qh, kh) * decay, 0.0)
        q_dec = qh * egc
        k_tail = kh * jnp.exp(gc_last - gc_col)
        c_dec = jnp.exp(gc_last)
        s = s_scr[h]
        v_new = u - _dot(w, s)
        o = _dot(q_dec, s) + _dot(attn, v_new)
        s_scr[h] = s * c_dec + _dot_tn(k_tail, v_new)
        o_ref[0, :, h * DV_B:(h + 1) * DV_B] = _rms(o, gn)

    @pl.when(c == n_c - 1)
    def _():
        s_out_ref[0] = s_scr[...]


def _gdn_chunk(q, k, v, bg, s0, g_onorm_row):
    b, t, _ = q.shape
    n_c = t // GDN_CHUNK
    seq = lambda w: pl.BlockSpec((1, GDN_CHUNK, w), lambda bi, c: (bi, c, 0))
    sspec = pl.BlockSpec((1, H_B, DK_B, DV_B), lambda bi, c: (bi, 0, 0, 0))
    return pl.pallas_call(
        _gdn_chunk_kernel,
        grid=(b, n_c),
        in_specs=[seq(H_B * DK_B), seq(H_B * DK_B), seq(W_B), seq(LANES), sspec,
                  pl.BlockSpec((1, DV_B), lambda bi, c: (0, 0))],
        out_specs=[seq(W_B), sspec],
        out_shape=[jax.ShapeDtypeStruct((b, t, W_B), F32),
                   jax.ShapeDtypeStruct((b, H_B, DK_B, DV_B), F32)],
        scratch_shapes=[pltpu.VMEM((H_B, DK_B, DV_B), F32)],
        compiler_params=_cparams(("parallel", "arbitrary")),
        name="gdn_chunk",
    )(q, k, v, bg, s0, g_onorm_row)


def _out_stage_kernel(x_ref, oa_ref, ob_ref, p_ref, gmix_ref, wzg_ref, wpa_ref, wpb_ref, wo_ref,
                      gple_ref, wpg_ref, wple_ref, gfin_ref, y_ref, *, final):
    x = x_ref[...]
    h = _rms(x, gmix_ref[...]).astype(BF16)
    zg = jnp.dot(h, wzg_ref[...], preferred_element_type=F32)
    za = zg[:, :W_A]
    zb = zg[:, W_A:W_A + W_B]
    d = x.shape[1]
    ga = zg[:, W_A + W_B:W_A + W_B + d]
    gb = zg[:, W_A + W_B + d:]
    ya = _dot(oa_ref[...] * _silu(za), wpa_ref[...])
    yb = _dot(ob_ref[...] * _silu(zb), wpb_ref[...])
    mixed = jax.nn.sigmoid(ga) * ya + jax.nn.sigmoid(gb) * yb
    x = x + _dot(mixed, wo_ref[...])
    gate = jax.nn.sigmoid(_dot(_rms(x, gple_ref[...]), wpg_ref[...]))
    x = x + gate * _dot(p_ref[...], wple_ref[...])
    y_ref[...] = _rms(x, gfin_ref[...]) if final else x


def _out_stage(x2d, oa, ob, p2d, gmix, wzg, wpa, wpb, wo, gple, wpg, wple, gfin, tm, final):
    n, d = x2d.shape
    row = lambda w: pl.BlockSpec((tm, w), lambda i: (i, 0))
    const = lambda a: pl.BlockSpec(a.shape, lambda i: (0, 0))
    return pl.pallas_call(
        functools.partial(_out_stage_kernel, final=final),
        grid=(n // tm,),
        in_specs=[row(d), row(W_A), row(W_B), row(p2d.shape[1]), const(gmix), const(wzg), const(wpa),
                  const(wpb), const(wo), const(gple), const(wpg), const(wple), const(gfin)],
        out_specs=row(d),
        out_shape=jax.ShapeDtypeStruct((n, d), F32),
        compiler_params=_cparams(("parallel",)),
        name="out_stage",
    )(x2d, oa, ob, p2d, gmix, wzg, wpa, wpb, wo, gple, wpg, wple, gfin)


def _page_sum_kernel(pt_ref, k_ref, o_ref):
    pp = pl.program_id(2)

    @pl.when(pp == 0)
    def _():
        o_ref[...] = jnp.zeros_like(o_ref)

    o_ref[0] += jnp.sum(k_ref[0], axis=0, keepdims=True)


def _page_block_sums(cache_k3, pt_flat, b, n_pages):
    ppb = MOBA_BLOCK // cache_k3.shape[1]
    nblk = n_pages // ppb
    w = cache_k3.shape[2]
    out = pl.pallas_call(
        _page_sum_kernel,
        grid_spec=pltpu.PrefetchScalarGridSpec(
            num_scalar_prefetch=1,
            grid=(b, nblk, ppb),
            in_specs=[pl.BlockSpec((1, cache_k3.shape[1], w),
                                   lambda bi, j, pp, pt: (pt[bi * n_pages + j * ppb + pp], 0, 0))],
            out_specs=pl.BlockSpec((1, 1, w), lambda bi, j, pp, pt: (bi * nblk + j, 0, 0)),
        ),
        out_shape=jax.ShapeDtypeStruct((b * nblk, 1, w), F32),
        compiler_params=_cparams(("parallel", "parallel", "arbitrary")),
        name="page_sums",
    )(pt_flat, cache_k3)
    return out.reshape(b, nblk, w)


def _sample_select_kernel(q_ref, ks_ref, o_ref, *, nblk):
    prod = ks_ref[0] * (1.0 / MOBA_BLOCK) * q_ref[0]
    lane_head = lax.broadcasted_iota(jnp.int32, (W_A, LANES), 0) // HD_A
    head = lax.broadcasted_iota(jnp.int32, (W_A, LANES), 1)
    gate = _dot_hi(prod, (lane_head == head).astype(F32))
    blk_f = lax.broadcasted_iota(jnp.int32, (nblk, LANES), 0).astype(F32)
    sels = _top3(gate, blk_f, 0, float(nblk))
    rows = lax.broadcasted_iota(jnp.int32, (SUBLANES, LANES), 0)
    out = jnp.zeros((SUBLANES, LANES), F32)
    for r, sel in enumerate(sels):
        out = jnp.where(rows == r, sel, out)
    o_ref[0] = out.astype(jnp.int32)


def _sample_select(q3, ksum):
    b, nblk, w = ksum.shape
    return pl.pallas_call(
        functools.partial(_sample_select_kernel, nblk=nblk),
        grid=(b,),
        in_specs=[pl.BlockSpec((1, 1, w), lambda bi: (bi, 0, 0)),
                  pl.BlockSpec((1, nblk, w), lambda bi: (bi, 0, 0))],
        out_specs=pl.BlockSpec((1, SUBLANES, LANES), lambda bi: (bi, 0, 0)),
        out_shape=jax.ShapeDtypeStruct((b, SUBLANES, LANES), jnp.int32),
        compiler_params=_cparams(("parallel",)),
        name="sample_select",
    )(q3, ksum)


def _paged_attn_kernel(pt_ref, top_ref, q_ref, kn_ref, vn_ref, k_ref, v_ref, o_ref, m_scr, l_scr, acc_scr):
    h = pl.program_id(1)
    s = pl.program_id(2)
    n_s = pl.num_programs(2)
    lane = lax.broadcasted_iota(jnp.int32, (1, W_A), 1)
    hmask = (lane // HD_A) == h
    qh = jnp.where(hmask, q_ref[0], 0.0) * (HD_A ** -0.5)

    @pl.when(s == 0)
    def _():
        m_scr[...] = jnp.sum(qh * kn_ref[0], axis=1, keepdims=True)
        l_scr[...] = jnp.ones_like(l_scr)
        acc_scr[...] = vn_ref[0]

    sc = jnp.sum(k_ref[0] * qh, axis=1, keepdims=True)
    m_old = m_scr[...]
    m_new = jnp.maximum(m_old, jnp.max(sc, axis=0, keepdims=True))
    alpha = jnp.exp(m_old - m_new)
    p = jnp.exp(sc - m_new)
    l_new = alpha * l_scr[...] + jnp.sum(p, axis=0, keepdims=True)
    acc_new = alpha * acc_scr[...] + jnp.sum(p * v_ref[0], axis=0, keepdims=True)
    m_scr[...] = m_new
    l_scr[...] = l_new
    acc_scr[...] = acc_new

    @pl.when((s == n_s - 1) & (h == 0))
    def _():
        o_ref[0] = jnp.where(hmask, acc_new / l_new, 0.0)

    @pl.when((s == n_s - 1) & (h > 0))
    def _():
        o_ref[0] = jnp.where(hmask, acc_new / l_new, o_ref[0])


def _paged_attn(q3, kn3, vn3, cache_k3, cache_v3, pt_flat, top_flat, n_pages):
    b, _, w = q3.shape
    page = cache_k3.shape[1]
    ppb = MOBA_BLOCK // page
    n_s = MOBA_TOPK * ppb

    def page_map(bi, h, s, pt, top):
        blk = top[(bi * MOBA_TOPK + s // ppb) * H_A + h]
        return (pt[bi * n_pages + blk * ppb + s % ppb], 0, 0)

    tok = pl.BlockSpec((1, 1, w), lambda bi, h, s, pt, top: (bi, 0, 0))
    return pl.pallas_call(
        _paged_attn_kernel,
        grid_spec=pltpu.PrefetchScalarGridSpec(
            num_scalar_prefetch=2,
            grid=(b, H_A, n_s),
            in_specs=[tok, tok, tok,
                      pl.BlockSpec((1, page, w), page_map), pl.BlockSpec((1, page, w), page_map)],
            out_specs=tok,
            scratch_shapes=[pltpu.VMEM((1, 1), F32), pltpu.VMEM((1, 1), F32), pltpu.VMEM((1, w), F32)],
        ),
        out_shape=jax.ShapeDtypeStruct((b, 1, w), F32),
        compiler_params=_cparams(("parallel", "arbitrary", "arbitrary")),
        name="paged_attn",
    )(pt_flat, top_flat, q3, kn3, vn3, cache_k3, cache_v3)


def _split_weights(w_in_l, d):
    o = 0
    parts = {}
    for name, n in (("qa", W_A), ("ka", W_A), ("va", W_A), ("za", W_A), ("qb", H_B * DK_B),
                    ("kb", H_B * DK_B), ("vb", W_B), ("zb", W_B), ("beta", H_B), ("alpha", H_B),
                    ("ga", d), ("gb", d)):
        parts[name] = w_in_l[:, o:o + n]
        o += n
    w_attn = jnp.concatenate([parts["qa"], parts["ka"], parts["va"]], axis=1).astype(BF16)
    w_qkv = jnp.concatenate([parts["qb"], parts["kb"], parts["vb"]], axis=1).astype(BF16)
    w_ba = jnp.concatenate([parts["beta"], parts["alpha"],
                            jnp.zeros((d, LANES - 2 * H_B), F32)], axis=1).astype(BF16)
    w_zg = jnp.concatenate([parts["za"], parts["zb"], parts["ga"], parts["gb"]], axis=1).astype(BF16)
    return w_attn, w_qkv, w_ba, w_zg


def _lane_row(vec, offset):
    out = jnp.zeros((1, LANES), F32)
    return out.at[0, offset:offset + vec.shape[0]].set(vec.astype(F32))


def _pick_tile(n, pref):
    t = pref
    while n % t:
        t //= 2
    return t


def kernel(x_prompt, x_sample, p_prompt, p_sample, cache_k, cache_v, page_table, state_gdn_s, state_gdn_conv, g_mix, w_in, conv_w, a_log, dt_bias, g_onorm, w_pa, w_pb, w_o, g_ple, w_ple_gate, w_ple, g_final):
    bp, tp, d = x_prompt.shape
    bs, ts, _ = x_sample.shape
    depth = w_in.shape[0]
    n_pages = page_table.shape[1]
    page = cache_k.shape[2]
    assert ts == 1 and tp % MOBA_BLOCK == 0 and (n_pages * page) % MOBA_BLOCK == 0
    assert tp % GDN_CHUNK == 0

    pos_p = jnp.arange(tp, dtype=jnp.int32)
    pos_s = jnp.full((bs,), n_pages * page, dtype=jnp.int32)
    tabs_p = _rope_tables(pos_p)
    tabs_s = _rope_tables(pos_s)
    pt_flat = page_table.reshape(-1).astype(jnp.int32)
    ts_pad = GDN_CHUNK

    xp = x_prompt.reshape(bp * tp, d)
    xs = x_sample.reshape(bs, d)
    outs = {k: [] for k in ("kp", "vp", "sp", "cp", "ks", "vs", "ss", "cs")}
    for l in range(depth):
        final = l == depth - 1
        w_attn, w_qkv, w_ba, w_zg = _split_weights(w_in[l], d)
        gmix = g_mix[l].reshape(1, d)
        al_row = _lane_row(a_log[l], H_B)
        dtb_row = _lane_row(dt_bias[l], H_B)
        gon = g_onorm[l].reshape(1, DV_B)
        wpa, wpb, wo = w_pa[l].astype(BF16), w_pb[l].astype(BF16), w_o[l].astype(BF16)
        wpg, wple = w_ple_gate[l].astype(BF16), w_ple[l].astype(BF16)
        gple = g_ple[l].reshape(1, d)
        gfin = g_final.reshape(1, d)

        tm = _pick_tile(tp, 512)
        qa, ka, va = _attn_proj(xp, gmix, w_attn, tabs_p, tm)
        conv0 = jnp.zeros((bp, SUBLANES, C_CONV), F32)
        qn, kn, vn, bg, tail = _gdn_proj(xp.reshape(bp, tp, d), gmix, w_qkv, w_ba, conv_w[l], conv0,
                                         al_row, dtb_row, _pick_tile(tp, 256), tp)
        ka3 = ka.reshape(bp, tp, W_A)
        kbar = _block_means(ka3)
        oa = _moba_prompt(qa.reshape(bp, tp, W_A), ka3, va.reshape(bp, tp, W_A), kbar)
        s0 = jnp.zeros((bp, H_B, DK_B, DV_B), F32)
        ob, s_fin = _gdn_chunk(qn, kn, vn, bg, s0, gon)
        xp = _out_stage(xp, oa.reshape(bp * tp, W_A), ob.reshape(bp * tp, W_B),
                        p_prompt[l].reshape(bp * tp, -1), gmix, w_zg, wpa, wpb, wo, gple, wpg, wple,
                        gfin, _pick_tile(bp * tp, 256), final)
        outs["kp"].append(ka.reshape(bp, tp, H_A, HD_A))
        outs["vp"].append(va.reshape(bp, tp, H_A, HD_A))
        outs["sp"].append(s_fin.astype(state_gdn_s.dtype))
        outs["cp"].append(tail[:, SUBLANES - (CONV_K - 1):, :])

        qa_s, ka_s, va_s = _attn_proj(xs, gmix, w_attn, tabs_s, bs)
        xs_pad = jnp.pad(xs.reshape(bs, 1, d), ((0, 0), (0, ts_pad - 1), (0, 0)))
        conv0_s = jnp.pad(state_gdn_conv[l].astype(F32), ((0, 0), (SUBLANES - (CONV_K - 1), 0), (0, 0)))
        qn_s, kn_s, vn_s, bg_s, tail_s = _gdn_proj(xs_pad, gmix, w_qkv, w_ba, conv_w[l], conv0_s,
                                                   al_row, dtb_row, ts_pad, 1)
        ck3 = cache_k[l].reshape(cache_k.shape[1], page, W_A)
        cv3 = cache_v[l].reshape(cache_v.shape[1], page, W_A)
        ksum = _page_block_sums(ck3, pt_flat, bs, n_pages)
        q3 = qa_s.reshape(bs, 1, W_A)
        top = _sample_select(q3, ksum)
        top_flat = top[:, :MOBA_TOPK, :H_A].reshape(-1)
        oa_s = _paged_attn(q3, ka_s.reshape(bs, 1, W_A), va_s.reshape(bs, 1, W_A), ck3, cv3,
                           pt_flat, top_flat, n_pages)
        ob_s, s_fin_s = _gdn_chunk(qn_s, kn_s, vn_s, bg_s, state_gdn_s[l].astype(F32), gon)
        xs = _out_stage(xs, oa_s.reshape(bs, W_A), ob_s[:, 0, :], p_sample[l].reshape(bs, -1), gmix, w_zg,
                        wpa, wpb, wo, gple, wpg, wple, gfin, bs, final)
        outs["ks"].append(ka_s.reshape(bs, 1, H_A, HD_A))
        outs["vs"].append(va_s.reshape(bs, 1, H_A, HD_A))
        outs["ss"].append(s_fin_s.astype(state_gdn_s.dtype))
        outs["cs"].append(tail_s[:, SUBLANES - (CONV_K - 1):, :])

    y_prompt = xp.reshape(bp, tp, d)
    y_sample = xs.reshape(bs, ts, d)
    st = lambda k: jnp.stack(outs[k])
    return (y_prompt, y_sample, st("kp"), st("vp"), st("sp"), st("cp"),
            st("ks"), st("vs"), st("ss"), st("cs"))
```

```python
import functools
import math

import jax
import jax.numpy as jnp
from jax import lax
from jax.experimental import pallas as pl
from jax.experimental.pallas import tpu as pltpu

F32 = jnp.float32
BF16 = jnp.bfloat16
HIGHEST = lax.Precision.HIGHEST

H_A = 8
HD_A = 64
W_A = H_A * HD_A
ROT_DIM = HD_A // 4
ROPE_THETA = 500000.0
MOBA_BLOCK = 256
MOBA_TOPK = 3
H_B = 8
DK_B = 128
DV_B = 128
W_B = H_B * DV_B
CONV_K = 4
C_CONV = H_B * (2 * DK_B + DV_B)
GDN_CHUNK = 64
EPS = 1e-6

LANES = 128
SUBLANES = 8
MXU_DIM = 256
VMEM_LIMIT_BYTES = 56 * 1024 * 1024

NEG_BIG = -1e30
KV_GROUP = 4


def _cparams(sem):
    return pltpu.CompilerParams(dimension_semantics=sem, vmem_limit_bytes=VMEM_LIMIT_BYTES)


def _rms(x, g):
    return x * lax.rsqrt(jnp.mean(x * x, axis=-1, keepdims=True) + EPS) * g


def _silu(x):
    return x * jax.nn.sigmoid(x)


def _dot(a, b):
    return jnp.dot(a.astype(BF16), b.astype(BF16), preferred_element_type=F32)


def _dot_nt(a, b):
    return lax.dot_general(a.astype(BF16), b.astype(BF16), (((1,), (1,)), ((), ())),
                           preferred_element_type=F32)


def _dot_tn(a, b):
    return lax.dot_general(a.astype(BF16), b.astype(BF16), (((0,), (0,)), ((), ())),
                           preferred_element_type=F32)


def _dot_hi(a, b):
    return jnp.dot(a, b, precision=HIGHEST, preferred_element_type=F32)


def _dot_nt_hi(a, b):
    return lax.dot_general(a, b, (((1,), (1,)), ((), ())), precision=HIGHEST,
                           preferred_element_type=F32)


def _attn_proj_kernel(x_ref, g_ref, w_ref, c_ref, sa_ref, sb_ref, q_ref, k_ref, v_ref):
    h = _rms(x_ref[...], g_ref[...]).astype(BF16)
    y = jnp.dot(h, w_ref[...], preferred_element_type=F32)
    c, sa, sb = c_ref[...], sa_ref[...], sb_ref[...]
    half = ROT_DIM // 2
    for dst, base in ((q_ref, 0), (k_ref, W_A)):
        for s in range(W_A // LANES):
            z = y[:, base + s * LANES: base + (s + 1) * LANES]
            dst[:, s * LANES:(s + 1) * LANES] = (
                z * c + pltpu.roll(z, LANES - half, 1) * sa + pltpu.roll(z, half, 1) * sb)
    v_ref[...] = y[:, 2 * W_A:]


def _rope_tables(pos):
    half = ROT_DIM // 2
    inv = jnp.power(ROPE_THETA, -jnp.arange(half, dtype=F32) / half)
    ang = pos.astype(F32)[:, None] * inv[None, :]
    cos, sin = jnp.cos(ang), jnp.sin(ang)
    t = pos.shape[0]
    ones = jnp.ones((t, HD_A - ROT_DIM), F32)
    zeros = jnp.zeros((t, HD_A - ROT_DIM), F32)
    zh = jnp.zeros((t, half), F32)
    c = jnp.concatenate([cos, cos, ones], axis=1)
    sa = jnp.concatenate([-sin, zh, zeros], axis=1)
    sb = jnp.concatenate([zh, sin, zeros], axis=1)
    rep = LANES // HD_A
    return tuple(jnp.tile(a, (1, rep)) for a in (c, sa, sb))


def _attn_proj(x2d, g, w, tabs, tm):
    n, d = x2d.shape
    t_tab = tabs[0].shape[0]
    period = t_tab // tm
    tab_spec = pl.BlockSpec((tm, LANES), lambda i: (i % period, 0))
    out_spec = pl.BlockSpec((tm, W_A), lambda i: (i, 0))
    out = jax.ShapeDtypeStruct((n, W_A), F32)
    return pl.pallas_call(
        _attn_proj_kernel,
        grid=(n // tm,),
        in_specs=[pl.BlockSpec((tm, d), lambda i: (i, 0)),
                  pl.BlockSpec((1, d), lambda i: (0, 0)),
                  pl.BlockSpec((d, 3 * W_A), lambda i: (0, 0)),
                  tab_spec, tab_spec, tab_spec],
        out_specs=[out_spec, out_spec, out_spec],
        out_shape=[out, out, out],
        compiler_params=_cparams(("parallel",)),
        name="attn_proj",
    )(x2d, g, w, *tabs)


def _gdn_proj_kernel(x_ref, g_ref, w_ref, wba_ref, cw_ref, c0_ref, al_ref, dtb_ref,
                     q_ref, k_ref, v_ref, bg_ref, tail_ref, ubuf, *, tm, t_valid, tail_row):
    i = pl.program_id(1)
    h = _rms(x_ref[0], g_ref[...]).astype(BF16)
    u = jnp.dot(h, w_ref[...], preferred_element_type=F32)

    @pl.when(i == 0)
    def _():
        ubuf[0:SUBLANES, :] = c0_ref[0]

    @pl.when(i > 0)
    def _():
        ubuf[0:SUBLANES, :] = ubuf[tm:tm + SUBLANES, :]

    ubuf[SUBLANES:SUBLANES + tm, :] = u
    cw = cw_ref[...]
    conv = u * cw[CONV_K - 1:CONV_K, :]
    for j in range(1, CONV_K):
        conv = conv + ubuf[SUBLANES - j:SUBLANES - j + tm, :] * cw[CONV_K - 1 - j:CONV_K - j, :]
    act = _silu(conv)
    if t_valid % tm != 0:
        row = i * tm + lax.broadcasted_iota(jnp.int32, (tm, 1), 0)
        valid = (row < t_valid).astype(F32)
    else:
        valid = None

    def put(ref, s, val):
        ref[0, :, s * LANES:(s + 1) * LANES] = val if valid is None else val * valid

    for s in range(H_B):
        for ref, base in ((q_ref, 0), (k_ref, H_B * DK_B)):
            z = act[:, base + s * DK_B: base + (s + 1) * DK_B]
            put(ref, s, z * lax.rsqrt(jnp.sum(z * z, axis=-1, keepdims=True) + EPS))
        put(v_ref, s, act[:, 2 * H_B * DK_B + s * DV_B: 2 * H_B * DK_B + (s + 1) * DV_B])

    ba = jnp.dot(h, wba_ref[...], preferred_element_type=F32)
    lane = lax.broadcasted_iota(jnp.int32, (1, LANES), 1)
    z = ba + dtb_ref[...]
    softplus = jnp.maximum(z, 0.0) + jnp.log1p(jnp.exp(-jnp.abs(z)))
    bg = jnp.where(lane < H_B, jax.nn.sigmoid(ba), -jnp.exp(al_ref[...]) * softplus)
    bg_ref[0] = bg if valid is None else bg * valid
    tail_ref[0] = ubuf[tail_row:tail_row + SUBLANES, :]


def _gdn_proj(x3d, g, w_qkv, w_ba, conv_w, conv0_pad, al_row, dtb_row, tm, t_valid):
    b, t, d = x3d.shape
    n_tiles = -(-t_valid // tm)
    tail_row = t_valid - (n_tiles - 1) * tm
    kern = functools.partial(_gdn_proj_kernel, tm=tm, t_valid=t_valid, tail_row=tail_row)
    seq = lambda w: pl.BlockSpec((1, tm, w), lambda bi, i: (bi, i, 0))
    const = lambda r, c: pl.BlockSpec((r, c), lambda bi, i: (0, 0))
    return pl.pallas_call(
        kern,
        grid=(b, n_tiles),
        in_specs=[seq(d), const(1, d), const(d, C_CONV), const(d, LANES), const(CONV_K, C_CONV),
                  pl.BlockSpec((1, SUBLANES, C_CONV), lambda bi, i: (bi, 0, 0)),
                  const(1, LANES), const(1, LANES)],
        out_specs=[seq(H_B * DK_B), seq(H_B * DK_B), seq(W_B), seq(LANES),
                   pl.BlockSpec((1, SUBLANES, C_CONV), lambda bi, i: (bi, 0, 0))],
        out_shape=[jax.ShapeDtypeStruct((b, n_tiles * tm, H_B * DK_B), F32),
                   jax.ShapeDtypeStruct((b, n_tiles * tm, H_B * DK_B), F32),
                   jax.ShapeDtypeStruct((b, n_tiles * tm, W_B), F32),
                   jax.ShapeDtypeStruct((b, n_tiles * tm, LANES), F32),
                   jax.ShapeDtypeStruct((b, SUBLANES, C_CONV), F32)],
        scratch_shapes=[pltpu.VMEM((tm + 2 * SUBLANES, C_CONV), F32)],
        compiler_params=_cparams(("parallel", "arbitrary")),
        name="gdn_proj",
    )(x3d, g, w_qkv, w_ba, conv_w, conv0_pad, al_row, dtb_row)


def _block_mean_kernel(k_ref, o_ref, *, nblk):
    k = k_ref[0].reshape(nblk, MOBA_BLOCK, W_A)
    o_ref[0] = jnp.sum(k, axis=1) * (1.0 / MOBA_BLOCK)


def _block_means(k3d):
    b, t, w = k3d.shape
    nb = t // MOBA_BLOCK
    grp = SUBLANES if nb % SUBLANES == 0 else nb
    return pl.pallas_call(
        functools.partial(_block_mean_kernel, nblk=grp),
        grid=(b, nb // grp),
        in_specs=[pl.BlockSpec((1, grp * MOBA_BLOCK, w), lambda bi, i: (bi, i, 0))],
        out_specs=pl.BlockSpec((1, grp, w), lambda bi, i: (bi, i, 0)),
        out_shape=jax.ShapeDtypeStruct((b, nb, w), F32),
        compiler_params=_cparams(("parallel", "parallel")),
        name="block_means",
    )(k3d)


def _top3(gate, idx, axis, big):
    sels = []
    g = gate
    for _ in range(MOBA_TOPK):
        m = jnp.max(g, axis=axis, keepdims=True)
        ii = jnp.min(jnp.where(g == m, idx, big), axis=axis, keepdims=True)
        sels.append(ii)
        g = jnp.where(idx == ii, -jnp.inf, g)
    return sels


def _moba_prompt_kernel(q_ref, k_ref, v_ref, kbar_ref, o_ref, kb_scr, vt_scr, *, nb):
    i = pl.program_id(2)
    tq = MOBA_BLOCK
    n_heads = LANES // HD_A

    @pl.when(i == 0)
    def _():
        def prep(j, carry):
            start = pl.multiple_of(j * MOBA_BLOCK, MOBA_BLOCK)
            kb_scr[j] = k_ref[0, pl.ds(start, MOBA_BLOCK), :].astype(BF16)
            vt_scr[j] = v_ref[0, pl.ds(start, MOBA_BLOCK), :].T.astype(BF16)
            return carry
        lax.fori_loop(0, nb, prep, 0)

    q_t = q_ref[0].T
    kbar = kbar_ref[0]
    dim_head = lax.broadcasted_iota(jnp.int32, (LANES, 1), 0) // HD_A
    blk = lax.broadcasted_iota(jnp.int32, (nb, 1), 0)
    blk_f = blk.astype(F32)
    key_i = lax.broadcasted_iota(jnp.int32, (tq, tq), 0)
    qry_i = lax.broadcasted_iota(jnp.int32, (tq, tq), 1)
    heads = range(n_heads)

    def attend(j0, n_blk, keep, stats):
        kj = kb_scr[pl.ds(j0, n_blk)].reshape(n_blk * MOBA_BLOCK, LANES)
        vj = jnp.concatenate([vt_scr[j0 + g] for g in range(n_blk)], axis=1)
        raw = [jnp.dot(kj, qs[hh], preferred_element_type=F32) for hh in heads]
        ss = [[jnp.where(keep[hh][g], raw[hh][g * MOBA_BLOCK:(g + 1) * MOBA_BLOCK], NEG_BIG)
               for g in range(n_blk)] for hh in heads]
        ms = []
        for hh in heads:
            m = jnp.max(ss[hh][0], axis=0, keepdims=True)
            for g in range(1, n_blk):
                m = jnp.maximum(m, jnp.max(ss[hh][g], axis=0, keepdims=True))
            ms.append(m if stats is None else jnp.maximum(stats[hh][0], m))
        ps = [[jnp.exp(s - ms[hh]) for s in ss[hh]] for hh in heads]
        ls = [sum(jnp.sum(p, axis=0, keepdims=True) for p in ps[hh]) for hh in heads]
        pv = [jnp.dot(vj[hh * HD_A:(hh + 1) * HD_A, :],
                      jnp.concatenate([p.astype(BF16) for p in ps[hh]], axis=0),
                      preferred_element_type=F32) for hh in heads]
        if stats is None:
            return [(ms[hh], ls[hh], pv[hh]) for hh in heads]
        out = []
        for hh in heads:
            m, l, acc = stats[hh]
            alpha = jnp.exp(m - ms[hh])
            out.append((ms[hh], alpha * l + ls[hh], alpha * acc + pv[hh]))
        return out

    qts = [jnp.where(dim_head == hh, q_t, 0.0) for hh in heads]
    sels = []
    for hh in heads:
        gate = jnp.where(blk < i, _dot_hi(kbar, qts[hh]), -jnp.inf)
        sels.append(_top3(gate, blk_f, 0, float(nb)))
    qs = [(qts[hh] * (HD_A ** -0.5)).astype(BF16) for hh in heads]
    causal = key_i <= qry_i
    stats0 = attend(i, 1, [[causal]] * n_heads, None)

    def make_body(n_blk):
        def body(t, carry):
            j0 = t * n_blk
            keep = []
            for hh in heads:
                i1, i2, i3 = sels[hh]
                jfs = [(j0 + g).astype(F32) for g in range(n_blk)]
                keep.append([(i1 == jf) | (i2 == jf) | (i3 == jf) for jf in jfs])
            stats = attend(j0, n_blk, keep, [carry[3 * hh:3 * hh + 3] for hh in heads])
            return tuple(x for st in stats for x in st)
        return body

    n_grp = i // KV_GROUP
    carry = lax.fori_loop(0, n_grp, make_body(KV_GROUP), tuple(x for st in stats0 for x in st))
    carry = lax.fori_loop(n_grp * KV_GROUP, i, make_body(1), carry)
    o_t = jnp.concatenate([carry[3 * hh + 2] / carry[3 * hh + 1] for hh in heads], axis=0)
    o_ref[0] = o_t.T


def _moba_prompt(q3d, k3d, v3d, kbar):
    b, t, w = q3d.shape
    nb = t // MOBA_BLOCK
    n_pairs = w // LANES
    qspec = pl.BlockSpec((1, MOBA_BLOCK, LANES), lambda bi, hp, i: (bi, i, hp))
    kvspec = pl.BlockSpec((1, t, LANES), lambda bi, hp, i: (bi, 0, hp))
    return pl.pallas_call(
        functools.partial(_moba_prompt_kernel, nb=nb),
        grid=(b, n_pairs, nb),
        in_specs=[qspec, kvspec, kvspec,
                  pl.BlockSpec((1, nb, LANES), lambda bi, hp, i: (bi, 0, hp))],
        out_specs=qspec,
        out_shape=jax.ShapeDtypeStruct((b, t, w), F32),
        scratch_shapes=[pltpu.VMEM((nb, MOBA_BLOCK, LANES), BF16),
                        pltpu.VMEM((nb, LANES, MOBA_BLOCK), BF16)],
        compiler_params=_cparams(("parallel", "parallel", "arbitrary")),
        name="moba_prompt",
    )(q3d, k3d, v3d, kbar)


def _gdn_seq_kernel(q_ref, k_ref, v_ref, bg_ref, s0_ref, gn_ref, o_ref, s_out_ref, s_scr, *, rows):
    step = pl.program_id(1)
    n_steps = pl.num_programs(1)
    cc = GDN_CHUNK
    ncs = rows // cc

    @pl.when(step == 0)
    def _():
        s_scr[...] = s0_ref[0]

    bg = bg_ref[0]
    gn = gn_ref[...]
    ri = lax.broadcasted_iota(jnp.int32, (rows, rows), 0)
    ci = lax.broadcasted_iota(jnp.int32, (rows, rows), 1)
    same = (ri // cc) == (ci // cc)
    gc = _dot_hi((same & (ci <= ri)).astype(F32), bg)
    gl = _dot_hi(same.astype(F32), bg)
    gc_t = gc.T
    e_gc = jnp.exp(gc)
    e_tail = jnp.exp(gl - gc)
    e_tot = jnp.exp(gl)
    pr = lax.broadcasted_iota(jnp.int32, (cc, rows), 0)
    pl_i = lax.broadcasted_iota(jnp.int32, (cc, rows), 1)
    pc = pl_i % cc
    lane_blk = pl_i // cc
    causal_p = pr >= pc
    strict_p = pr > pc

    def pack(full):
        out = full[(ncs - 1) * cc:ncs * cc]
        for c in range(ncs - 2, -1, -1):
            out = jnp.where(lane_blk == c, full[c * cc:(c + 1) * cc], out)
        return out

    def pack_col(col):
        out = col[(ncs - 1) * cc:ncs * cc]
        for c in range(ncs - 2, -1, -1):
            out = jnp.where(lane_blk == c, col[c * cc:(c + 1) * cc], out)
        return out

    def bdiag(p):
        return jnp.where(same, jnp.concatenate([p] * ncs, axis=0), 0.0)

    heads = range(H_B)
    col = lambda a, h: a[:, H_B + h:H_B + h + 1]
    qs = [q_ref[0, :, h * DK_B:(h + 1) * DK_B] * (DK_B ** -0.5) for h in heads]
    ks = [k_ref[0, :, h * DK_B:(h + 1) * DK_B] for h in heads]
    kbetas = [ks[h] * bg[:, h:h + 1] for h in heads]
    fulls = [_dot_nt(jnp.concatenate([kbetas[h], qs[h]], axis=0), ks[h]) for h in heads]
    decays = [jnp.exp(jnp.where(causal_p, pack_col(col(gc, h)) - gc_t[H_B + h:H_B + h + 1, :], -jnp.inf))
              for h in heads]
    attns = [jnp.where(causal_p, pack(fulls[h][rows:]) * decays[h], 0.0) for h in heads]
    es = [-jnp.where(strict_p, pack(fulls[h][:rows]) * decays[h], 0.0) for h in heads]
    pws = [_dot(es[h], bdiag(es[h])) for h in heads]
    for _ in range(int(math.log2(cc)) - 2):
        rs_ = [_dot(jnp.concatenate([es[h], pws[h]], axis=0), bdiag(pws[h])) for h in heads]
        es = [es[h] + pws[h] + rs_[h][:cc] for h in heads]
        pws = [rs_[h][cc:] for h in heads]
    es = [es[h] + pws[h] + _dot(es[h], bdiag(pws[h])) for h in heads]
    rhss = [jnp.concatenate([v_ref[0, :, h * DV_B:(h + 1) * DV_B] * bg[:, h:h + 1],
                             kbetas[h] * col(e_gc, h)], axis=1) for h in heads]
    sols = [rhss[h] + _dot(bdiag(es[h]), rhss[h]) for h in heads]
    q_decs = [qs[h] * col(e_gc, h) for h in heads]
    k_tails = [ks[h] * col(e_tail, h) for h in heads]
    ss = [s_scr[h] for h in heads]
    for c in range(ncs):
        rs = slice(c * cc, (c + 1) * cc)
        wqs = [_dot(jnp.concatenate([sols[h][rs, DV_B:], q_decs[h][rs]], axis=0), ss[h]) for h in heads]
        v_news = [sols[h][rs, :DV_B] - wqs[h][:cc] for h in heads]
        os_ = [wqs[h][cc:] + _dot(attns[h][:, rs], v_news[h]) for h in heads]
        kvs = [_dot_tn(k_tails[h][rs], v_news[h]) for h in heads]
        ss = [ss[h] * e_tot[c * cc:c * cc + 1, H_B + h:H_B + h + 1] + kvs[h] for h in heads]
        for h in heads:
            o_ref[0, rs, h * DV_B:(h + 1) * DV_B] = _rms(os_[h], gn)
    for h in heads:
        s_scr[h] = ss[h]


    @pl.when(step == n_steps - 1)
    def _():
        s_out_ref[0] = s_scr[...]


def _gdn_seq(q, k, v, bg, s0, g_onorm_row):
    b, t, _ = q.shape
    rows = MXU_DIM
    while t % rows:
        rows //= 2
    seq = lambda w: pl.BlockSpec((1, rows, w), lambda bi, c: (bi, c, 0))
    sspec = pl.BlockSpec((1, H_B, DK_B, DV_B), lambda bi, c: (bi, 0, 0, 0))
    return pl.pallas_call(
        functools.partial(_gdn_seq_kernel, rows=rows),
        grid=(b, t // rows),
        in_specs=[seq(H_B * DK_B), seq(H_B * DK_B), seq(W_B), seq(LANES), sspec,
                  pl.BlockSpec((1, DV_B), lambda bi, c: (0, 0))],
        out_specs=[seq(W_B), sspec],
        out_shape=[jax.ShapeDtypeStruct((b, t, W_B), F32),
                   jax.ShapeDtypeStruct((b, H_B, DK_B, DV_B), F32)],
        scratch_shapes=[pltpu.VMEM((H_B, DK_B, DV_B), F32)],
        compiler_params=_cparams(("parallel", "arbitrary")),
        name="gdn_seq",
    )(q, k, v, bg, s0, g_onorm_row)


def _gdn_step_kernel(q_ref, k_ref, v_ref, bg_ref, s0_ref, gn_ref, o_ref, s_out_ref):
    bg = bg_ref[0]
    gn = gn_ref[...]
    row = lax.broadcasted_iota(jnp.int32, (SUBLANES, 1), 0)
    for h in range(H_B):
        beta = bg[0:1, h:h + 1]
        eg = jnp.exp(bg[0:1, H_B + h:H_B + h + 1])
        q8 = q_ref[0, :, h * DK_B:(h + 1) * DK_B] * (DK_B ** -0.5)
        k8 = k_ref[0, :, h * DK_B:(h + 1) * DK_B]
        v_row = v_ref[0, 0:1, h * DV_B:(h + 1) * DV_B]
        s0 = s0_ref[0, h]
        kq_s = _dot_hi(k8 + pltpu.roll(q8, 1, 0), s0)
        v_new = beta * (v_row - eg * kq_s[0:1])
        qk = jnp.sum(q8[0:1] * k8[0:1], axis=1, keepdims=True)
        o = eg * kq_s[1:2] + qk * v_new
        vn8 = jnp.where(row == 0, v_new, 0.0)
        s_out_ref[0, h] = s0 * eg + lax.dot_general(k8, vn8, (((0,), (0,)), ((), ())), precision=HIGHEST,
                                                    preferred_element_type=F32)
        o_ref[0, :, h * DV_B:(h + 1) * DV_B] = jnp.where(row == 0, _rms(o, gn), 0.0)


def _gdn_step(q, k, v, bg, s0, g_onorm_row):
    b = q.shape[0]
    tok = lambda w: pl.BlockSpec((1, SUBLANES, w), lambda bi: (bi, 0, 0))
    sspec = pl.BlockSpec((1, H_B, DK_B, DV_B), lambda bi: (bi, 0, 0, 0))
    return pl.pallas_call(
        _gdn_step_kernel,
        grid=(b,),
        in_specs=[tok(H_B * DK_B), tok(H_B * DK_B), tok(W_B), tok(LANES), sspec,
                  pl.BlockSpec((1, DV_B), lambda bi: (0, 0))],
        out_specs=[tok(W_B), sspec],
        out_shape=[jax.ShapeDtypeStruct((b, SUBLANES, W_B), F32),
                   jax.ShapeDtypeStruct((b, H_B, DK_B, DV_B), F32)],
        compiler_params=_cparams(("parallel",)),
        name="gdn_step",
    )(q, k, v, bg, s0, g_onorm_row)


def _out_stage_kernel(x_ref, oa_ref, ob_ref, p_ref, gmix_ref, wzg_ref, wpa_ref, wpb_ref, wo_ref,
                      gple_ref, wpg_ref, wple_ref, gfin_ref, y_ref, *, final):
    x = x_ref[...]
    h = _rms(x, gmix_ref[...]).astype(BF16)
    zg = jnp.dot(h, wzg_ref[...], preferred_element_type=F32)
    za = zg[:, :W_A]
    zb = zg[:, W_A:W_A + W_B]
    d = x.shape[1]
    ga = zg[:, W_A + W_B:W_A + W_B + d]
    gb = zg[:, W_A + W_B + d:]
    ya = _dot(oa_ref[...] * _silu(za), wpa_ref[...])
    yb = _dot(ob_ref[...] * _silu(zb), wpb_ref[...])
    mixed = jax.nn.sigmoid(ga) * ya + jax.nn.sigmoid(gb) * yb
    x = x + _dot(mixed, wo_ref[...])
    gate = jax.nn.sigmoid(_dot(_rms(x, gple_ref[...]), wpg_ref[...]))
    x = x + gate * _dot(p_ref[...], wple_ref[...])
    y_ref[...] = _rms(x, gfin_ref[...]) if final else x


def _out_stage(x2d, oa, ob, p2d, gmix, wzg, wpa, wpb, wo, gple, wpg, wple, gfin, tm, final):
    n, d = x2d.shape
    row = lambda w: pl.BlockSpec((tm, w), lambda i: (i, 0))
    const = lambda a: pl.BlockSpec(a.shape, lambda i: (0, 0))
    return pl.pallas_call(
        functools.partial(_out_stage_kernel, final=final),
        grid=(n // tm,),
        in_specs=[row(d), row(W_A), row(W_B), row(p2d.shape[1]), const(gmix), const(wzg), const(wpa),
                  const(wpb), const(wo), const(gple), const(wpg), const(wple), const(gfin)],
        out_specs=row(d),
        out_shape=jax.ShapeDtypeStruct((n, d), F32),
        compiler_params=_cparams(("parallel",)),
        name="out_stage",
    )(x2d, oa, ob, p2d, gmix, wzg, wpa, wpb, wo, gple, wpg, wple, gfin)


PAGES_PER_STEP = 8


def _page_sum_kernel(pt_ref, *refs, ppb):
    k_refs, o_ref = refs[:-1], refs[-1]
    acc = None
    for r, k_ref in enumerate(k_refs):
        s = jnp.sum(k_ref[0, 0], axis=0)
        acc = s if r % ppb == 0 else acc + s
        if r % ppb == ppb - 1:
            o_ref[0, r // ppb] = acc


def _page_block_sums(cache_k, layer, pt_flat, b, n_pages):
    _, _, page, h, hd = cache_k.shape
    ppb = MOBA_BLOCK // page
    nblk = n_pages // ppb
    nps = PAGES_PER_STEP if n_pages % PAGES_PER_STEP == 0 else ppb
    in_specs = [pl.BlockSpec((1, 1, page, h, hd),
                             lambda bi, g, pt, r=r: (layer, pt[bi * n_pages + g * nps + r], 0, 0, 0))
                for r in range(nps)]
    return pl.pallas_call(
        functools.partial(_page_sum_kernel, ppb=ppb),
        grid_spec=pltpu.PrefetchScalarGridSpec(
            num_scalar_prefetch=1,
            grid=(b, n_pages // nps),
            in_specs=in_specs,
            out_specs=pl.BlockSpec((1, nps // ppb, h, hd), lambda bi, g, pt: (bi, g, 0, 0)),
        ),
        out_shape=jax.ShapeDtypeStruct((b, nblk, h, hd), F32),
        compiler_params=_cparams(("parallel", "parallel")),
        name="page_sums",
    )(pt_flat, *([cache_k] * nps))


def _sample_select_kernel(q_ref, ks_ref, o_ref, *, nblk):
    prod = ks_ref[0] * (1.0 / MOBA_BLOCK) * q_ref[0]
    lane_head = lax.broadcasted_iota(jnp.int32, (W_A, LANES), 0) // HD_A
    head = lax.broadcasted_iota(jnp.int32, (W_A, LANES), 1)
    gate = _dot_hi(prod, (lane_head == head).astype(F32))
    blk_f = lax.broadcasted_iota(jnp.int32, (nblk, LANES), 0).astype(F32)
    sels = _top3(gate, blk_f, 0, float(nblk))
    rows = lax.broadcasted_iota(jnp.int32, (SUBLANES, LANES), 0)
    out = jnp.zeros((SUBLANES, LANES), F32)
    for r, sel in enumerate(sels):
        out = jnp.where(rows == r, sel, out)
    o_ref[0] = out.astype(jnp.int32)


def _sample_select(q3, ksum):
    b, nblk, w = ksum.shape
    return pl.pallas_call(
        functools.partial(_sample_select_kernel, nblk=nblk),
        grid=(b,),
        in_specs=[pl.BlockSpec((1, 1, w), lambda bi: (bi, 0, 0)),
                  pl.BlockSpec((1, nblk, w), lambda bi: (bi, 0, 0))],
        out_specs=pl.BlockSpec((1, SUBLANES, LANES), lambda bi: (bi, 0, 0)),
        out_shape=jax.ShapeDtypeStruct((b, SUBLANES, LANES), jnp.int32),
        compiler_params=_cparams(("parallel",)),
        name="sample_select",
    )(q3, ksum)


def _paged_attn_kernel(pt_ref, top_ref, q_ref, kn_ref, vn_ref, *refs, n_s):
    k_refs, v_refs, o_ref = refs[:n_s], refs[n_s:2 * n_s], refs[2 * n_s]
    h = pl.program_id(1)
    hm = lax.broadcasted_iota(jnp.int32, (H_A, 1), 0) == h
    qm = jnp.where(hm, q_ref[0], 0.0) * (HD_A ** -0.5)
    m = jnp.sum(jnp.sum(qm * kn_ref[0], axis=1, keepdims=True), axis=0, keepdims=True)
    l = jnp.ones((1, 1), F32)
    acc = vn_ref[0]
    for k_ref, v_ref in zip(k_refs, v_refs):
        sc = jnp.sum(k_ref[0, 0] * qm[None], axis=2, keepdims=True)
        sc = jnp.where(hm[None], sc, NEG_BIG)
        m_blk = jnp.max(jnp.max(sc, axis=0), axis=0, keepdims=True)
        m_new = jnp.maximum(m, m_blk)
        alpha = jnp.exp(m - m_new)
        p = jnp.exp(sc - m_new)
        l = alpha * l + jnp.sum(jnp.sum(p, axis=0), axis=0, keepdims=True)
        acc = alpha * acc + jnp.sum(p * v_ref[0, 0], axis=0)
        m = m_new
    res = acc / l

    @pl.when(h == 0)
    def _():
        o_ref[0] = jnp.where(hm, res, 0.0)

    @pl.when(h > 0)
    def _():
        o_ref[0] = jnp.where(hm, res, o_ref[0])


def _paged_attn(q3, kn3, vn3, cache_k, cache_v, layer, pt_flat, top_flat, n_pages):
    b = q3.shape[0]
    _, _, page, h, hd = cache_k.shape
    ppb = MOBA_BLOCK // page
    n_s = MOBA_TOPK * ppb

    def page_spec(s):
        def page_map(bi, hi, pt, top):
            blk = top[(bi * MOBA_TOPK + s // ppb) * H_A + hi]
            return (layer, pt[bi * n_pages + blk * ppb + s % ppb], 0, 0, 0)
        return pl.BlockSpec((1, 1, page, h, hd), page_map)

    tok = pl.BlockSpec((1, h, hd), lambda bi, hi, pt, top: (bi, 0, 0))
    pages = [page_spec(s) for s in range(n_s)]
    return pl.pallas_call(
        functools.partial(_paged_attn_kernel, n_s=n_s),
        grid_spec=pltpu.PrefetchScalarGridSpec(
            num_scalar_prefetch=2,
            grid=(b, H_A),
            in_specs=[tok, tok, tok] + pages + pages,
            out_specs=tok,
        ),
        out_shape=jax.ShapeDtypeStruct((b, h, hd), F32),
        compiler_params=_cparams(("parallel", "arbitrary")),
        name="paged_attn",
    )(pt_flat, top_flat, q3, kn3, vn3, *([cache_k] * n_s), *([cache_v] * n_s))


def _split_weights(w_in_l, d):
    o = 0
    parts = {}
    for name, n in (("qa", W_A), ("ka", W_A), ("va", W_A), ("za", W_A), ("qb", H_B * DK_B),
                    ("kb", H_B * DK_B), ("vb", W_B), ("zb", W_B), ("beta", H_B), ("alpha", H_B),
                    ("ga", d), ("gb", d)):
        parts[name] = w_in_l[:, o:o + n]
        o += n
    w_attn = jnp.concatenate([parts["qa"], parts["ka"], parts["va"]], axis=1).astype(BF16)
    w_qkv = jnp.concatenate([parts["qb"], parts["kb"], parts["vb"]], axis=1).astype(BF16)
    w_ba = jnp.concatenate([parts["beta"], parts["alpha"],
                            jnp.zeros((d, LANES - 2 * H_B), F32)], axis=1).astype(BF16)
    w_zg = jnp.concatenate([parts["za"], parts["zb"], parts["ga"], parts["gb"]], axis=1).astype(BF16)
    return w_attn, w_qkv, w_ba, w_zg


def _lane_row(vec, offset):
    out = jnp.zeros((1, LANES), F32)
    return out.at[0, offset:offset + vec.shape[0]].set(vec.astype(F32))


def _pick_tile(n, pref):
    t = pref
    while n % t:
        t //= 2
    return t


def kernel(x_prompt, x_sample, p_prompt, p_sample, cache_k, cache_v, page_table, state_gdn_s, state_gdn_conv, g_mix, w_in, conv_w, a_log, dt_bias, g_onorm, w_pa, w_pb, w_o, g_ple, w_ple_gate, w_ple, g_final):
    bp, tp, d = x_prompt.shape
    bs, ts, _ = x_sample.shape
    depth = w_in.shape[0]
    n_pages = page_table.shape[1]
    page = cache_k.shape[2]
    assert ts == 1 and tp % MOBA_BLOCK == 0 and (n_pages * page) % MOBA_BLOCK == 0
    assert tp % GDN_CHUNK == 0

    pos_p = jnp.arange(tp, dtype=jnp.int32)
    pos_s = jnp.full((bs,), n_pages * page, dtype=jnp.int32)
    tabs_p = _rope_tables(pos_p)
    tabs_s = _rope_tables(pos_s)
    pt_flat = page_table.reshape(-1).astype(jnp.int32)
    ts_pad = SUBLANES

    xp = x_prompt.reshape(bp * tp, d)
    xs = x_sample.reshape(bs, d)
    outs = {k: [] for k in ("kp", "vp", "sp", "cp", "ks", "vs", "ss", "cs")}
    for l in range(depth):
        final = l == depth - 1
        w_attn, w_qkv, w_ba, w_zg = _split_weights(w_in[l], d)
        gmix = g_mix[l].reshape(1, d)
        al_row = _lane_row(a_log[l], H_B)
        dtb_row = _lane_row(dt_bias[l], H_B)
        gon = g_onorm[l].reshape(1, DV_B)
        wpa, wpb, wo = w_pa[l].astype(BF16), w_pb[l].astype(BF16), w_o[l].astype(BF16)
        wpg, wple = w_ple_gate[l].astype(BF16), w_ple[l].astype(BF16)
        gple = g_ple[l].reshape(1, d)
        gfin = g_final.reshape(1, d)

        tm = _pick_tile(tp, 512)
        qa, ka, va = _attn_proj(xp, gmix, w_attn, tabs_p, tm)
        conv0 = jnp.zeros((bp, SUBLANES, C_CONV), F32)
        qn, kn, vn, bg, tail = _gdn_proj(xp.reshape(bp, tp, d), gmix, w_qkv, w_ba, conv_w[l], conv0,
                                         al_row, dtb_row, _pick_tile(tp, 256), tp)
        ka3 = ka.reshape(bp, tp, W_A)
        kbar = _block_means(ka3)
        oa = _moba_prompt(qa.reshape(bp, tp, W_A), ka3, va.reshape(bp, tp, W_A), kbar)
        s0 = jnp.zeros((bp, H_B, DK_B, DV_B), F32)
        ob, s_fin = _gdn_seq(qn, kn, vn, bg, s0, gon)
        xp = _out_stage(xp, oa.reshape(bp * tp, W_A), ob.reshape(bp * tp, W_B),
                        p_prompt[l].reshape(bp * tp, -1), gmix, w_zg, wpa, wpb, wo, gple, wpg, wple,
                        gfin, _pick_tile(bp * tp, 256), final)
        outs["kp"].append(ka.reshape(bp, tp, H_A, HD_A))
        outs["vp"].append(va.reshape(bp, tp, H_A, HD_A))
        outs["sp"].append(s_fin.astype(state_gdn_s.dtype))
        outs["cp"].append(tail[:, SUBLANES - (CONV_K - 1):, :])

        qa_s, ka_s, va_s = _attn_proj(xs, gmix, w_attn, tabs_s, bs)
        xs_pad = jnp.pad(xs.reshape(bs, 1, d), ((0, 0), (0, ts_pad - 1), (0, 0)))
        conv0_s = jnp.pad(state_gdn_conv[l].astype(F32), ((0, 0), (SUBLANES - (CONV_K - 1), 0), (0, 0)))
        qn_s, kn_s, vn_s, bg_s, tail_s = _gdn_proj(xs_pad, gmix, w_qkv, w_ba, conv_w[l], conv0_s,
                                                   al_row, dtb_row, ts_pad, 1)
        ksum = _page_block_sums(cache_k, l, pt_flat, bs, n_pages)
        top = _sample_select(qa_s.reshape(bs, 1, W_A), ksum.reshape(bs, -1, W_A))
        top_flat = top[:, :MOBA_TOPK, :H_A].reshape(-1)
        oa_s = _paged_attn(qa_s.reshape(bs, H_A, HD_A), ka_s.reshape(bs, H_A, HD_A),
                           va_s.reshape(bs, H_A, HD_A), cache_k, cache_v, l, pt_flat, top_flat, n_pages)
        ob_s, s_fin_s = _gdn_step(qn_s, kn_s, vn_s, bg_s, state_gdn_s[l].astype(F32), gon)
        xs = _out_stage(xs, oa_s.reshape(bs, W_A), ob_s[:, 0, :], p_sample[l].reshape(bs, -1), gmix, w_zg,
                        wpa, wpb, wo, gple, wpg, wple, gfin, bs, final)
        outs["ks"].append(ka_s.reshape(bs, 1, H_A, HD_A))
        outs["vs"].append(va_s.reshape(bs, 1, H_A, HD_A))
        outs["ss"].append(s_fin_s.astype(state_gdn_s.dtype))
        outs["cs"].append(tail_s[:, SUBLANES - (CONV_K - 1):, :])

    y_prompt = xp.reshape(bp, tp, d)
    y_sample = xs.reshape(bs, ts, d)
    st = lambda k: jnp.stack(outs[k])
    return (y_prompt, y_sample, st("kp"), st("vp"), st("sp"), st("cp"),
            st("ks"), st("vs"), st("ss"), st("cs"))
```

```python
import functools
import math

import jax
import jax.numpy as jnp
from jax import lax
from jax.experimental import pallas as pl
from jax.experimental.pallas import tpu as pltpu

F32 = jnp.float32
BF16 = jnp.bfloat16
HIGHEST = lax.Precision.HIGHEST

H_A = 8
HD_A = 64
W_A = H_A * HD_A
ROT_DIM = HD_A // 4
ROPE_THETA = 500000.0
MOBA_BLOCK = 256
MOBA_TOPK = 3
H_B = 8
DK_B = 128
DV_B = 128
W_B = H_B * DV_B
CONV_K = 4
C_CONV = H_B * (2 * DK_B + DV_B)
GDN_CHUNK = 64
EPS = 1e-6

LANES = 128
SUBLANES = 8
MXU_DIM = 256
VMEM_LIMIT_BYTES = 56 * 1024 * 1024

NEG_BIG = -1e30
KV_GROUP = 4
V_ROWS = HD_A + 16


def _cparams(sem):
    return pltpu.CompilerParams(dimension_semantics=sem, vmem_limit_bytes=VMEM_LIMIT_BYTES)


def _rms(x, g):
    return x * lax.rsqrt(jnp.mean(x * x, axis=-1, keepdims=True) + EPS) * g


def _silu(x):
    return x * jax.nn.sigmoid(x)


def _dot(a, b):
    return jnp.dot(a.astype(BF16), b.astype(BF16), preferred_element_type=F32)


def _dot_nt(a, b):
    return lax.dot_general(a.astype(BF16), b.astype(BF16), (((1,), (1,)), ((), ())),
                           preferred_element_type=F32)


def _dot_tn(a, b):
    return lax.dot_general(a.astype(BF16), b.astype(BF16), (((0,), (0,)), ((), ())),
                           preferred_element_type=F32)


def _dot_hi(a, b):
    return jnp.dot(a, b, precision=HIGHEST, preferred_element_type=F32)


def _dot_nt_hi(a, b):
    return lax.dot_general(a, b, (((1,), (1,)), ((), ())), precision=HIGHEST,
                           preferred_element_type=F32)


def _attn_proj_kernel(x_ref, g_ref, w_ref, c_ref, sa_ref, sb_ref, q_ref, k_ref, v_ref):
    h = _rms(x_ref[...], g_ref[...]).astype(BF16)
    y = jnp.dot(h, w_ref[...], preferred_element_type=F32)
    c, sa, sb = c_ref[...], sa_ref[...], sb_ref[...]
    half = ROT_DIM // 2
    for dst, base in ((q_ref, 0), (k_ref, W_A)):
        for s in range(W_A // LANES):
            z = y[:, base + s * LANES: base + (s + 1) * LANES]
            dst[:, s * LANES:(s + 1) * LANES] = (
                z * c + pltpu.roll(z, LANES - half, 1) * sa + pltpu.roll(z, half, 1) * sb)
    v_ref[...] = y[:, 2 * W_A:]


def _rope_tables(pos):
    half = ROT_DIM // 2
    inv = jnp.power(ROPE_THETA, -jnp.arange(half, dtype=F32) / half)
    ang = pos.astype(F32)[:, None] * inv[None, :]
    cos, sin = jnp.cos(ang), jnp.sin(ang)
    t = pos.shape[0]
    ones = jnp.ones((t, HD_A - ROT_DIM), F32)
    zeros = jnp.zeros((t, HD_A - ROT_DIM), F32)
    zh = jnp.zeros((t, half), F32)
    c = jnp.concatenate([cos, cos, ones], axis=1)
    sa = jnp.concatenate([-sin, zh, zeros], axis=1)
    sb = jnp.concatenate([zh, sin, zeros], axis=1)
    rep = LANES // HD_A
    return tuple(jnp.tile(a, (1, rep)) for a in (c, sa, sb))


def _attn_proj(x2d, g, w, tabs, tm):
    n, d = x2d.shape
    t_tab = tabs[0].shape[0]
    period = t_tab // tm
    tab_spec = pl.BlockSpec((tm, LANES), lambda i: (i % period, 0))
    out_spec = pl.BlockSpec((tm, W_A), lambda i: (i, 0))
    out = jax.ShapeDtypeStruct((n, W_A), F32)
    return pl.pallas_call(
        _attn_proj_kernel,
        grid=(n // tm,),
        in_specs=[pl.BlockSpec((tm, d), lambda i: (i, 0)),
                  pl.BlockSpec((1, d), lambda i: (0, 0)),
                  pl.BlockSpec((d, 3 * W_A), lambda i: (0, 0)),
                  tab_spec, tab_spec, tab_spec],
        out_specs=[out_spec, out_spec, out_spec],
        out_shape=[out, out, out],
        compiler_params=_cparams(("parallel",)),
        name="attn_proj",
    )(x2d, g, w, *tabs)


def _gdn_proj_kernel(x_ref, g_ref, w_ref, wba_ref, cw_ref, c0_ref, al_ref, dtb_ref,
                     q_ref, k_ref, v_ref, bg_ref, tail_ref, ubuf, *, tm, t_valid, tail_row):
    i = pl.program_id(1)
    h = _rms(x_ref[0], g_ref[...]).astype(BF16)
    u = jnp.dot(h, w_ref[...], preferred_element_type=F32)

    @pl.when(i == 0)
    def _():
        ubuf[0:SUBLANES, :] = c0_ref[0]

    @pl.when(i > 0)
    def _():
        ubuf[0:SUBLANES, :] = ubuf[tm:tm + SUBLANES, :]

    ubuf[SUBLANES:SUBLANES + tm, :] = u
    cw = cw_ref[...]
    conv = u * cw[CONV_K - 1:CONV_K, :]
    for j in range(1, CONV_K):
        conv = conv + ubuf[SUBLANES - j:SUBLANES - j + tm, :] * cw[CONV_K - 1 - j:CONV_K - j, :]
    act = _silu(conv)
    if t_valid % tm != 0:
        row = i * tm + lax.broadcasted_iota(jnp.int32, (tm, 1), 0)
        valid = (row < t_valid).astype(F32)
    else:
        valid = None

    def put(ref, s, val):
        ref[0, :, s * LANES:(s + 1) * LANES] = val if valid is None else val * valid

    for s in range(H_B):
        for ref, base in ((q_ref, 0), (k_ref, H_B * DK_B)):
            z = act[:, base + s * DK_B: base + (s + 1) * DK_B]
            put(ref, s, z * lax.rsqrt(jnp.sum(z * z, axis=-1, keepdims=True) + EPS))
        put(v_ref, s, act[:, 2 * H_B * DK_B + s * DV_B: 2 * H_B * DK_B + (s + 1) * DV_B])

    ba = jnp.dot(h, wba_ref[...], preferred_element_type=F32)
    lane = lax.broadcasted_iota(jnp.int32, (1, LANES), 1)
    z = ba + dtb_ref[...]
    softplus = jnp.maximum(z, 0.0) + jnp.log1p(jnp.exp(-jnp.abs(z)))
    bg = jnp.where(lane < H_B, jax.nn.sigmoid(ba), -jnp.exp(al_ref[...]) * softplus)
    bg_ref[0] = bg if valid is None else bg * valid
    tail_ref[0] = ubuf[tail_row:tail_row + SUBLANES, :]


def _gdn_proj(x3d, g, w_qkv, w_ba, conv_w, conv0_pad, al_row, dtb_row, tm, t_valid):
    b, t, d = x3d.shape
    n_tiles = -(-t_valid // tm)
    tail_row = t_valid - (n_tiles - 1) * tm
    kern = functools.partial(_gdn_proj_kernel, tm=tm, t_valid=t_valid, tail_row=tail_row)
    seq = lambda w: pl.BlockSpec((1, tm, w), lambda bi, i: (bi, i, 0))
    const = lambda r, c: pl.BlockSpec((r, c), lambda bi, i: (0, 0))
    return pl.pallas_call(
        kern,
        grid=(b, n_tiles),
        in_specs=[seq(d), const(1, d), const(d, C_CONV), const(d, LANES), const(CONV_K, C_CONV),
                  pl.BlockSpec((1, SUBLANES, C_CONV), lambda bi, i: (bi, 0, 0)),
                  const(1, LANES), const(1, LANES)],
        out_specs=[seq(H_B * DK_B), seq(H_B * DK_B), seq(W_B), seq(LANES),
                   pl.BlockSpec((1, SUBLANES, C_CONV), lambda bi, i: (bi, 0, 0))],
        out_shape=[jax.ShapeDtypeStruct((b, n_tiles * tm, H_B * DK_B), F32),
                   jax.ShapeDtypeStruct((b, n_tiles * tm, H_B * DK_B), F32),
                   jax.ShapeDtypeStruct((b, n_tiles * tm, W_B), F32),
                   jax.ShapeDtypeStruct((b, n_tiles * tm, LANES), F32),
                   jax.ShapeDtypeStruct((b, SUBLANES, C_CONV), F32)],
        scratch_shapes=[pltpu.VMEM((tm + 2 * SUBLANES, C_CONV), F32)],
        compiler_params=_cparams(("parallel", "arbitrary")),
        name="gdn_proj",
    )(x3d, g, w_qkv, w_ba, conv_w, conv0_pad, al_row, dtb_row)


def _block_mean_kernel(k_ref, o_ref, *, nblk):
    k = k_ref[0].reshape(nblk, MOBA_BLOCK, W_A)
    o_ref[0] = jnp.sum(k, axis=1) * (1.0 / MOBA_BLOCK)


def _block_means(k3d):
    b, t, w = k3d.shape
    nb = t // MOBA_BLOCK
    grp = SUBLANES if nb % SUBLANES == 0 else nb
    return pl.pallas_call(
        functools.partial(_block_mean_kernel, nblk=grp),
        grid=(b, nb // grp),
        in_specs=[pl.BlockSpec((1, grp * MOBA_BLOCK, w), lambda bi, i: (bi, i, 0))],
        out_specs=pl.BlockSpec((1, grp, w), lambda bi, i: (bi, i, 0)),
        out_shape=jax.ShapeDtypeStruct((b, nb, w), F32),
        compiler_params=_cparams(("parallel", "parallel")),
        name="block_means",
    )(k3d)


def _top3(gate, idx, axis, big):
    sels = []
    g = gate
    for _ in range(MOBA_TOPK):
        m = jnp.max(g, axis=axis, keepdims=True)
        ii = jnp.min(jnp.where(g == m, idx, big), axis=axis, keepdims=True)
        sels.append(ii)
        g = jnp.where(idx == ii, -jnp.inf, g)
    return sels


def _moba_prompt_kernel(q_ref, k_ref, v_ref, kbar_ref, o_ref, kb_scr, vt_scr, *, nb):
    i = pl.program_id(2)
    tq = MOBA_BLOCK
    n_heads = LANES // HD_A

    @pl.when(i == 0)
    def _():
        ones = jnp.ones((V_ROWS - HD_A, MOBA_BLOCK), BF16)

        def prep(j, carry):
            start = pl.multiple_of(j * MOBA_BLOCK, MOBA_BLOCK)
            kb_scr[j] = k_ref[0, pl.ds(start, MOBA_BLOCK), :].astype(BF16)
            v_t = v_ref[0, pl.ds(start, MOBA_BLOCK), :].T.astype(BF16)
            for hh in range(n_heads):
                vt_scr[j, hh, 0:HD_A, :] = v_t[hh * HD_A:(hh + 1) * HD_A]
                vt_scr[j, hh, HD_A:V_ROWS, :] = ones
            return carry
        lax.fori_loop(0, nb, prep, 0)
        for j in range(nb, kb_scr.shape[0]):
            kb_scr[j] = jnp.zeros((MOBA_BLOCK, LANES), BF16)
            vt_scr[j] = jnp.zeros((n_heads, V_ROWS, MOBA_BLOCK), BF16)

    q_t = q_ref[0].T
    kbar = kbar_ref[0]
    dim_head = lax.broadcasted_iota(jnp.int32, (LANES, 1), 0) // HD_A
    blk = lax.broadcasted_iota(jnp.int32, (nb, 1), 0)
    blk_f = blk.astype(F32)
    key_i = lax.broadcasted_iota(jnp.int32, (tq, tq), 0)
    qry_i = lax.broadcasted_iota(jnp.int32, (tq, tq), 1)
    heads = range(n_heads)

    def attend(j0, n_blk, keep, prev):
        raws = [[jnp.dot(kb_scr[j0 + g], qs[hh], preferred_element_type=F32) for hh in heads]
                for g in range(n_blk)]
        ms = [None if prev is None else prev[hh][0] for hh in heads]
        m_at = [[None] * n_blk for _ in heads]
        pvs = [[None] * n_blk for _ in heads]
        for g in range(n_blk):
            for hh in heads:
                s = jnp.where(keep[hh][g], raws[g][hh], NEG_BIG)
                m = jnp.max(s, axis=0, keepdims=True)
                if ms[hh] is not None:
                    m = jnp.maximum(ms[hh], m)
                p = jnp.exp2(s - m).astype(BF16)
                pvs[hh][g] = jnp.dot(vt_scr[j0 + g, hh], p, preferred_element_type=F32)
                m_at[hh][g] = ms[hh] = m
        out = []
        for hh in heads:
            m_fin = ms[hh]
            accl = None if prev is None else jnp.exp2(prev[hh][0] - m_fin) * prev[hh][1]
            for g in range(n_blk):
                term = pvs[hh][g] if g == n_blk - 1 else jnp.exp2(m_at[hh][g] - m_fin) * pvs[hh][g]
                accl = term if accl is None else accl + term
            out.append((m_fin, accl))
        return out

    qts = [jnp.where(dim_head == hh, q_t, 0.0) for hh in heads]
    sels = []
    for hh in heads:
        gate = jnp.where(blk < i, _dot_hi(kbar, qts[hh]), -jnp.inf)
        sels.append(_top3(gate, blk_f, 0, float(nb)))
    qs = [(qts[hh] * (HD_A ** -0.5 * math.log2(math.e))).astype(BF16) for hh in heads]
    causal = key_i <= qry_i
    state0 = attend(i, 1, [[causal]] * n_heads, None)

    def body(t, carry):
        j0 = t * KV_GROUP
        keep = []
        for hh in heads:
            i1, i2, i3 = sels[hh]
            jfs = [(j0 + g).astype(F32) for g in range(KV_GROUP)]
            keep.append([(i1 == jf) | (i2 == jf) | (i3 == jf) for jf in jfs])
        state = attend(j0, KV_GROUP, keep, [carry[2 * hh:2 * hh + 2] for hh in heads])
        return tuple(x for st in state for x in st)

    carry = lax.fori_loop(0, (i + KV_GROUP - 1) // KV_GROUP, body, tuple(x for st in state0 for x in st))
    o_t = jnp.concatenate([carry[2 * hh + 1][:HD_A] / carry[2 * hh + 1][HD_A:HD_A + 1] for hh in heads],
                          axis=0)
    o_ref[0] = o_t.T


def _moba_prompt(q3d, k3d, v3d, kbar):
    b, t, w = q3d.shape
    nb = t // MOBA_BLOCK
    nb_pad = -(-nb // KV_GROUP) * KV_GROUP
    n_pairs = w // LANES
    qspec = pl.BlockSpec((1, MOBA_BLOCK, LANES), lambda bi, hp, i: (bi, i, hp))
    kvspec = pl.BlockSpec((1, t, LANES), lambda bi, hp, i: (bi, 0, hp))
    return pl.pallas_call(
        functools.partial(_moba_prompt_kernel, nb=nb),
        grid=(b, n_pairs, nb),
        in_specs=[qspec, kvspec, kvspec,
                  pl.BlockSpec((1, nb, LANES), lambda bi, hp, i: (bi, 0, hp))],
        out_specs=qspec,
        out_shape=jax.ShapeDtypeStruct((b, t, w), F32),
        scratch_shapes=[pltpu.VMEM((nb_pad, MOBA_BLOCK, LANES), BF16),
                        pltpu.VMEM((nb_pad, LANES // HD_A, V_ROWS, MOBA_BLOCK), BF16)],
        compiler_params=_cparams(("parallel", "parallel", "arbitrary")),
        name="moba_prompt",
    )(q3d, k3d, v3d, kbar)


def _gdn_seq_kernel(q_ref, k_ref, v_ref, bg_ref, s0_ref, gn_ref, o_ref, s_out_ref, s_scr, *, rows):
    step = pl.program_id(1)
    n_steps = pl.num_programs(1)
    cc = GDN_CHUNK
    ncs = rows // cc

    @pl.when(step == 0)
    def _():
        s_scr[...] = s0_ref[0]

    bg = bg_ref[0]
    gn = gn_ref[...]
    ri = lax.broadcasted_iota(jnp.int32, (rows, rows), 0)
    ci = lax.broadcasted_iota(jnp.int32, (rows, rows), 1)
    same = (ri // cc) == (ci // cc)
    gc = _dot_hi((same & (ci <= ri)).astype(F32), bg)
    gl = _dot_hi(same.astype(F32), bg)
    gc_t = gc.T
    e_gc = jnp.exp(gc)
    e_tail = jnp.exp(gl - gc)
    e_tot = jnp.exp(gl)
    pr = lax.broadcasted_iota(jnp.int32, (cc, rows), 0)
    pl_i = lax.broadcasted_iota(jnp.int32, (cc, rows), 1)
    pc = pl_i % cc
    lane_blk = pl_i // cc
    causal_p = pr >= pc
    strict_p = pr > pc

    def pack(full):
        out = full[(ncs - 1) * cc:ncs * cc]
        for c in range(ncs - 2, -1, -1):
            out = jnp.where(lane_blk == c, full[c * cc:(c + 1) * cc], out)
        return out

    def pack_col(col):
        out = col[(ncs - 1) * cc:ncs * cc]
        for c in range(ncs - 2, -1, -1):
            out = jnp.where(lane_blk == c, col[c * cc:(c + 1) * cc], out)
        return out

    def bdiag(p):
        return jnp.where(same, jnp.concatenate([p] * ncs, axis=0), 0.0)

    heads = range(H_B)
    col = lambda a, h: a[:, H_B + h:H_B + h + 1]
    qs = [q_ref[0, :, h * DK_B:(h + 1) * DK_B] * (DK_B ** -0.5) for h in heads]
    ks = [k_ref[0, :, h * DK_B:(h + 1) * DK_B] for h in heads]
    kbetas = [ks[h] * bg[:, h:h + 1] for h in heads]
    fulls = [_dot_nt(jnp.concatenate([kbetas[h], qs[h]], axis=0), ks[h]) for h in heads]
    decays = [jnp.exp(jnp.where(causal_p, pack_col(col(gc, h)) - gc_t[H_B + h:H_B + h + 1, :], -jnp.inf))
              for h in heads]
    attns = [jnp.where(causal_p, pack(fulls[h][rows:]) * decays[h], 0.0) for h in heads]
    es = [-jnp.where(strict_p, pack(fulls[h][:rows]) * decays[h], 0.0) for h in heads]
    pws = [_dot(es[h], bdiag(es[h])) for h in heads]
    for _ in range(int(math.log2(cc)) - 2):
        rs_ = [_dot(jnp.concatenate([es[h], pws[h]], axis=0), bdiag(pws[h])) for h in heads]
        es = [es[h] + pws[h] + rs_[h][:cc] for h in heads]
        pws = [rs_[h][cc:] for h in heads]
    es = [es[h] + pws[h] + _dot(es[h], bdiag(pws[h])) for h in heads]
    rhss = [jnp.concatenate([v_ref[0, :, h * DV_B:(h + 1) * DV_B] * bg[:, h:h + 1],
                             kbetas[h] * col(e_gc, h)], axis=1) for h in heads]
    sols = [rhss[h] + _dot(bdiag(es[h]), rhss[h]) for h in heads]
    q_decs = [qs[h] * col(e_gc, h) for h in heads]
    k_tails = [ks[h] * col(e_tail, h) for h in heads]
    ss = [s_scr[h] for h in heads]
    for c in range(ncs):
        rs = slice(c * cc, (c + 1) * cc)
        wqs = [_dot(jnp.concatenate([sols[h][rs, DV_B:], q_decs[h][rs]], axis=0), ss[h]) for h in heads]
        v_news = [sols[h][rs, :DV_B] - wqs[h][:cc] for h in heads]
        os_ = [wqs[h][cc:] + _dot(attns[h][:, rs], v_news[h]) for h in heads]
        kvs = [_dot_tn(k_tails[h][rs], v_news[h]) for h in heads]
        ss = [ss[h] * e_tot[c * cc:c * cc + 1, H_B + h:H_B + h + 1] + kvs[h] for h in heads]
        for h in heads:
            o_ref[0, rs, h * DV_B:(h + 1) * DV_B] = _rms(os_[h], gn)
    for h in heads:
        s_scr[h] = ss[h]


    @pl.when(step == n_steps - 1)
    def _():
        s_out_ref[0] = s_scr[...]


def _gdn_seq(q, k, v, bg, s0, g_onorm_row):
    b, t, _ = q.shape
    rows = MXU_DIM
    while t % rows:
        rows //= 2
    seq = lambda w: pl.BlockSpec((1, rows, w), lambda bi, c: (bi, c, 0))
    sspec = pl.BlockSpec((1, H_B, DK_B, DV_B), lambda bi, c: (bi, 0, 0, 0))
    return pl.pallas_call(
        functools.partial(_gdn_seq_kernel, rows=rows),
        grid=(b, t // rows),
        in_specs=[seq(H_B * DK_B), seq(H_B * DK_B), seq(W_B), seq(LANES), sspec,
                  pl.BlockSpec((1, DV_B), lambda bi, c: (0, 0))],
        out_specs=[seq(W_B), sspec],
        out_shape=[jax.ShapeDtypeStruct((b, t, W_B), F32),
                   jax.ShapeDtypeStruct((b, H_B, DK_B, DV_B), F32)],
        scratch_shapes=[pltpu.VMEM((H_B, DK_B, DV_B), F32)],
        compiler_params=_cparams(("parallel", "arbitrary")),
        name="gdn_seq",
    )(q, k, v, bg, s0, g_onorm_row)


def _gdn_step_kernel(q_ref, k_ref, v_ref, bg_ref, s0_ref, gn_ref, o_ref, s_out_ref):
    bg = bg_ref[0]
    gn = gn_ref[...]
    row = lax.broadcasted_iota(jnp.int32, (SUBLANES, 1), 0)
    for h in range(H_B):
        beta = bg[0:1, h:h + 1]
        eg = jnp.exp(bg[0:1, H_B + h:H_B + h + 1])
        q8 = q_ref[0, :, h * DK_B:(h + 1) * DK_B] * (DK_B ** -0.5)
        k8 = k_ref[0, :, h * DK_B:(h + 1) * DK_B]
        v_row = v_ref[0, 0:1, h * DV_B:(h + 1) * DV_B]
        s0 = s0_ref[0, h]
        kq_s = _dot_hi(k8 + pltpu.roll(q8, 1, 0), s0)
        v_new = beta * (v_row - eg * kq_s[0:1])
        qk = jnp.sum(q8[0:1] * k8[0:1], axis=1, keepdims=True)
        o = eg * kq_s[1:2] + qk * v_new
        vn8 = jnp.where(row == 0, v_new, 0.0)
        s_out_ref[0, h] = s0 * eg + lax.dot_general(k8, vn8, (((0,), (0,)), ((), ())), precision=HIGHEST,
                                                    preferred_element_type=F32)
        o_ref[0, :, h * DV_B:(h + 1) * DV_B] = jnp.where(row == 0, _rms(o, gn), 0.0)


def _gdn_step(q, k, v, bg, s0, g_onorm_row):
    b = q.shape[0]
    tok = lambda w: pl.BlockSpec((1, SUBLANES, w), lambda bi: (bi, 0, 0))
    sspec = pl.BlockSpec((1, H_B, DK_B, DV_B), lambda bi: (bi, 0, 0, 0))
    return pl.pallas_call(
        _gdn_step_kernel,
        grid=(b,),
        in_specs=[tok(H_B * DK_B), tok(H_B * DK_B), tok(W_B), tok(LANES), sspec,
                  pl.BlockSpec((1, DV_B), lambda bi: (0, 0))],
        out_specs=[tok(W_B), sspec],
        out_shape=[jax.ShapeDtypeStruct((b, SUBLANES, W_B), F32),
                   jax.ShapeDtypeStruct((b, H_B, DK_B, DV_B), F32)],
        compiler_params=_cparams(("parallel",)),
        name="gdn_step",
    )(q, k, v, bg, s0, g_onorm_row)


def _out_stage_kernel(x_ref, oa_ref, ob_ref, p_ref, gmix_ref, wzg_ref, wpa_ref, wpb_ref, wo_ref,
                      gple_ref, wpg_ref, wple_ref, gfin_ref, y_ref, *, final):
    x = x_ref[...]
    h = _rms(x, gmix_ref[...]).astype(BF16)
    zg = jnp.dot(h, wzg_ref[...], preferred_element_type=F32)
    za = zg[:, :W_A]
    zb = zg[:, W_A:W_A + W_B]
    d = x.shape[1]
    ga = zg[:, W_A + W_B:W_A + W_B + d]
    gb = zg[:, W_A + W_B + d:]
    ya = _dot(oa_ref[...] * _silu(za), wpa_ref[...])
    yb = _dot(ob_ref[...] * _silu(zb), wpb_ref[...])
    mixed = jax.nn.sigmoid(ga) * ya + jax.nn.sigmoid(gb) * yb
    x = x + _dot(mixed, wo_ref[...])
    gate = jax.nn.sigmoid(_dot(_rms(x, gple_ref[...]), wpg_ref[...]))
    x = x + gate * _dot(p_ref[...], wple_ref[...])
    y_ref[...] = _rms(x, gfin_ref[...]) if final else x


def _out_stage(x2d, oa, ob, p2d, gmix, wzg, wpa, wpb, wo, gple, wpg, wple, gfin, tm, final):
    n, d = x2d.shape
    row = lambda w: pl.BlockSpec((tm, w), lambda i: (i, 0))
    const = lambda a: pl.BlockSpec(a.shape, lambda i: (0, 0))
    return pl.pallas_call(
        functools.partial(_out_stage_kernel, final=final),
        grid=(n // tm,),
        in_specs=[row(d), row(W_A), row(W_B), row(p2d.shape[1]), const(gmix), const(wzg), const(wpa),
                  const(wpb), const(wo), const(gple), const(wpg), const(wple), const(gfin)],
        out_specs=row(d),
        out_shape=jax.ShapeDtypeStruct((n, d), F32),
        compiler_params=_cparams(("parallel",)),
        name="out_stage",
    )(x2d, oa, ob, p2d, gmix, wzg, wpa, wpb, wo, gple, wpg, wple, gfin)


PAGES_PER_STEP = 8


def _page_scores_kernel(pt_ref, q_ref, *refs):
    k_refs, o_ref = refs[:-1], refs[-1]
    qb = q_ref[0]
    for r, k_ref in enumerate(k_refs):
        o_ref[0, r] = jnp.sum(k_ref[0, 0] * qb, axis=1)


def _page_scores(q_b, cache_kt, layer, pt_flat, n_pages):
    b = q_b.shape[0]
    _, _, h, hd, page = cache_kt.shape
    nps = PAGES_PER_STEP if n_pages % PAGES_PER_STEP == 0 else 1
    in_specs = [pl.BlockSpec((1, h, hd, page), lambda bi, g, pt: (bi, 0, 0, 0))]
    in_specs += [pl.BlockSpec((1, 1, h, hd, page),
                              lambda bi, g, pt, r=r: (layer, pt[bi * n_pages + g * nps + r], 0, 0, 0))
                 for r in range(nps)]
    return pl.pallas_call(
        _page_scores_kernel,
        grid_spec=pltpu.PrefetchScalarGridSpec(
            num_scalar_prefetch=1,
            grid=(b, n_pages // nps),
            in_specs=in_specs,
            out_specs=pl.BlockSpec((1, nps, h, page), lambda bi, g, pt: (bi, g, 0, 0)),
        ),
        out_shape=jax.ShapeDtypeStruct((b, n_pages, h, page), F32),
        compiler_params=_cparams(("parallel", "parallel")),
        name="page_scores",
    )(pt_flat, q_b, *([cache_kt] * nps))


def _sample_select_kernel(p_ref, o_ref, *, nblk, ppb):
    tot = jnp.sum(p_ref[0], axis=2, keepdims=True)
    gate = jnp.sum(tot.reshape(nblk, ppb, H_A, 1), axis=1) * (1.0 / MOBA_BLOCK)
    blk_f = lax.broadcasted_iota(jnp.int32, (nblk, 1, 1), 0).astype(F32)
    for r, sel in enumerate(_top3(gate, blk_f, 0, float(nblk))):
        o_ref[0, r] = jnp.broadcast_to(sel[0], (H_A, LANES)).astype(jnp.int32)


def _sample_select(scores, ppb):
    b, n_pages, h, page = scores.shape
    return pl.pallas_call(
        functools.partial(_sample_select_kernel, nblk=n_pages // ppb, ppb=ppb),
        grid=(b,),
        in_specs=[pl.BlockSpec((1, n_pages, h, page), lambda bi: (bi, 0, 0, 0))],
        out_specs=pl.BlockSpec((1, MOBA_TOPK, h, LANES), lambda bi: (bi, 0, 0, 0)),
        out_shape=jax.ShapeDtypeStruct((b, MOBA_TOPK, h, LANES), jnp.int32),
        compiler_params=_cparams(("parallel",)),
        name="sample_select",
    )(scores)


def _paged_attn_kernel(pt_ref, top_ref, q_ref, kn_ref, vn_ref, *refs, n_s):
    s_refs, v_refs, o_ref = refs[:n_s], refs[n_s:2 * n_s], refs[2 * n_s]
    h = pl.program_id(1)
    hm = lax.broadcasted_iota(jnp.int32, (H_A, 1), 0) == h
    scale = HD_A ** -0.5
    pick = lambda a: jnp.sum(jnp.where(hm, a, 0.0), axis=0, keepdims=True)
    own = jnp.sum(pick(q_ref[0] * kn_ref[0]), axis=1, keepdims=True) * scale
    rows = [pick(s_ref[0, 0]) * scale for s_ref in s_refs]
    m = own
    for s in rows:
        m = jnp.maximum(m, jnp.max(s, axis=1, keepdims=True))
    p_own = jnp.exp(own - m)
    ps = [jnp.exp(s - m) for s in rows]
    l = p_own + sum(jnp.sum(p, axis=1, keepdims=True) for p in ps)
    first = lax.broadcasted_iota(jnp.int32, (SUBLANES, 1), 0) == 0
    acc = sum(_dot_nt_hi(jnp.where(first, p, 0.0), v_ref[0, 0, 0]) for p, v_ref in zip(ps, v_refs))
    res = (acc[0:1] + p_own * pick(vn_ref[0])) / l

    @pl.when(h == 0)
    def _():
        o_ref[0] = jnp.where(hm, res, 0.0)

    @pl.when(h > 0)
    def _():
        o_ref[0] = jnp.where(hm, res, o_ref[0])


def _paged_attn(q3, kn3, vn3, scores, cache_vt, layer, pt_flat, top_flat, n_pages):
    b = q3.shape[0]
    _, _, h, hd, page = cache_vt.shape
    ppb = MOBA_BLOCK // page
    n_s = MOBA_TOPK * ppb

    def logical(s, bi, hi, top):
        return top[(bi * MOBA_TOPK + s // ppb) * H_A + hi] * ppb + s % ppb

    tok = pl.BlockSpec((1, h, hd), lambda bi, hi, pt, top: (bi, 0, 0))
    s_specs = [pl.BlockSpec((1, 1, h, page),
                            lambda bi, hi, pt, top, s=s: (bi, logical(s, bi, hi, top), 0, 0))
               for s in range(n_s)]
    v_specs = [pl.BlockSpec((1, 1, 1, hd, page),
                            lambda bi, hi, pt, top, s=s: (layer, pt[bi * n_pages + logical(s, bi, hi, top)],
                                                          hi, 0, 0))
               for s in range(n_s)]
    return pl.pallas_call(
        functools.partial(_paged_attn_kernel, n_s=n_s),
        grid_spec=pltpu.PrefetchScalarGridSpec(
            num_scalar_prefetch=2,
            grid=(b, H_A),
            in_specs=[tok, tok, tok] + s_specs + v_specs,
            out_specs=tok,
        ),
        out_shape=jax.ShapeDtypeStruct((b, h, hd), F32),
        compiler_params=_cparams(("parallel", "arbitrary")),
        name="paged_attn",
    )(pt_flat, top_flat, q3, kn3, vn3, *([scores] * n_s), *([cache_vt] * n_s))


def _split_weights(w_in_l, d):
    o = 0
    parts = {}
    for name, n in (("qa", W_A), ("ka", W_A), ("va", W_A), ("za", W_A), ("qb", H_B * DK_B),
                    ("kb", H_B * DK_B), ("vb", W_B), ("zb", W_B), ("beta", H_B), ("alpha", H_B),
                    ("ga", d), ("gb", d)):
        parts[name] = w_in_l[:, o:o + n]
        o += n
    w_attn = jnp.concatenate([parts["qa"], parts["ka"], parts["va"]], axis=1).astype(BF16)
    w_qkv = jnp.concatenate([parts["qb"], parts["kb"], parts["vb"]], axis=1).astype(BF16)
    w_ba = jnp.concatenate([parts["beta"], parts["alpha"],
                            jnp.zeros((d, LANES - 2 * H_B), F32)], axis=1).astype(BF16)
    w_zg = jnp.concatenate([parts["za"], parts["zb"], parts["ga"], parts["gb"]], axis=1).astype(BF16)
    return w_attn, w_qkv, w_ba, w_zg


def _lane_row(vec, offset):
    out = jnp.zeros((1, LANES), F32)
    return out.at[0, offset:offset + vec.shape[0]].set(vec.astype(F32))


def _pick_tile(n, pref):
    t = pref
    while n % t:
        t //= 2
    return t


def kernel(x_prompt, x_sample, p_prompt, p_sample, cache_k, cache_v, page_table, state_gdn_s, state_gdn_conv, g_mix, w_in, conv_w, a_log, dt_bias, g_onorm, w_pa, w_pb, w_o, g_ple, w_ple_gate, w_ple, g_final):
    bp, tp, d = x_prompt.shape
    bs, ts, _ = x_sample.shape
    depth = w_in.shape[0]
    n_pages = page_table.shape[1]
    page = cache_k.shape[2]
    assert ts == 1 and tp % MOBA_BLOCK == 0 and (n_pages * page) % MOBA_BLOCK == 0
    assert tp % GDN_CHUNK == 0

    pos_p = jnp.arange(tp, dtype=jnp.int32)
    pos_s = jnp.full((bs,), n_pages * page, dtype=jnp.int32)
    tabs_p = _rope_tables(pos_p)
    tabs_s = _rope_tables(pos_s)
    pt_flat = page_table.reshape(-1).astype(jnp.int32)
    cache_kt = jnp.transpose(cache_k, (0, 1, 3, 4, 2))
    cache_vt = jnp.transpose(cache_v, (0, 1, 3, 4, 2))
    ts_pad = SUBLANES

    xp = x_prompt.reshape(bp * tp, d)
    xs = x_sample.reshape(bs, d)
    outs = {k: [] for k in ("kp", "vp", "sp", "cp", "ks", "vs", "ss", "cs")}
    for l in range(depth):
        final = l == depth - 1
        w_attn, w_qkv, w_ba, w_zg = _split_weights(w_in[l], d)
        gmix = g_mix[l].reshape(1, d)
        al_row = _lane_row(a_log[l], H_B)
        dtb_row = _lane_row(dt_bias[l], H_B)
        gon = g_onorm[l].reshape(1, DV_B)
        wpa, wpb, wo = w_pa[l].astype(BF16), w_pb[l].astype(BF16), w_o[l].astype(BF16)
        wpg, wple = w_ple_gate[l].astype(BF16), w_ple[l].astype(BF16)
        gple = g_ple[l].reshape(1, d)
        gfin = g_final.reshape(1, d)

        tm = _pick_tile(tp, 512)
        qa, ka, va = _attn_proj(xp, gmix, w_attn, tabs_p, tm)
        conv0 = jnp.zeros((bp, SUBLANES, C_CONV), F32)
        qn, kn, vn, bg, tail = _gdn_proj(xp.reshape(bp, tp, d), gmix, w_qkv, w_ba, conv_w[l], conv0,
                                         al_row, dtb_row, _pick_tile(tp, 256), tp)
        ka3 = ka.reshape(bp, tp, W_A)
        kbar = _block_means(ka3)
        oa = _moba_prompt(qa.reshape(bp, tp, W_A), ka3, va.reshape(bp, tp, W_A), kbar)
        s0 = jnp.zeros((bp, H_B, DK_B, DV_B), F32)
        ob, s_fin = _gdn_seq(qn, kn, vn, bg, s0, gon)
        xp = _out_stage(xp, oa.reshape(bp * tp, W_A), ob.reshape(bp * tp, W_B),
                        p_prompt[l].reshape(bp * tp, -1), gmix, w_zg, wpa, wpb, wo, gple, wpg, wple,
                        gfin, _pick_tile(bp * tp, 256), final)
        outs["kp"].append(ka.reshape(bp, tp, H_A, HD_A))
        outs["vp"].append(va.reshape(bp, tp, H_A, HD_A))
        outs["sp"].append(s_fin.astype(state_gdn_s.dtype))
        outs["cp"].append(tail[:, SUBLANES - (CONV_K - 1):, :])

        qa_s, ka_s, va_s = _attn_proj(xs, gmix, w_attn, tabs_s, bs)
        xs_pad = jnp.pad(xs.reshape(bs, 1, d), ((0, 0), (0, ts_pad - 1), (0, 0)))
        conv0_s = jnp.pad(state_gdn_conv[l].astype(F32), ((0, 0), (SUBLANES - (CONV_K - 1), 0), (0, 0)))
        qn_s, kn_s, vn_s, bg_s, tail_s = _gdn_proj(xs_pad, gmix, w_qkv, w_ba, conv_w[l], conv0_s,
                                                   al_row, dtb_row, ts_pad, 1)
        q3 = qa_s.reshape(bs, H_A, HD_A)
        scores = _page_scores(jnp.broadcast_to(q3[..., None], (bs, H_A, HD_A, page)), cache_kt, l, pt_flat,
                              n_pages)
        top = _sample_select(scores, MOBA_BLOCK // page)
        top_flat = top[:, :, :, 0].reshape(-1)
        oa_s = _paged_attn(q3, ka_s.reshape(bs, H_A, HD_A), va_s.reshape(bs, H_A, HD_A), scores, cache_vt,
                           l, pt_flat, top_flat, n_pages)
        ob_s, s_fin_s = _gdn_step(qn_s, kn_s, vn_s, bg_s, state_gdn_s[l].astype(F32), gon)
        xs = _out_stage(xs, oa_s.reshape(bs, W_A), ob_s[:, 0, :], p_sample[l].reshape(bs, -1), gmix, w_zg,
                        wpa, wpb, wo, gple, wpg, wple, gfin, bs, final)
        outs["ks"].append(ka_s.reshape(bs, 1, H_A, HD_A))
        outs["vs"].append(va_s.reshape(bs, 1, H_A, HD_A))
        outs["ss"].append(s_fin_s.astype(state_gdn_s.dtype))
        outs["cs"].append(tail_s[:, SUBLANES - (CONV_K - 1):, :])

    y_prompt = xp.reshape(bp, tp, d)
    y_sample = xs.reshape(bs, ts, d)
    st = lambda k: jnp.stack(outs[k])
    return (y_prompt, y_sample, st("kp"), st("vp"), st("sp"), st("cp"),
            st("ks"), st("vs"), st("ss"), st("cs"))
```

```python
import functools
import math

import jax
import jax.numpy as jnp
from jax import lax
from jax.experimental import pallas as pl
from jax.experimental.pallas import tpu as pltpu

F32 = jnp.float32
BF16 = jnp.bfloat16
HIGHEST = lax.Precision.HIGHEST

H_A = 8
HD_A = 64
W_A = H_A * HD_A
ROT_DIM = HD_A // 4
ROPE_THETA = 500000.0
MOBA_BLOCK = 256
MOBA_TOPK = 3
H_B = 8
DK_B = 128
DV_B = 128
W_B = H_B * DV_B
CONV_K = 4
C_CONV = H_B * (2 * DK_B + DV_B)
GDN_CHUNK = 64
EPS = 1e-6

LANES = 128
SUBLANES = 8
MXU_DIM = 256
VMEM_LIMIT_BYTES = 56 * 1024 * 1024

NEG_BIG = -1e30
KV_GROUP = 4
PROJ_CHUNK = 512
V_ROWS = HD_A + 16


def _cparams(sem):
    return pltpu.CompilerParams(dimension_semantics=sem, vmem_limit_bytes=VMEM_LIMIT_BYTES)


def _rms(x, g):
    return x * lax.rsqrt(jnp.mean(x * x, axis=-1, keepdims=True) + EPS) * g


def _silu(x):
    return x * jax.nn.sigmoid(x)


def _dot(a, b):
    return jnp.dot(a.astype(BF16), b.astype(BF16), preferred_element_type=F32)


def _dot_nt(a, b):
    return lax.dot_general(a.astype(BF16), b.astype(BF16), (((1,), (1,)), ((), ())),
                           preferred_element_type=F32)


def _dot_tn(a, b):
    return lax.dot_general(a.astype(BF16), b.astype(BF16), (((0,), (0,)), ((), ())),
                           preferred_element_type=F32)


def _dot_hi(a, b):
    return jnp.dot(a, b, precision=HIGHEST, preferred_element_type=F32)


def _dot_nt_hi(a, b):
    return lax.dot_general(a, b, (((1,), (1,)), ((), ())), precision=HIGHEST,
                           preferred_element_type=F32)


def _attn_qkv(x_ref, g_ref, w_ref, c_ref, sa_ref, sb_ref):
    h = _rms(x_ref[...], g_ref[...]).astype(BF16)
    y = jnp.dot(h, w_ref[...], preferred_element_type=F32)
    c, sa, sb = c_ref[...], sa_ref[...], sb_ref[...]
    half = ROT_DIM // 2
    slabs = lambda base: [y[:, base + s * LANES: base + (s + 1) * LANES] for s in range(W_A // LANES)]
    rope = lambda z: z * c + pltpu.roll(z, LANES - half, 1) * sa + pltpu.roll(z, half, 1) * sb
    return [rope(z) for z in slabs(0)], [rope(z) for z in slabs(W_A)], slabs(2 * W_A)


def _attn_proj_kernel(x_ref, g_ref, w_ref, c_ref, sa_ref, sb_ref, q_ref, k_ref, v_ref):
    qs, ks, vs = _attn_qkv(x_ref, g_ref, w_ref, c_ref, sa_ref, sb_ref)
    for s in range(W_A // LANES):
        q_ref[:, s * LANES:(s + 1) * LANES] = qs[s]
        k_ref[:, s * LANES:(s + 1) * LANES] = ks[s]
        v_ref[:, s * LANES:(s + 1) * LANES] = vs[s]


def _attn_proj_seq_kernel(x_ref, g_ref, w_ref, c_ref, sa_ref, sb_ref, q_ref, kb_ref, kt_ref, vt_ref, kbar_ref):
    qs, ks, vs = _attn_qkv(x_ref, g_ref, w_ref, c_ref, sa_ref, sb_ref)
    tm = x_ref.shape[0]
    hps = LANES // HD_A
    for s in range(W_A // LANES):
        q_ref[:, s * LANES:(s + 1) * LANES] = qs[s]
        kb_ref[:, s * LANES:(s + 1) * LANES] = ks[s].astype(BF16)
        kt_ref[0, s * hps:(s + 1) * hps] = ks[s].T.reshape(hps, HD_A, tm)
        vt_ref[0, s * hps:(s + 1) * hps] = vs[s].T.reshape(hps, HD_A, tm)
        kbar_ref[0, :, s * LANES:(s + 1) * LANES] = jnp.sum(
            ks[s].reshape(tm // MOBA_BLOCK, MOBA_BLOCK, LANES), axis=1) * (1.0 / MOBA_BLOCK)


def _rope_tables(pos):
    half = ROT_DIM // 2
    inv = jnp.power(ROPE_THETA, -jnp.arange(half, dtype=F32) / half)
    ang = pos.astype(F32)[:, None] * inv[None, :]
    cos, sin = jnp.cos(ang), jnp.sin(ang)
    t = pos.shape[0]
    ones = jnp.ones((t, HD_A - ROT_DIM), F32)
    zeros = jnp.zeros((t, HD_A - ROT_DIM), F32)
    zh = jnp.zeros((t, half), F32)
    c = jnp.concatenate([cos, cos, ones], axis=1)
    sa = jnp.concatenate([-sin, zh, zeros], axis=1)
    sb = jnp.concatenate([zh, sin, zeros], axis=1)
    rep = LANES // HD_A
    return tuple(jnp.tile(a, (1, rep)) for a in (c, sa, sb))


def _attn_proj(x2d, g, w, tabs, tm):
    n, d = x2d.shape
    t_tab = tabs[0].shape[0]
    period = t_tab // tm
    tab_spec = pl.BlockSpec((tm, LANES), lambda i: (i % period, 0))
    out_spec = pl.BlockSpec((tm, W_A), lambda i: (i, 0))
    out = jax.ShapeDtypeStruct((n, W_A), F32)
    return pl.pallas_call(
        _attn_proj_kernel,
        grid=(n // tm,),
        in_specs=[pl.BlockSpec((tm, d), lambda i: (i, 0)),
                  pl.BlockSpec((1, d), lambda i: (0, 0)),
                  pl.BlockSpec((d, 3 * W_A), lambda i: (0, 0)),
                  tab_spec, tab_spec, tab_spec],
        out_specs=[out_spec, out_spec, out_spec],
        out_shape=[out, out, out],
        compiler_params=_cparams(("parallel",)),
        name="attn_proj",
    )(x2d, g, w, *tabs)


def _attn_proj_seq(x2d, g, w, tabs, tm, batch):
    n, d = x2d.shape
    t = n // batch
    period = t // tm
    bpt = tm // MOBA_BLOCK
    tab_spec = pl.BlockSpec((tm, LANES), lambda i: (i % period, 0))
    row_spec = pl.BlockSpec((tm, W_A), lambda i: (i, 0))
    t_spec = pl.BlockSpec((1, H_A, HD_A, tm), lambda i: (i // period, 0, 0, i % period))
    t_shape = jax.ShapeDtypeStruct((batch, H_A, HD_A, t), F32)
    return pl.pallas_call(
        _attn_proj_seq_kernel,
        grid=(n // tm,),
        in_specs=[pl.BlockSpec((tm, d), lambda i: (i, 0)),
                  pl.BlockSpec((1, d), lambda i: (0, 0)),
                  pl.BlockSpec((d, 3 * W_A), lambda i: (0, 0)),
                  tab_spec, tab_spec, tab_spec],
        out_specs=[row_spec, row_spec, t_spec, t_spec, pl.BlockSpec((1, bpt, W_A), lambda i: (i, 0, 0))],
        out_shape=[jax.ShapeDtypeStruct((n, W_A), F32), jax.ShapeDtypeStruct((n, W_A), BF16), t_shape, t_shape,
                   jax.ShapeDtypeStruct((n // tm, bpt, W_A), F32)],
        compiler_params=_cparams(("parallel",)),
        name="attn_proj_seq",
    )(x2d, g, w, *tabs)


def _gdn_proj_kernel(x_ref, g_ref, w_ref, wba_ref, cw_ref, c0_ref, al_ref, dtb_ref,
                     q_ref, k_ref, v_ref, bg_ref, tail_ref, ubuf, *, tm, t_valid, tail_row):
    i = pl.program_id(1)

    @pl.when(i == 0)
    def _():
        ubuf[0:SUBLANES, :] = c0_ref[0]

    @pl.when(i > 0)
    def _():
        ubuf[0:SUBLANES, :] = ubuf[tm:tm + SUBLANES, :]

    h = _rms(x_ref[0], g_ref[...]).astype(BF16)
    if t_valid % tm != 0:
        row = i * tm + lax.broadcasted_iota(jnp.int32, (tm, 1), 0)
        valid = (row < t_valid).astype(F32)
    else:
        valid = None

    def put(ref, s, val):
        ref[0, :, s * LANES:(s + 1) * LANES] = val if valid is None else val * valid

    cw = cw_ref[...]
    n_chunks = C_CONV // PROJ_CHUNK
    dot_chunk = lambda c: jnp.dot(h, w_ref[:, c * PROJ_CHUNK:(c + 1) * PROJ_CHUNK], preferred_element_type=F32)
    nxt = dot_chunk(0)
    for c in range(n_chunks):
        u = nxt
        if c + 1 < n_chunks:
            nxt = dot_chunk(c + 1)
        cols = slice(c * PROJ_CHUNK, (c + 1) * PROJ_CHUNK)
        ubuf[SUBLANES:SUBLANES + tm, cols] = u
        conv = u * cw[CONV_K - 1:CONV_K, cols]
        for j in range(1, CONV_K):
            conv = conv + ubuf[SUBLANES - j:SUBLANES - j + tm, cols] * cw[CONV_K - 1 - j:CONV_K - j, cols]
        act = _silu(conv)
        for s in range(PROJ_CHUNK // LANES):
            slab = c * (PROJ_CHUNK // LANES) + s
            z = act[:, s * LANES:(s + 1) * LANES]
            if slab < 2 * H_B:
                z = z * lax.rsqrt(jnp.sum(z * z, axis=-1, keepdims=True) + EPS)
            put((q_ref, k_ref, v_ref)[slab // H_B], slab % H_B, z)

    ba = jnp.dot(h, wba_ref[...], preferred_element_type=F32)
    lane = lax.broadcasted_iota(jnp.int32, (1, LANES), 1)
    z = ba + dtb_ref[...]
    softplus = jnp.maximum(z, 0.0) + jnp.log1p(jnp.exp(-jnp.abs(z)))
    bg = jnp.where(lane < H_B, jax.nn.sigmoid(ba), -jnp.exp(al_ref[...]) * softplus)
    bg_ref[0] = bg if valid is None else bg * valid
    tail_ref[0] = ubuf[tail_row:tail_row + SUBLANES, :]


def _gdn_proj(x3d, g, w_qkv, w_ba, conv_w, conv0_pad, al_row, dtb_row, tm, t_valid):
    b, t, d = x3d.shape
    n_tiles = -(-t_valid // tm)
    tail_row = t_valid - (n_tiles - 1) * tm
    kern = functools.partial(_gdn_proj_kernel, tm=tm, t_valid=t_valid, tail_row=tail_row)
    seq = lambda w: pl.BlockSpec((1, tm, w), lambda bi, i: (bi, i, 0))
    const = lambda r, c: pl.BlockSpec((r, c), lambda bi, i: (0, 0))
    return pl.pallas_call(
        kern,
        grid=(b, n_tiles),
        in_specs=[seq(d), const(1, d), const(d, C_CONV), const(d, LANES), const(CONV_K, C_CONV),
                  pl.BlockSpec((1, SUBLANES, C_CONV), lambda bi, i: (bi, 0, 0)),
                  const(1, LANES), const(1, LANES)],
        out_specs=[seq(H_B * DK_B), seq(H_B * DK_B), seq(W_B), seq(LANES),
                   pl.BlockSpec((1, SUBLANES, C_CONV), lambda bi, i: (bi, 0, 0))],
        out_shape=[jax.ShapeDtypeStruct((b, n_tiles * tm, H_B * DK_B), F32),
                   jax.ShapeDtypeStruct((b, n_tiles * tm, H_B * DK_B), F32),
                   jax.ShapeDtypeStruct((b, n_tiles * tm, W_B), F32),
                   jax.ShapeDtypeStruct((b, n_tiles * tm, LANES), F32),
                   jax.ShapeDtypeStruct((b, SUBLANES, C_CONV), F32)],
        scratch_shapes=[pltpu.VMEM((tm + 2 * SUBLANES, C_CONV), F32)],
        compiler_params=_cparams(("parallel", "arbitrary")),
        name="gdn_proj",
    )(x3d, g, w_qkv, w_ba, conv_w, conv0_pad, al_row, dtb_row)


def _top3(gate, idx, axis, big):
    sels = []
    g = gate
    for _ in range(MOBA_TOPK):
        m = jnp.max(g, axis=axis, keepdims=True)
        ii = jnp.min(jnp.where(g == m, idx, big), axis=axis, keepdims=True)
        sels.append(ii)
        g = jnp.where(idx == ii, -jnp.inf, g)
    return sels


def _moba_prompt_kernel(q_ref, k_ref, vt_ref, kbar_ref, o_ref, vt_scr, *, nb):
    i = pl.program_id(2)
    tq = MOBA_BLOCK
    n_heads = LANES // HD_A

    @pl.when(i == 0)
    def _():
        ones = jnp.ones((V_ROWS - HD_A, MOBA_BLOCK), BF16)
        for j in range(vt_scr.shape[0]):
            for hh in range(n_heads):
                if j < nb:
                    vt_scr[j, hh, 0:HD_A, :] = vt_ref[0, hh, :, j * MOBA_BLOCK:(j + 1) * MOBA_BLOCK].astype(BF16)
                    vt_scr[j, hh, HD_A:V_ROWS, :] = ones
                else:
                    vt_scr[j, hh] = jnp.zeros((V_ROWS, MOBA_BLOCK), BF16)

    def k_block(j):
        start = pl.multiple_of(jnp.minimum(j, nb - 1) * MOBA_BLOCK, MOBA_BLOCK)
        return k_ref[0, pl.ds(start, MOBA_BLOCK), :]

    q_t = q_ref[0].T
    kbar = kbar_ref[0]
    dim_head = lax.broadcasted_iota(jnp.int32, (LANES, 1), 0) // HD_A
    blk = lax.broadcasted_iota(jnp.int32, (nb, 1), 0)
    blk_f = blk.astype(F32)
    key_i = lax.broadcasted_iota(jnp.int32, (tq, tq), 0)
    qry_i = lax.broadcasted_iota(jnp.int32, (tq, tq), 1)
    heads = range(n_heads)

    def attend(j0, n_blk, keep, prev):
        raws = [[jnp.dot(kg, qs[hh], preferred_element_type=F32) for hh in heads]
                for kg in [k_block(j0 + g) for g in range(n_blk)]]
        ms = [None if prev is None else prev[hh][0] for hh in heads]
        m_at = [[None] * n_blk for _ in heads]
        pvs = [[None] * n_blk for _ in heads]
        for g in range(n_blk):
            for hh in heads:
                s = jnp.where(keep[hh][g], raws[g][hh], NEG_BIG)
                m = jnp.max(s, axis=0, keepdims=True)
                if ms[hh] is not None:
                    m = jnp.maximum(ms[hh], m)
                p = jnp.exp2(s - m).astype(BF16)
                pvs[hh][g] = jnp.dot(vt_scr[j0 + g, hh], p, preferred_element_type=F32)
                m_at[hh][g] = ms[hh] = m
        out = []
        for hh in heads:
            m_fin = ms[hh]
            accl = None if prev is None else jnp.exp2(prev[hh][0] - m_fin) * prev[hh][1]
            for g in range(n_blk):
                term = pvs[hh][g] if g == n_blk - 1 else jnp.exp2(m_at[hh][g] - m_fin) * pvs[hh][g]
                accl = term if accl is None else accl + term
            out.append((m_fin, accl))
        return out

    qts = [jnp.where(dim_head == hh, q_t, 0.0) for hh in heads]
    sels = []
    for hh in heads:
        gate = jnp.where(blk < i, _dot_hi(kbar, qts[hh]), -jnp.inf)
        sels.append(_top3(gate, blk_f, 0, float(nb)))
    qs = [(qts[hh] * (HD_A ** -0.5 * math.log2(math.e))).astype(BF16) for hh in heads]
    causal = key_i <= qry_i
    state0 = attend(i, 1, [[causal]] * n_heads, None)

    def body(t, carry):
        j0 = t * KV_GROUP
        keep = []
        for hh in heads:
            i1, i2, i3 = sels[hh]
            jfs = [(j0 + g).astype(F32) for g in range(KV_GROUP)]
            keep.append([(i1 == jf) | (i2 == jf) | (i3 == jf) for jf in jfs])
        state = attend(j0, KV_GROUP, keep, [carry[2 * hh:2 * hh + 2] for hh in heads])
        return tuple(x for st in state for x in st)

    carry = lax.fori_loop(0, (i + KV_GROUP - 1) // KV_GROUP, body, tuple(x for st in state0 for x in st))
    o_t = jnp.concatenate([carry[2 * hh + 1][:HD_A] / carry[2 * hh + 1][HD_A:HD_A + 1] for hh in heads],
                          axis=0)
    o_ref[0] = o_t.T


def _moba_prompt(q3d, kb3d, vt4d, kbar):
    b, t, w = q3d.shape
    nb = t // MOBA_BLOCK
    nb_pad = -(-nb // KV_GROUP) * KV_GROUP
    n_pairs = w // LANES
    hpp = LANES // HD_A
    qspec = pl.BlockSpec((1, MOBA_BLOCK, LANES), lambda bi, hp, i: (bi, i, hp))
    return pl.pallas_call(
        functools.partial(_moba_prompt_kernel, nb=nb),
        grid=(b, n_pairs, nb),
        in_specs=[qspec,
                  pl.BlockSpec((1, t, LANES), lambda bi, hp, i: (bi, 0, hp)),
                  pl.BlockSpec((1, hpp, HD_A, t), lambda bi, hp, i: (bi, hp, 0, 0)),
                  pl.BlockSpec((1, nb, LANES), lambda bi, hp, i: (bi, 0, hp))],
        out_specs=qspec,
        out_shape=jax.ShapeDtypeStruct((b, t, w), F32),
        scratch_shapes=[pltpu.VMEM((nb_pad, hpp, V_ROWS, MOBA_BLOCK), BF16)],
        compiler_params=_cparams(("parallel", "parallel", "arbitrary")),
        name="moba_prompt",
    )(q3d, kb3d, vt4d, kbar)


def _gdn_seq_kernel(q_ref, k_ref, v_ref, bg_ref, s0_ref, gn_ref, o_ref, s_out_ref, s_scr, *, rows):
    step = pl.program_id(1)
    n_steps = pl.num_programs(1)
    cc = GDN_CHUNK
    ncs = rows // cc

    @pl.when(step == 0)
    def _():
        s_scr[...] = s0_ref[0]

    bg = bg_ref[0]
    gn = gn_ref[...]
    ri = lax.broadcasted_iota(jnp.int32, (rows, rows), 0)
    ci = lax.broadcasted_iota(jnp.int32, (rows, rows), 1)
    same = (ri // cc) == (ci // cc)
    gc = _dot_hi((same & (ci <= ri)).astype(F32), bg)
    gl = _dot_hi(same.astype(F32), bg)
    gc_t = gc.T
    e_gc = jnp.exp(gc)
    e_tail = jnp.exp(gl - gc)
    e_tot = jnp.exp(gl)
    pr = lax.broadcasted_iota(jnp.int32, (cc, rows), 0)
    pl_i = lax.broadcasted_iota(jnp.int32, (cc, rows), 1)
    pc = pl_i % cc
    lane_blk = pl_i // cc
    causal_p = pr >= pc
    strict_p = pr > pc

    def pack(full):
        out = full[(ncs - 1) * cc:ncs * cc]
        for c in range(ncs - 2, -1, -1):
            out = jnp.where(lane_blk == c, full[c * cc:(c + 1) * cc], out)
        return out

    def pack_col(col):
        out = col[(ncs - 1) * cc:ncs * cc]
        for c in range(ncs - 2, -1, -1):
            out = jnp.where(lane_blk == c, col[c * cc:(c + 1) * cc], out)
        return out

    def bdiag(p):
        return jnp.where(same, jnp.concatenate([p] * ncs, axis=0), 0.0)

    heads = range(H_B)
    col = lambda a, h: a[:, H_B + h:H_B + h + 1]
    qs = [q_ref[0, :, h * DK_B:(h + 1) * DK_B] * (DK_B ** -0.5) for h in heads]
    ks = [k_ref[0, :, h * DK_B:(h + 1) * DK_B] for h in heads]
    kbetas = [ks[h] * bg[:, h:h + 1] for h in heads]
    fulls = [_dot_nt(jnp.concatenate([kbetas[h], qs[h]], axis=0), ks[h]) for h in heads]
    decays = [jnp.exp(jnp.where(causal_p, pack_col(col(gc, h)) - gc_t[H_B + h:H_B + h + 1, :], -jnp.inf))
              for h in heads]
    attns = [jnp.where(causal_p, pack(fulls[h][rows:]) * decays[h], 0.0) for h in heads]
    es = [-jnp.where(strict_p, pack(fulls[h][:rows]) * decays[h], 0.0) for h in heads]
    pws = [_dot(es[h], bdiag(es[h])) for h in heads]
    for _ in range(int(math.log2(cc)) - 2):
        rs_ = [_dot(jnp.concatenate([es[h], pws[h]], axis=0), bdiag(pws[h])) for h in heads]
        es = [es[h] + pws[h] + rs_[h][:cc] for h in heads]
        pws = [rs_[h][cc:] for h in heads]
    es = [es[h] + pws[h] + _dot(es[h], bdiag(pws[h])) for h in heads]
    rhss = [jnp.concatenate([v_ref[0, :, h * DV_B:(h + 1) * DV_B] * bg[:, h:h + 1],
                             kbetas[h] * col(e_gc, h)], axis=1) for h in heads]
    sols = [rhss[h] + _dot(bdiag(es[h]), rhss[h]) for h in heads]
    q_decs = [qs[h] * col(e_gc, h) for h in heads]
    k_tails = [ks[h] * col(e_tail, h) for h in heads]
    ss = [s_scr[h] for h in heads]
    for c in range(ncs):
        rs = slice(c * cc, (c + 1) * cc)
        wqs = [_dot(jnp.concatenate([sols[h][rs, DV_B:], q_decs[h][rs]], axis=0), ss[h]) for h in heads]
        v_news = [sols[h][rs, :DV_B] - wqs[h][:cc] for h in heads]
        os_ = [wqs[h][cc:] + _dot(attns[h][:, rs], v_news[h]) for h in heads]
        kvs = [_dot_tn(k_tails[h][rs], v_news[h]) for h in heads]
        ss = [ss[h] * e_tot[c * cc:c * cc + 1, H_B + h:H_B + h + 1] + kvs[h] for h in heads]
        for h in heads:
            o_ref[0, rs, h * DV_B:(h + 1) * DV_B] = _rms(os_[h], gn)
    for h in heads:
        s_scr[h] = ss[h]


    @pl.when(step == n_steps - 1)
    def _():
        s_out_ref[0] = s_scr[...]


def _gdn_seq(q, k, v, bg, s0, g_onorm_row):
    b, t, _ = q.shape
    rows = MXU_DIM
    while t % rows:
        rows //= 2
    seq = lambda w: pl.BlockSpec((1, rows, w), lambda bi, c: (bi, c, 0))
    sspec = pl.BlockSpec((1, H_B, DK_B, DV_B), lambda bi, c: (bi, 0, 0, 0))
    return pl.pallas_call(
        functools.partial(_gdn_seq_kernel, rows=rows),
        grid=(b, t // rows),
        in_specs=[seq(H_B * DK_B), seq(H_B * DK_B), seq(W_B), seq(LANES), sspec,
                  pl.BlockSpec((1, DV_B), lambda bi, c: (0, 0))],
        out_specs=[seq(W_B), sspec],
        out_shape=[jax.ShapeDtypeStruct((b, t, W_B), F32),
                   jax.ShapeDtypeStruct((b, H_B, DK_B, DV_B), F32)],
        scratch_shapes=[pltpu.VMEM((H_B, DK_B, DV_B), F32)],
        compiler_params=_cparams(("parallel", "arbitrary")),
        name="gdn_seq",
    )(q, k, v, bg, s0, g_onorm_row)


def _gdn_step_kernel(kq_ref, v_ref, bg_ref, s0_ref, gn_ref, o_ref, s_out_ref):
    bg = bg_ref[0]
    gn = gn_ref[...]
    kq = kq_ref[0]
    row = lax.broadcasted_iota(jnp.int32, (SUBLANES, 1), 0)
    for h in range(H_B):
        beta = bg[0:1, h:h + 1]
        eg = jnp.exp(bg[0:1, H_B + h:H_B + h + 1])
        k_col = kq[:, h:h + 1]
        q_col = kq[:, H_B + h:H_B + h + 1] * (DK_B ** -0.5)
        v_row = v_ref[0, 0:1, h * DV_B:(h + 1) * DV_B]
        s0 = s0_ref[0, h]
        k_s = jnp.sum(k_col * s0, axis=0, keepdims=True)
        q_s = jnp.sum(q_col * s0, axis=0, keepdims=True)
        v_new = beta * (v_row - eg * k_s)
        qk = jnp.sum(q_col * k_col, axis=0, keepdims=True)
        o = eg * q_s + qk * v_new
        s_out_ref[0, h] = s0 * eg + k_col * v_new
        o_ref[0, :, h * DV_B:(h + 1) * DV_B] = jnp.where(row == 0, _rms(o, gn), 0.0)


def _gdn_step(kq_cols, v, bg, s0, g_onorm_row):
    b = v.shape[0]
    tok = lambda w: pl.BlockSpec((1, SUBLANES, w), lambda bi: (bi, 0, 0))
    sspec = pl.BlockSpec((1, H_B, DK_B, DV_B), lambda bi: (bi, 0, 0, 0))
    return pl.pallas_call(
        _gdn_step_kernel,
        grid=(b,),
        in_specs=[pl.BlockSpec((1, DK_B, 2 * H_B), lambda bi: (bi, 0, 0)), tok(W_B), tok(LANES), sspec,
                  pl.BlockSpec((1, DV_B), lambda bi: (0, 0))],
        out_specs=[tok(W_B), sspec],
        out_shape=[jax.ShapeDtypeStruct((b, SUBLANES, W_B), F32),
                   jax.ShapeDtypeStruct((b, H_B, DK_B, DV_B), F32)],
        compiler_params=_cparams(("parallel",)),
        name="gdn_step",
    )(kq_cols, v, bg, s0, g_onorm_row)


def _out_stage_kernel(x_ref, oa_ref, ob_ref, p_ref, gmix_ref, wzg_ref, wpa_ref, wpb_ref, wo_ref,
                      gple_ref, wpg_ref, wple_ref, gfin_ref, y_ref, *, final):
    x = x_ref[...]
    h = _rms(x, gmix_ref[...]).astype(BF16)
    zg = jnp.dot(h, wzg_ref[...], preferred_element_type=F32)
    za = zg[:, :W_A]
    zb = zg[:, W_A:W_A + W_B]
    d = x.shape[1]
    ga = zg[:, W_A + W_B:W_A + W_B + d]
    gb = zg[:, W_A + W_B + d:]
    ya = _dot(oa_ref[...] * _silu(za), wpa_ref[...])
    yb = _dot(ob_ref[...] * _silu(zb), wpb_ref[...])
    mixed = jax.nn.sigmoid(ga) * ya + jax.nn.sigmoid(gb) * yb
    x = x + _dot(mixed, wo_ref[...])
    gate = jax.nn.sigmoid(_dot(_rms(x, gple_ref[...]), wpg_ref[...]))
    x = x + gate * _dot(p_ref[...], wple_ref[...])
    y_ref[...] = _rms(x, gfin_ref[...]) if final else x


def _out_stage(x2d, oa, ob, p2d, gmix, wzg, wpa, wpb, wo, gple, wpg, wple, gfin, tm, final):
    n, d = x2d.shape
    row = lambda w: pl.BlockSpec((tm, w), lambda i: (i, 0))
    const = lambda a: pl.BlockSpec(a.shape, lambda i: (0, 0))
    return pl.pallas_call(
        functools.partial(_out_stage_kernel, final=final),
        grid=(n // tm,),
        in_specs=[row(d), row(W_A), row(W_B), row(p2d.shape[1]), const(gmix), const(wzg), const(wpa),
                  const(wpb), const(wo), const(gple), const(wpg), const(wple), const(gfin)],
        out_specs=row(d),
        out_shape=jax.ShapeDtypeStruct((n, d), F32),
        compiler_params=_cparams(("parallel",)),
        name="out_stage",
    )(x2d, oa, ob, p2d, gmix, wzg, wpa, wpb, wo, gple, wpg, wple, gfin)


PAGES_PER_STEP = 16


def _page_scores_kernel(pt_ref, q_ref, *refs):
    k_refs, o_ref = refs[:-1], refs[-1]
    qb = q_ref[0]
    for r, k_ref in enumerate(k_refs):
        o_ref[0, r] = jnp.sum(k_ref[0, 0] * qb, axis=1)


def _page_scores(q_b, cache_kt, layer, pt_flat, n_pages):
    b = q_b.shape[0]
    _, _, h, hd, page = cache_kt.shape
    nps = PAGES_PER_STEP if n_pages % PAGES_PER_STEP == 0 else 1
    in_specs = [pl.BlockSpec((1, h, hd, page), lambda bi, g, pt: (bi, 0, 0, 0))]
    in_specs += [pl.BlockSpec((1, 1, h, hd, page),
                              lambda bi, g, pt, r=r: (layer, pt[bi * n_pages + g * nps + r], 0, 0, 0))
                 for r in range(nps)]
    return pl.pallas_call(
        _page_scores_kernel,
        grid_spec=pltpu.PrefetchScalarGridSpec(
            num_scalar_prefetch=1,
            grid=(b, n_pages // nps),
            in_specs=in_specs,
            out_specs=pl.BlockSpec((1, nps, h, page), lambda bi, g, pt: (bi, g, 0, 0)),
        ),
        out_shape=jax.ShapeDtypeStruct((b, n_pages, h, page), F32),
        compiler_params=_cparams(("parallel", "parallel")),
        name="page_scores",
    )(pt_flat, q_b, *([cache_kt] * nps))


def _sample_select_kernel(p_ref, o_ref, *, nblk, ppb):
    tot = jnp.sum(p_ref[0], axis=2, keepdims=True)
    gate = jnp.sum(tot.reshape(nblk, ppb, H_A, 1), axis=1) * (1.0 / MOBA_BLOCK)
    blk_f = lax.broadcasted_iota(jnp.int32, (nblk, 1, 1), 0).astype(F32)
    for r, sel in enumerate(_top3(gate, blk_f, 0, float(nblk))):
        o_ref[0, r] = jnp.broadcast_to(sel[0], (H_A, LANES)).astype(jnp.int32)


def _sample_select(scores, ppb):
    b, n_pages, h, page = scores.shape
    return pl.pallas_call(
        functools.partial(_sample_select_kernel, nblk=n_pages // ppb, ppb=ppb),
        grid=(b,),
        in_specs=[pl.BlockSpec((1, n_pages, h, page), lambda bi: (bi, 0, 0, 0))],
        out_specs=pl.BlockSpec((1, MOBA_TOPK, h, LANES), lambda bi: (bi, 0, 0, 0)),
        out_shape=jax.ShapeDtypeStruct((b, MOBA_TOPK, h, LANES), jnp.int32),
        compiler_params=_cparams(("parallel",)),
        name="sample_select",
    )(scores)


def _paged_attn_kernel(pt_ref, top_ref, q_ref, kn_ref, vn_ref, *refs, n_s):
    s_refs, v_refs, o_ref = refs[:n_s], refs[n_s:2 * n_s], refs[2 * n_s]
    h = pl.program_id(1)
    hm = lax.broadcasted_iota(jnp.int32, (H_A, 1), 0) == h
    scale = HD_A ** -0.5
    pick = lambda a: jnp.sum(jnp.where(hm, a, 0.0), axis=0, keepdims=True)
    own = jnp.sum(pick(q_ref[0] * kn_ref[0]), axis=1, keepdims=True) * scale
    rows = [pick(s_ref[0, 0]) * scale for s_ref in s_refs]
    m = own
    for s in rows:
        m = jnp.maximum(m, jnp.max(s, axis=1, keepdims=True))
    p_own = jnp.exp(own - m)
    ps = [jnp.exp(s - m) for s in rows]
    l = p_own + sum(jnp.sum(p, axis=1, keepdims=True) for p in ps)
    acc = sum(p * v_ref[0, 0, 0] for p, v_ref in zip(ps, v_refs))
    lane_h = lax.broadcasted_iota(jnp.int32, (1, H_A), 1) == h
    vn_col = jnp.sum(jnp.where(lane_h, vn_ref[0], 0.0), axis=1, keepdims=True)
    res = (jnp.sum(acc, axis=1, keepdims=True) + p_own * vn_col) / l

    @pl.when(h == 0)
    def _():
        o_ref[0] = jnp.where(lane_h, res, 0.0)

    @pl.when(h > 0)
    def _():
        o_ref[0] = jnp.where(lane_h, res, o_ref[0])


def _paged_attn(q3, kn3, vn_t, scores, cache_vt, layer, pt_flat, top_flat, n_pages):
    b = q3.shape[0]
    _, _, h, hd, page = cache_vt.shape
    ppb = MOBA_BLOCK // page
    n_s = MOBA_TOPK * ppb

    def logical(s, bi, hi, top):
        return top[(bi * MOBA_TOPK + s // ppb) * H_A + hi] * ppb + s % ppb

    tok = pl.BlockSpec((1, h, hd), lambda bi, hi, pt, top: (bi, 0, 0))
    tok_t = pl.BlockSpec((1, hd, h), lambda bi, hi, pt, top: (bi, 0, 0))
    s_specs = [pl.BlockSpec((1, 1, h, page),
                            lambda bi, hi, pt, top, s=s: (bi, logical(s, bi, hi, top), 0, 0))
               for s in range(n_s)]
    v_specs = [pl.BlockSpec((1, 1, 1, hd, page),
                            lambda bi, hi, pt, top, s=s: (layer, pt[bi * n_pages + logical(s, bi, hi, top)],
                                                          hi, 0, 0))
               for s in range(n_s)]
    return pl.pallas_call(
        functools.partial(_paged_attn_kernel, n_s=n_s),
        grid_spec=pltpu.PrefetchScalarGridSpec(
            num_scalar_prefetch=2,
            grid=(b, H_A),
            in_specs=[tok, tok, tok_t] + s_specs + v_specs,
            out_specs=tok_t,
        ),
        out_shape=jax.ShapeDtypeStruct((b, hd, h), F32),
        compiler_params=_cparams(("parallel", "arbitrary")),
        name="paged_attn",
    )(pt_flat, top_flat, q3, kn3, vn_t, *([scores] * n_s), *([cache_vt] * n_s))


def _split_weights(w_in_l, d):
    o = 0
    parts = {}
    for name, n in (("qa", W_A), ("ka", W_A), ("va", W_A), ("za", W_A), ("qb", H_B * DK_B),
                    ("kb", H_B * DK_B), ("vb", W_B), ("zb", W_B), ("beta", H_B), ("alpha", H_B),
                    ("ga", d), ("gb", d)):
        parts[name] = w_in_l[:, o:o + n]
        o += n
    w_attn = jnp.concatenate([parts["qa"], parts["ka"], parts["va"]], axis=1).astype(BF16)
    w_qkv = jnp.concatenate([parts["qb"], parts["kb"], parts["vb"]], axis=1).astype(BF16)
    w_ba = jnp.concatenate([parts["beta"], parts["alpha"],
                            jnp.zeros((d, LANES - 2 * H_B), F32)], axis=1).astype(BF16)
    w_zg = jnp.concatenate([parts["za"], parts["zb"], parts["ga"], parts["gb"]], axis=1).astype(BF16)
    return w_attn, w_qkv, w_ba, w_zg


def _lane_row(vec, offset):
    out = jnp.zeros((1, LANES), F32)
    return out.at[0, offset:offset + vec.shape[0]].set(vec.astype(F32))


def _pick_tile(n, pref):
    t = pref
    while n % t:
        t //= 2
    return t


def kernel(x_prompt, x_sample, p_prompt, p_sample, cache_k, cache_v, page_table, state_gdn_s, state_gdn_conv, g_mix, w_in, conv_w, a_log, dt_bias, g_onorm, w_pa, w_pb, w_o, g_ple, w_ple_gate, w_ple, g_final):
    bp, tp, d = x_prompt.shape
    bs, ts, _ = x_sample.shape
    depth = w_in.shape[0]
    n_pages = page_table.shape[1]
    page = cache_k.shape[2]
    assert ts == 1 and tp % MOBA_BLOCK == 0 and (n_pages * page) % MOBA_BLOCK == 0
    assert tp % GDN_CHUNK == 0

    pos_p = jnp.arange(tp, dtype=jnp.int32)
    pos_s = jnp.full((bs,), n_pages * page, dtype=jnp.int32)
    tabs_p = _rope_tables(pos_p)
    tabs_s = _rope_tables(pos_s)
    pt_flat = page_table.reshape(-1).astype(jnp.int32)
    cache_kt = jnp.transpose(cache_k, (0, 1, 3, 4, 2))
    cache_vt = jnp.transpose(cache_v, (0, 1, 3, 4, 2))
    ts_pad = SUBLANES

    xp = x_prompt.reshape(bp * tp, d)
    xs = x_sample.reshape(bs, d)
    outs = {k: [] for k in ("kp", "vp", "sp", "cp", "ks", "vs", "ss", "cs")}
    for l in range(depth):
        final = l == depth - 1
        w_attn, w_qkv, w_ba, w_zg = _split_weights(w_in[l], d)
        gmix = g_mix[l].reshape(1, d)
        al_row = _lane_row(a_log[l], H_B)
        dtb_row = _lane_row(dt_bias[l], H_B)
        gon = g_onorm[l].reshape(1, DV_B)
        wpa, wpb, wo = w_pa[l].astype(BF16), w_pb[l].astype(BF16), w_o[l].astype(BF16)
        wpg, wple = w_ple_gate[l].astype(BF16), w_ple[l].astype(BF16)
        gple = g_ple[l].reshape(1, d)
        gfin = g_final.reshape(1, d)

        qa, kb, kt, vt, kbar = _attn_proj_seq(xp, gmix, w_attn, tabs_p, _pick_tile(tp, 512), bp)
        conv0 = jnp.zeros((bp, SUBLANES, C_CONV), F32)
        qn, kn, vn, bg, tail = _gdn_proj(xp.reshape(bp, tp, d), gmix, w_qkv, w_ba, conv_w[l], conv0,
                                         al_row, dtb_row, _pick_tile(tp, 256), tp)
        oa = _moba_prompt(qa.reshape(bp, tp, W_A), kb.reshape(bp, tp, W_A), vt,
                          kbar.reshape(bp, tp // MOBA_BLOCK, W_A))
        s0 = jnp.zeros((bp, H_B, DK_B, DV_B), F32)
        ob, s_fin = _gdn_seq(qn, kn, vn, bg, s0, gon)
        xp = _out_stage(xp, oa.reshape(bp * tp, W_A), ob.reshape(bp * tp, W_B),
                        p_prompt[l].reshape(bp * tp, -1), gmix, w_zg, wpa, wpb, wo, gple, wpg, wple,
                        gfin, _pick_tile(bp * tp, 256), final)
        outs["kp"].append(jnp.transpose(kt, (0, 3, 1, 2)))
        outs["vp"].append(jnp.transpose(vt, (0, 3, 1, 2)))
        outs["sp"].append(s_fin.astype(state_gdn_s.dtype))
        outs["cp"].append(tail[:, SUBLANES - (CONV_K - 1):, :])

        qa_s, ka_s, va_s = _attn_proj(xs, gmix, w_attn, tabs_s, bs)
        xs_pad = jnp.pad(xs.reshape(bs, 1, d), ((0, 0), (0, ts_pad - 1), (0, 0)))
        conv0_s = jnp.pad(state_gdn_conv[l].astype(F32), ((0, 0), (SUBLANES - (CONV_K - 1), 0), (0, 0)))
        qn_s, kn_s, vn_s, bg_s, tail_s = _gdn_proj(xs_pad, gmix, w_qkv, w_ba, conv_w[l], conv0_s,
                                                   al_row, dtb_row, ts_pad, 1)
        q3 = qa_s.reshape(bs, H_A, HD_A)
        scores = _page_scores(jnp.broadcast_to(q3[..., None], (bs, H_A, HD_A, page)), cache_kt, l, pt_flat,
                              n_pages)
        top = _sample_select(scores, MOBA_BLOCK // page)
        top_flat = top[:, :, :, 0].reshape(-1)
        vn_t = jnp.transpose(va_s.reshape(bs, H_A, HD_A), (0, 2, 1))
        oa_t = _paged_attn(q3, ka_s.reshape(bs, H_A, HD_A), vn_t, scores, cache_vt, l, pt_flat, top_flat,
                           n_pages)
        oa_s = jnp.transpose(oa_t, (0, 2, 1))
        cols = lambda a: jnp.transpose(a[:, 0, :].reshape(bs, H_B, DK_B), (0, 2, 1))
        kq_cols = jnp.concatenate([cols(kn_s), cols(qn_s)], axis=2)
        ob_s, s_fin_s = _gdn_step(kq_cols, vn_s, bg_s, state_gdn_s[l].astype(F32), gon)
        xs = _out_stage(xs, oa_s.reshape(bs, W_A), ob_s[:, 0, :], p_sample[l].reshape(bs, -1), gmix, w_zg,
                        wpa, wpb, wo, gple, wpg, wple, gfin, bs, final)
        outs["ks"].append(ka_s.reshape(bs, 1, H_A, HD_A))
        outs["vs"].append(va_s.reshape(bs, 1, H_A, HD_A))
        outs["ss"].append(s_fin_s.astype(state_gdn_s.dtype))
        outs["cs"].append(tail_s[:, SUBLANES - (CONV_K - 1):, :])

    y_prompt = xp.reshape(bp, tp, d)
    y_sample = xs.reshape(bs, ts, d)
    st = lambda k: jnp.stack(outs[k])
    return (y_prompt, y_sample, st("kp"), st("vp"), st("sp"), st("cp"),
            st("ks"), st("vs"), st("ss"), st("cs"))
```

```python
import functools
import math

import jax
import jax.numpy as jnp
from jax import lax
from jax.experimental import pallas as pl
from jax.experimental.pallas import tpu as pltpu

F32 = jnp.float32
BF16 = jnp.bfloat16
HIGHEST = lax.Precision.HIGHEST

H_A = 8
HD_A = 64
W_A = H_A * HD_A
ROT_DIM = HD_A // 4
ROPE_THETA = 500000.0
MOBA_BLOCK = 256
MOBA_TOPK = 3
H_B = 8
DK_B = 128
DV_B = 128
W_B = H_B * DV_B
CONV_K = 4
C_CONV = H_B * (2 * DK_B + DV_B)
GDN_CHUNK = 64
EPS = 1e-6

LANES = 128
SUBLANES = 8
MXU_DIM = 256
VMEM_LIMIT_BYTES = 56 * 1024 * 1024

NEG_BIG = -1e30
KV_GROUP = 4
PROJ_CHUNK = 512
V_ROWS = HD_A + 16


def _cparams(sem):
    return pltpu.CompilerParams(dimension_semantics=sem, vmem_limit_bytes=VMEM_LIMIT_BYTES)


def _rms(x, g):
    return x * lax.rsqrt(jnp.mean(x * x, axis=-1, keepdims=True) + EPS) * g


def _silu(x):
    return x * jax.nn.sigmoid(x)


def _dot(a, b):
    return jnp.dot(a.astype(BF16), b.astype(BF16), preferred_element_type=F32)


def _dot_nt(a, b):
    return lax.dot_general(a.astype(BF16), b.astype(BF16), (((1,), (1,)), ((), ())),
                           preferred_element_type=F32)


def _dot_tn(a, b):
    return lax.dot_general(a.astype(BF16), b.astype(BF16), (((0,), (0,)), ((), ())),
                           preferred_element_type=F32)


def _dot_hi(a, b):
    return jnp.dot(a, b, precision=HIGHEST, preferred_element_type=F32)


def _dot_nt_hi(a, b):
    return lax.dot_general(a, b, (((1,), (1,)), ((), ())), precision=HIGHEST,
                           preferred_element_type=F32)


def _attn_qkv(x_ref, g_ref, w_ref, c_ref, sa_ref, sb_ref):
    h = _rms(x_ref[...], g_ref[...]).astype(BF16)
    y = jnp.dot(h, w_ref[...], preferred_element_type=F32)
    c, sa, sb = c_ref[...], sa_ref[...], sb_ref[...]
    half = ROT_DIM // 2
    slabs = lambda base: [y[:, base + s * LANES: base + (s + 1) * LANES] for s in range(W_A // LANES)]
    rope = lambda z: z * c + pltpu.roll(z, LANES - half, 1) * sa + pltpu.roll(z, half, 1) * sb
    return [rope(z) for z in slabs(0)], [rope(z) for z in slabs(W_A)], slabs(2 * W_A)


def _attn_proj_kernel(x_ref, g_ref, w_ref, c_ref, sa_ref, sb_ref, q_ref, k_ref, v_ref):
    qs, ks, vs = _attn_qkv(x_ref, g_ref, w_ref, c_ref, sa_ref, sb_ref)
    for s in range(W_A // LANES):
        q_ref[:, s * LANES:(s + 1) * LANES] = qs[s]
        k_ref[:, s * LANES:(s + 1) * LANES] = ks[s]
        v_ref[:, s * LANES:(s + 1) * LANES] = vs[s]


def _attn_proj_seq_kernel(x_ref, g_ref, w_ref, c_ref, sa_ref, sb_ref, q_ref, kb_ref, kt_ref, vt_ref, kbar_ref):
    qs, ks, vs = _attn_qkv(x_ref, g_ref, w_ref, c_ref, sa_ref, sb_ref)
    tm = x_ref.shape[0]
    hps = LANES // HD_A
    for s in range(W_A // LANES):
        q_ref[:, s * LANES:(s + 1) * LANES] = qs[s]
        kb_ref[:, s * LANES:(s + 1) * LANES] = ks[s].astype(BF16)
        kt_ref[0, s * hps:(s + 1) * hps] = ks[s].T.reshape(hps, HD_A, tm)
        vt_ref[0, s * hps:(s + 1) * hps] = vs[s].T.reshape(hps, HD_A, tm)
        kbar_ref[0, :, s * LANES:(s + 1) * LANES] = jnp.sum(
            ks[s].reshape(tm // MOBA_BLOCK, MOBA_BLOCK, LANES), axis=1) * (1.0 / MOBA_BLOCK)


def _rope_tables(pos):
    half = ROT_DIM // 2
    inv = jnp.power(ROPE_THETA, -jnp.arange(half, dtype=F32) / half)
    ang = pos.astype(F32)[:, None] * inv[None, :]
    cos, sin = jnp.cos(ang), jnp.sin(ang)
    t = pos.shape[0]
    ones = jnp.ones((t, HD_A - ROT_DIM), F32)
    zeros = jnp.zeros((t, HD_A - ROT_DIM), F32)
    zh = jnp.zeros((t, half), F32)
    c = jnp.concatenate([cos, cos, ones], axis=1)
    sa = jnp.concatenate([-sin, zh, zeros], axis=1)
    sb = jnp.concatenate([zh, sin, zeros], axis=1)
    rep = LANES // HD_A
    return tuple(jnp.tile(a, (1, rep)) for a in (c, sa, sb))


def _attn_proj(x2d, g, w, tabs, tm):
    n, d = x2d.shape
    t_tab = tabs[0].shape[0]
    period = t_tab // tm
    tab_spec = pl.BlockSpec((tm, LANES), lambda i: (i % period, 0))
    out_spec = pl.BlockSpec((tm, W_A), lambda i: (i, 0))
    out = jax.ShapeDtypeStruct((n, W_A), F32)
    return pl.pallas_call(
        _attn_proj_kernel,
        grid=(n // tm,),
        in_specs=[pl.BlockSpec((tm, d), lambda i: (i, 0)),
                  pl.BlockSpec((1, d), lambda i: (0, 0)),
                  pl.BlockSpec((d, 3 * W_A), lambda i: (0, 0)),
                  tab_spec, tab_spec, tab_spec],
        out_specs=[out_spec, out_spec, out_spec],
        out_shape=[out, out, out],
        compiler_params=_cparams(("parallel",)),
        name="attn_proj",
    )(x2d, g, w, *tabs)


def _attn_proj_seq(x2d, g, w, tabs, tm, batch):
    n, d = x2d.shape
    t = n // batch
    period = t // tm
    bpt = tm // MOBA_BLOCK
    tab_spec = pl.BlockSpec((tm, LANES), lambda i: (i % period, 0))
    row_spec = pl.BlockSpec((tm, W_A), lambda i: (i, 0))
    t_spec = pl.BlockSpec((1, H_A, HD_A, tm), lambda i: (i // period, 0, 0, i % period))
    t_shape = jax.ShapeDtypeStruct((batch, H_A, HD_A, t), F32)
    return pl.pallas_call(
        _attn_proj_seq_kernel,
        grid=(n // tm,),
        in_specs=[pl.BlockSpec((tm, d), lambda i: (i, 0)),
                  pl.BlockSpec((1, d), lambda i: (0, 0)),
                  pl.BlockSpec((d, 3 * W_A), lambda i: (0, 0)),
                  tab_spec, tab_spec, tab_spec],
        out_specs=[row_spec, row_spec, t_spec, t_spec, pl.BlockSpec((1, bpt, W_A), lambda i: (i, 0, 0))],
        out_shape=[jax.ShapeDtypeStruct((n, W_A), F32), jax.ShapeDtypeStruct((n, W_A), BF16), t_shape, t_shape,
                   jax.ShapeDtypeStruct((n // tm, bpt, W_A), F32)],
        compiler_params=_cparams(("parallel",)),
        name="attn_proj_seq",
    )(x2d, g, w, *tabs)


def _gdn_proj_kernel(x_ref, g_ref, w_ref, wba_ref, cw_ref, c0_ref, al_ref, dtb_ref,
                     q_ref, k_ref, v_ref, bg_ref, tail_ref, ubuf, *, tm, t_valid, tail_row):
    i = pl.program_id(1)

    @pl.when(i == 0)
    def _():
        ubuf[0:SUBLANES, :] = c0_ref[0]

    @pl.when(i > 0)
    def _():
        ubuf[0:SUBLANES, :] = ubuf[tm:tm + SUBLANES, :]

    h = _rms(x_ref[0], g_ref[...]).astype(BF16)
    if t_valid % tm != 0:
        row = i * tm + lax.broadcasted_iota(jnp.int32, (tm, 1), 0)
        valid = (row < t_valid).astype(F32)
    else:
        valid = None

    def put(ref, s, val):
        ref[0, :, s * LANES:(s + 1) * LANES] = val if valid is None else val * valid

    cw = cw_ref[...]
    n_chunks = C_CONV // PROJ_CHUNK
    dot_chunk = lambda c: jnp.dot(h, w_ref[:, c * PROJ_CHUNK:(c + 1) * PROJ_CHUNK], preferred_element_type=F32)
    nxt = dot_chunk(0)
    for c in range(n_chunks):
        u = nxt
        if c + 1 < n_chunks:
            nxt = dot_chunk(c + 1)
        cols = slice(c * PROJ_CHUNK, (c + 1) * PROJ_CHUNK)
        ubuf[SUBLANES:SUBLANES + tm, cols] = u
        conv = u * cw[CONV_K - 1:CONV_K, cols]
        for j in range(1, CONV_K):
            conv = conv + ubuf[SUBLANES - j:SUBLANES - j + tm, cols] * cw[CONV_K - 1 - j:CONV_K - j, cols]
        act = _silu(conv)
        for s in range(PROJ_CHUNK // LANES):
            slab = c * (PROJ_CHUNK // LANES) + s
            z = act[:, s * LANES:(s + 1) * LANES]
            if slab < 2 * H_B:
                z = z * lax.rsqrt(jnp.sum(z * z, axis=-1, keepdims=True) + EPS)
            put((q_ref, k_ref, v_ref)[slab // H_B], slab % H_B, z)

    ba = jnp.dot(h, wba_ref[...], preferred_element_type=F32)
    lane = lax.broadcasted_iota(jnp.int32, (1, LANES), 1)
    z = ba + dtb_ref[...]
    softplus = jnp.maximum(z, 0.0) + jnp.log1p(jnp.exp(-jnp.abs(z)))
    bg = jnp.where(lane < H_B, jax.nn.sigmoid(ba), -jnp.exp(al_ref[...]) * softplus)
    bg_ref[0] = bg if valid is None else bg * valid
    tail_ref[0] = ubuf[tail_row:tail_row + SUBLANES, :]


def _gdn_proj(x3d, g, w_qkv, w_ba, conv_w, conv0_pad, al_row, dtb_row, tm, t_valid):
    b, t, d = x3d.shape
    n_tiles = -(-t_valid // tm)
    tail_row = t_valid - (n_tiles - 1) * tm
    kern = functools.partial(_gdn_proj_kernel, tm=tm, t_valid=t_valid, tail_row=tail_row)
    seq = lambda w: pl.BlockSpec((1, tm, w), lambda bi, i: (bi, i, 0))
    const = lambda r, c: pl.BlockSpec((r, c), lambda bi, i: (0, 0))
    return pl.pallas_call(
        kern,
        grid=(b, n_tiles),
        in_specs=[seq(d), const(1, d), const(d, C_CONV), const(d, LANES), const(CONV_K, C_CONV),
                  pl.BlockSpec((1, SUBLANES, C_CONV), lambda bi, i: (bi, 0, 0)),
                  const(1, LANES), const(1, LANES)],
        out_specs=[seq(H_B * DK_B), seq(H_B * DK_B), seq(W_B), seq(LANES),
                   pl.BlockSpec((1, SUBLANES, C_CONV), lambda bi, i: (bi, 0, 0))],
        out_shape=[jax.ShapeDtypeStruct((b, n_tiles * tm, H_B * DK_B), F32),
                   jax.ShapeDtypeStruct((b, n_tiles * tm, H_B * DK_B), F32),
                   jax.ShapeDtypeStruct((b, n_tiles * tm, W_B), F32),
                   jax.ShapeDtypeStruct((b, n_tiles * tm, LANES), F32),
                   jax.ShapeDtypeStruct((b, SUBLANES, C_CONV), F32)],
        scratch_shapes=[pltpu.VMEM((tm + 2 * SUBLANES, C_CONV), F32)],
        compiler_params=_cparams(("parallel", "arbitrary")),
        name="gdn_proj",
    )(x3d, g, w_qkv, w_ba, conv_w, conv0_pad, al_row, dtb_row)


def _gdn_tok_kernel(x_ref, g_ref, w_ref, wba_ref, cw_ref, c0_ref, al_ref, dtb_ref,
                    q_ref, k_ref, v_ref, bg_ref, tail_ref):
    h = _rms(x_ref[...], g_ref[...]).astype(BF16)
    cw = cw_ref[...]
    for c in range(C_CONV // PROJ_CHUNK):
        cols = slice(c * PROJ_CHUNK, (c + 1) * PROJ_CHUNK)
        u = jnp.dot(h, w_ref[:, cols], preferred_element_type=F32)
        conv = u * cw[CONV_K - 1:CONV_K, cols]
        for r in range(CONV_K - 1):
            conv = conv + c0_ref[r, :, cols] * cw[r:r + 1, cols]
            if r > 0:
                tail_ref[r - 1, :, cols] = c0_ref[r, :, cols]
        tail_ref[CONV_K - 2, :, cols] = u
        act = _silu(conv)
        for s in range(PROJ_CHUNK // LANES):
            slab = c * (PROJ_CHUNK // LANES) + s
            z = act[:, s * LANES:(s + 1) * LANES]
            if slab < 2 * H_B:
                z = z * lax.rsqrt(jnp.sum(z * z, axis=-1, keepdims=True) + EPS)
            (q_ref, k_ref, v_ref)[slab // H_B][:, (slab % H_B) * LANES:(slab % H_B + 1) * LANES] = z
    ba = jnp.dot(h, wba_ref[...], preferred_element_type=F32)
    lane = lax.broadcasted_iota(jnp.int32, (1, LANES), 1)
    z = ba + dtb_ref[...]
    softplus = jnp.maximum(z, 0.0) + jnp.log1p(jnp.exp(-jnp.abs(z)))
    bg_ref[...] = jnp.where(lane < H_B, jax.nn.sigmoid(ba), -jnp.exp(al_ref[...]) * softplus)


def _gdn_tok(x2d, g, w_qkv, w_ba, conv_w, c0, al_row, dtb_row):
    b, d = x2d.shape
    full = lambda *shape: pl.BlockSpec(shape, lambda i: (0,) * len(shape))
    return pl.pallas_call(
        _gdn_tok_kernel,
        grid=(1,),
        in_specs=[full(b, d), full(1, d), full(d, C_CONV), full(d, LANES), full(CONV_K, C_CONV),
                  full(CONV_K - 1, b, C_CONV), full(1, LANES), full(1, LANES)],
        out_specs=[full(b, H_B * DK_B), full(b, H_B * DK_B), full(b, W_B), full(b, LANES),
                   full(CONV_K - 1, b, C_CONV)],
        out_shape=[jax.ShapeDtypeStruct((b, H_B * DK_B), F32), jax.ShapeDtypeStruct((b, H_B * DK_B), F32),
                   jax.ShapeDtypeStruct((b, W_B), F32), jax.ShapeDtypeStruct((b, LANES), F32),
                   jax.ShapeDtypeStruct((CONV_K - 1, b, C_CONV), F32)],
        compiler_params=_cparams(("arbitrary",)),
        name="gdn_tok",
    )(x2d, g, w_qkv, w_ba, conv_w, c0, al_row, dtb_row)


def _top3(gate, idx, axis, big):
    sels = []
    g = gate
    for _ in range(MOBA_TOPK):
        m = jnp.max(g, axis=axis, keepdims=True)
        ii = jnp.min(jnp.where(g == m, idx, big), axis=axis, keepdims=True)
        sels.append(ii)
        g = jnp.where(idx == ii, -jnp.inf, g)
    return sels


def _moba_prompt_kernel(q_ref, k_ref, vt_ref, kbar_ref, o_ref, vt_scr, *, nb):
    i = pl.program_id(2)
    tq = MOBA_BLOCK
    n_heads = LANES // HD_A

    @pl.when(i == 0)
    def _():
        ones = jnp.ones((V_ROWS - HD_A, MOBA_BLOCK), BF16)
        for j in range(vt_scr.shape[0]):
            for hh in range(n_heads):
                if j < nb:
                    vt_scr[j, hh, 0:HD_A, :] = vt_ref[0, hh, :, j * MOBA_BLOCK:(j + 1) * MOBA_BLOCK].astype(BF16)
                    vt_scr[j, hh, HD_A:V_ROWS, :] = ones
                else:
                    vt_scr[j, hh] = jnp.zeros((V_ROWS, MOBA_BLOCK), BF16)

    def k_block(j):
        start = pl.multiple_of(jnp.minimum(j, nb - 1) * MOBA_BLOCK, MOBA_BLOCK)
        return k_ref[0, pl.ds(start, MOBA_BLOCK), :]

    q_t = q_ref[0].T
    kbar = kbar_ref[0]
    dim_head = lax.broadcasted_iota(jnp.int32, (LANES, 1), 0) // HD_A
    blk = lax.broadcasted_iota(jnp.int32, (nb, 1), 0)
    blk_f = blk.astype(F32)
    key_i = lax.broadcasted_iota(jnp.int32, (tq, tq), 0)
    qry_i = lax.broadcasted_iota(jnp.int32, (tq, tq), 1)
    heads = range(n_heads)

    def attend(j0, n_blk, keep, prev):
        raws = [[jnp.dot(kg, qs[hh], preferred_element_type=F32) for hh in heads]
                for kg in [k_block(j0 + g) for g in range(n_blk)]]
        ms = [None if prev is None else prev[hh][0] for hh in heads]
        m_at = [[None] * n_blk for _ in heads]
        pvs = [[None] * n_blk for _ in heads]
        for g in range(n_blk):
            for hh in heads:
                s = jnp.where(keep[hh][g], raws[g][hh].astype(BF16), NEG_BIG)
                m = jnp.max(s, axis=0, keepdims=True)
                if ms[hh] is not None:
                    m = jnp.maximum(ms[hh], m)
                p = jnp.exp2(s - m)
                pvs[hh][g] = jnp.dot(vt_scr[j0 + g, hh], p, preferred_element_type=F32)
                m_at[hh][g] = ms[hh] = m
        out = []
        for hh in heads:
            m_fin = ms[hh]
            rescale = lambda m_old: jnp.exp2(m_old.astype(F32) - m_fin.astype(F32))
            accl = None if prev is None else rescale(prev[hh][0]) * prev[hh][1]
            for g in range(n_blk):
                term = pvs[hh][g] if g == n_blk - 1 else rescale(m_at[hh][g]) * pvs[hh][g]
                accl = term if accl is None else accl + term
            out.append((m_fin, accl))
        return out

    qts = [jnp.where(dim_head == hh, q_t, 0.0) for hh in heads]
    qs = [(qts[hh] * (HD_A ** -0.5 * math.log2(math.e))).astype(BF16) for hh in heads]
    causal = key_i <= qry_i
    state0 = attend(i, 1, [[causal]] * n_heads, None)
    sels = []
    for hh in heads:
        gate = jnp.where(blk < i, _dot_hi(kbar, qts[hh]), -jnp.inf)
        sels.append(_top3(gate, blk_f, 0, float(nb)))

    def body(t, carry):
        j0 = t * KV_GROUP
        keep = []
        for hh in heads:
            i1, i2, i3 = sels[hh]
            jfs = [(j0 + g).astype(F32) for g in range(KV_GROUP)]
            keep.append([(i1 == jf) | (i2 == jf) | (i3 == jf) for jf in jfs])
        state = attend(j0, KV_GROUP, keep, [carry[2 * hh:2 * hh + 2] for hh in heads])
        return tuple(x for st in state for x in st)

    carry = lax.fori_loop(0, (i + KV_GROUP - 1) // KV_GROUP, body, tuple(x for st in state0 for x in st))
    o_t = jnp.concatenate([carry[2 * hh + 1][:HD_A] / carry[2 * hh + 1][HD_A:HD_A + 1] for hh in heads],
                          axis=0)
    o_ref[0] = o_t.T


def _moba_prompt(q3d, kb3d, vt4d, kbar):
    b, t, w = q3d.shape
    nb = t // MOBA_BLOCK
    nb_pad = -(-nb // KV_GROUP) * KV_GROUP
    n_pairs = w // LANES
    hpp = LANES // HD_A
    qspec = pl.BlockSpec((1, MOBA_BLOCK, LANES), lambda bi, hp, i: (bi, i, hp))
    return pl.pallas_call(
        functools.partial(_moba_prompt_kernel, nb=nb),
        grid=(b, n_pairs, nb),
        in_specs=[qspec,
                  pl.BlockSpec((1, t, LANES), lambda bi, hp, i: (bi, 0, hp)),
                  pl.BlockSpec((1, hpp, HD_A, t), lambda bi, hp, i: (bi, hp, 0, 0)),
                  pl.BlockSpec((1, nb, LANES), lambda bi, hp, i: (bi, 0, hp))],
        out_specs=qspec,
        out_shape=jax.ShapeDtypeStruct((b, t, w), F32),
        scratch_shapes=[pltpu.VMEM((nb_pad, hpp, V_ROWS, MOBA_BLOCK), BF16)],
        compiler_params=_cparams(("parallel", "parallel", "arbitrary")),
        name="moba_prompt",
    )(q3d, kb3d, vt4d, kbar)


def _gdn_seq_kernel(q_ref, k_ref, v_ref, bg_ref, s0_ref, gn_ref, o_ref, s_out_ref, s_scr, *, rows):
    step = pl.program_id(1)
    n_steps = pl.num_programs(1)
    cc = GDN_CHUNK
    ncs = rows // cc

    @pl.when(step == 0)
    def _():
        s_scr[...] = s0_ref[0]

    bg = bg_ref[0]
    gn = gn_ref[...]
    ri = lax.broadcasted_iota(jnp.int32, (rows, rows), 0)
    ci = lax.broadcasted_iota(jnp.int32, (rows, rows), 1)
    same = (ri // cc) == (ci // cc)
    sum_mats = jnp.concatenate([(same & (ci <= ri)).astype(BF16), same.astype(BF16)], axis=0)
    b_hi = bg.astype(BF16)
    r_1 = bg - b_hi.astype(F32)
    b_mid = r_1.astype(BF16)
    b_lo = (r_1 - b_mid.astype(F32)).astype(BF16)
    sums = sum(jnp.dot(sum_mats, piece, preferred_element_type=F32) for piece in (b_hi, b_mid, b_lo))
    gc = sums[:rows]
    gl = sums[rows:]
    gc_t = gc.T
    e_gc = jnp.exp(gc)
    e_tail = jnp.exp(gl - gc)
    e_tot = jnp.exp(gl)
    pr = lax.broadcasted_iota(jnp.int32, (cc, rows), 0)
    pl_i = lax.broadcasted_iota(jnp.int32, (cc, rows), 1)
    pc = pl_i % cc
    lane_blk = pl_i // cc
    causal_p = pr >= pc
    strict_p = pr > pc

    def pack(full):
        out = full[(ncs - 1) * cc:ncs * cc]
        for c in range(ncs - 2, -1, -1):
            out = jnp.where(lane_blk == c, full[c * cc:(c + 1) * cc], out)
        return out

    def pack_col(col):
        out = col[(ncs - 1) * cc:ncs * cc]
        for c in range(ncs - 2, -1, -1):
            out = jnp.where(lane_blk == c, col[c * cc:(c + 1) * cc], out)
        return out

    def bdiag(p):
        return jnp.where(same, jnp.concatenate([p] * ncs, axis=0), 0.0)

    heads = range(H_B)
    col = lambda a, h: a[:, H_B + h:H_B + h + 1]
    qs = [q_ref[0, :, h * DK_B:(h + 1) * DK_B] * (DK_B ** -0.5) for h in heads]
    ks = [k_ref[0, :, h * DK_B:(h + 1) * DK_B] for h in heads]
    kbetas = [ks[h] * bg[:, h:h + 1] for h in heads]
    fulls = [_dot_nt(jnp.concatenate([kbetas[h], qs[h]], axis=0), ks[h]) for h in heads]
    decays = [jnp.exp(jnp.where(causal_p, pack_col(col(gc, h)) - gc_t[H_B + h:H_B + h + 1, :], -jnp.inf))
              for h in heads]
    attns = [jnp.where(causal_p, pack(fulls[h][rows:]) * decays[h], 0.0) for h in heads]
    es = [-jnp.where(strict_p, pack(fulls[h][:rows]) * decays[h], 0.0) for h in heads]
    pws = [_dot(es[h], bdiag(es[h])) for h in heads]
    for _ in range(int(math.log2(cc)) - 2):
        rs_ = [_dot(jnp.concatenate([es[h], pws[h]], axis=0), bdiag(pws[h])) for h in heads]
        es = [es[h] + pws[h] + rs_[h][:cc] for h in heads]
        pws = [rs_[h][cc:] for h in heads]
    es = [es[h] + pws[h] + _dot(es[h], bdiag(pws[h])) for h in heads]
    rhss = [jnp.concatenate([v_ref[0, :, h * DV_B:(h + 1) * DV_B] * bg[:, h:h + 1],
                             kbetas[h] * col(e_gc, h)], axis=1) for h in heads]
    sols = [rhss[h] + _dot(bdiag(es[h]), rhss[h]) for h in heads]
    q_decs = [qs[h] * col(e_gc, h) for h in heads]
    k_tails = [ks[h] * col(e_tail, h) for h in heads]
    ss = [s_scr[h] for h in heads]
    for c in range(ncs):
        rs = slice(c * cc, (c + 1) * cc)
        wqs = [_dot(jnp.concatenate([sols[h][rs, DV_B:], q_decs[h][rs]], axis=0), ss[h]) for h in heads]
        v_news = [sols[h][rs, :DV_B] - wqs[h][:cc] for h in heads]
        os_ = [wqs[h][cc:] + _dot(attns[h][:, rs], v_news[h]) for h in heads]
        kvs = [_dot_tn(k_tails[h][rs], v_news[h]) for h in heads]
        ss = [ss[h] * e_tot[c * cc:c * cc + 1, H_B + h:H_B + h + 1] + kvs[h] for h in heads]
        for h in heads:
            o_ref[0, rs, h * DV_B:(h + 1) * DV_B] = _rms(os_[h], gn)
    for h in heads:
        s_scr[h] = ss[h]


    @pl.when(step == n_steps - 1)
    def _():
        s_out_ref[0] = s_scr[...]


def _gdn_seq(q, k, v, bg, s0, g_onorm_row):
    b, t, _ = q.shape
    rows = MXU_DIM
    while t % rows:
        rows //= 2
    seq = lambda w: pl.BlockSpec((1, rows, w), lambda bi, c: (bi, c, 0))
    sspec = pl.BlockSpec((1, H_B, DK_B, DV_B), lambda bi, c: (bi, 0, 0, 0))
    return pl.pallas_call(
        functools.partial(_gdn_seq_kernel, rows=rows),
        grid=(b, t // rows),
        in_specs=[seq(H_B * DK_B), seq(H_B * DK_B), seq(W_B), seq(LANES), sspec,
                  pl.BlockSpec((1, DV_B), lambda bi, c: (0, 0))],
        out_specs=[seq(W_B), sspec],
        out_shape=[jax.ShapeDtypeStruct((b, t, W_B), F32),
                   jax.ShapeDtypeStruct((b, H_B, DK_B, DV_B), F32)],
        scratch_shapes=[pltpu.VMEM((H_B, DK_B, DV_B), F32)],
        compiler_params=_cparams(("parallel", "arbitrary")),
        name="gdn_seq",
    )(q, k, v, bg, s0, g_onorm_row)


def _gdn_step_kernel(kq_ref, v_ref, bg_ref, s0_ref, gn_ref, o_ref, s_out_ref):
    bg = bg_ref[0]
    gn = gn_ref[...]
    kq = kq_ref[0]
    for h in range(H_B):
        beta = bg[:, h:h + 1]
        eg = jnp.exp(bg[:, H_B + h:H_B + h + 1])
        k_col = kq[:, h:h + 1]
        q_col = kq[:, H_B + h:H_B + h + 1] * (DK_B ** -0.5)
        v_row = v_ref[0, :, h * DV_B:(h + 1) * DV_B]
        s0 = s0_ref[0, h]
        k_s = jnp.sum(k_col * s0, axis=0, keepdims=True)
        q_s = jnp.sum(q_col * s0, axis=0, keepdims=True)
        v_new = beta * (v_row - eg * k_s)
        qk = jnp.sum(q_col * k_col, axis=0, keepdims=True)
        o = eg * q_s + qk * v_new
        s_out_ref[0, h] = s0 * eg + k_col * v_new
        o_ref[0, :, h * DV_B:(h + 1) * DV_B] = _rms(o, gn)


def _gdn_step(kq_cols, v, bg, s0, g_onorm_row):
    b = v.shape[0]
    tok = lambda w: pl.BlockSpec((1, 1, w), lambda bi: (bi, 0, 0))
    sspec = pl.BlockSpec((1, H_B, DK_B, DV_B), lambda bi: (bi, 0, 0, 0))
    return pl.pallas_call(
        _gdn_step_kernel,
        grid=(b,),
        in_specs=[pl.BlockSpec((1, DK_B, 2 * H_B), lambda bi: (bi, 0, 0)), tok(W_B), tok(LANES), sspec,
                  pl.BlockSpec((1, DV_B), lambda bi: (0, 0))],
        out_specs=[tok(W_B), sspec],
        out_shape=[jax.ShapeDtypeStruct((b, 1, W_B), F32),
                   jax.ShapeDtypeStruct((b, H_B, DK_B, DV_B), F32)],
        compiler_params=_cparams(("parallel",)),
        name="gdn_step",
    )(kq_cols, v, bg, s0, g_onorm_row)


def _out_stage_kernel(x_ref, oa_ref, ob_ref, p_ref, gmix_ref, wzg_ref, wpa_ref, wpb_ref, wo_ref,
                      gple_ref, wpg_ref, wple_ref, gfin_ref, y_ref, *, final):
    x = x_ref[...]
    h = _rms(x, gmix_ref[...]).astype(BF16)
    zg = jnp.dot(h, wzg_ref[...], preferred_element_type=F32)
    za = zg[:, :W_A]
    zb = zg[:, W_A:W_A + W_B]
    d = x.shape[1]
    ga = zg[:, W_A + W_B:W_A + W_B + d]
    gb = zg[:, W_A + W_B + d:]
    ya = _dot(oa_ref[...] * _silu(za), wpa_ref[...])
    yb = _dot(ob_ref[...] * _silu(zb), wpb_ref[...])
    mixed = jax.nn.sigmoid(ga) * ya + jax.nn.sigmoid(gb) * yb
    x = x + _dot(mixed, wo_ref[...])
    gate = jax.nn.sigmoid(_dot(_rms(x, gple_ref[...]), wpg_ref[...]))
    x = x + gate * _dot(p_ref[...], wple_ref[...])
    y_ref[...] = _rms(x, gfin_ref[...]) if final else x


def _out_stage(x2d, oa, ob, p2d, gmix, wzg, wpa, wpb, wo, gple, wpg, wple, gfin, tm, final):
    n, d = x2d.shape
    row = lambda w: pl.BlockSpec((tm, w), lambda i: (i, 0))
    const = lambda a: pl.BlockSpec(a.shape, lambda i: (0, 0))
    return pl.pallas_call(
        functools.partial(_out_stage_kernel, final=final),
        grid=(n // tm,),
        in_specs=[row(d), row(W_A), row(W_B), row(p2d.shape[1]), const(gmix), const(wzg), const(wpa),
                  const(wpb), const(wo), const(gple), const(wpg), const(wple), const(gfin)],
        out_specs=row(d),
        out_shape=jax.ShapeDtypeStruct((n, d), F32),
        compiler_params=_cparams(("parallel",)),
        name="out_stage",
    )(x2d, oa, ob, p2d, gmix, wzg, wpa, wpb, wo, gple, wpg, wple, gfin)


PAGES_PER_STEP = 32


def _page_scores_kernel(pt_ref, q_ref, *refs):
    k_refs, o_ref = refs[:-1], refs[-1]
    qb = q_ref[0]
    for r, k_ref in enumerate(k_refs):
        o_ref[0, r] = jnp.sum(k_ref[0, 0] * qb, axis=1)


def _page_scores(q_b, cache_kt, layer, pt_flat, n_pages):
    b = q_b.shape[0]
    _, _, h, hd, page = cache_kt.shape
    nps = PAGES_PER_STEP if n_pages % PAGES_PER_STEP == 0 else 1
    in_specs = [pl.BlockSpec((1, h, hd, page), lambda bi, g, pt: (bi, 0, 0, 0))]
    in_specs += [pl.BlockSpec((1, 1, h, hd, page),
                              lambda bi, g, pt, r=r: (layer, pt[bi * n_pages + g * nps + r], 0, 0, 0))
                 for r in range(nps)]
    return pl.pallas_call(
        _page_scores_kernel,
        grid_spec=pltpu.PrefetchScalarGridSpec(
            num_scalar_prefetch=1,
            grid=(b, n_pages // nps),
            in_specs=in_specs,
            out_specs=pl.BlockSpec((1, nps, h, page), lambda bi, g, pt: (bi, g, 0, 0)),
        ),
        out_shape=jax.ShapeDtypeStruct((b, n_pages, h, page), F32),
        compiler_params=_cparams(("parallel", "parallel")),
        name="page_scores",
    )(pt_flat, q_b, *([cache_kt] * nps))


def _sample_select_kernel(p_ref, o_ref, *, nblk, ppb):
    tot = jnp.sum(p_ref[0], axis=2, keepdims=True)
    gate = jnp.sum(tot.reshape(nblk, ppb, H_A, 1), axis=1) * (1.0 / MOBA_BLOCK)
    blk_f = lax.broadcasted_iota(jnp.int32, (nblk, 1, 1), 0).astype(F32)
    for r, sel in enumerate(_top3(gate, blk_f, 0, float(nblk))):
        o_ref[0, r] = jnp.broadcast_to(sel[0], (H_A, LANES)).astype(jnp.int32)


def _sample_select(scores, ppb):
    b, n_pages, h, page = scores.shape
    return pl.pallas_call(
        functools.partial(_sample_select_kernel, nblk=n_pages // ppb, ppb=ppb),
        grid=(b,),
        in_specs=[pl.BlockSpec((1, n_pages, h, page), lambda bi: (bi, 0, 0, 0))],
        out_specs=pl.BlockSpec((1, MOBA_TOPK, h, LANES), lambda bi: (bi, 0, 0, 0)),
        out_shape=jax.ShapeDtypeStruct((b, MOBA_TOPK, h, LANES), jnp.int32),
        compiler_params=_cparams(("parallel",)),
        name="sample_select",
    )(scores)


def _paged_attn_kernel(pt_ref, top_ref, q_ref, kn_ref, vn_ref, *refs, n_s):
    s_refs, v_refs, o_ref = refs[:n_s], refs[n_s:2 * n_s], refs[2 * n_s]
    h = pl.program_id(1)
    hm = lax.broadcasted_iota(jnp.int32, (H_A, 1), 0) == h
    scale = HD_A ** -0.5
    pick = lambda a: jnp.sum(jnp.where(hm, a, 0.0), axis=0, keepdims=True)
    own = jnp.sum(pick(q_ref[0] * kn_ref[0]), axis=1, keepdims=True) * scale
    rows = [pick(s_ref[0, 0]) * scale for s_ref in s_refs]
    m = own
    for s in rows:
        m = jnp.maximum(m, jnp.max(s, axis=1, keepdims=True))
    p_own = jnp.exp(own - m)
    ps = [jnp.exp(s - m) for s in rows]
    l = p_own + sum(jnp.sum(p, axis=1, keepdims=True) for p in ps)
    acc = sum(p * v_ref[0, 0, 0] for p, v_ref in zip(ps, v_refs))
    lane_h = lax.broadcasted_iota(jnp.int32, (1, H_A), 1) == h
    vn_col = jnp.sum(jnp.where(lane_h, vn_ref[0], 0.0), axis=1, keepdims=True)
    res = (jnp.sum(acc, axis=1, keepdims=True) + p_own * vn_col) / l

    @pl.when(h == 0)
    def _():
        o_ref[0] = jnp.where(lane_h, res, 0.0)

    @pl.when(h > 0)
    def _():
        o_ref[0] = jnp.where(lane_h, res, o_ref[0])


def _paged_attn(q3, kn3, vn_t, scores, cache_vt, layer, pt_flat, top_flat, n_pages):
    b = q3.shape[0]
    _, _, h, hd, page = cache_vt.shape
    ppb = MOBA_BLOCK // page
    n_s = MOBA_TOPK * ppb

    def logical(s, bi, hi, top):
        return top[(bi * MOBA_TOPK + s // ppb) * H_A + hi] * ppb + s % ppb

    tok = pl.BlockSpec((1, h, hd), lambda bi, hi, pt, top: (bi, 0, 0))
    tok_t = pl.BlockSpec((1, hd, h), lambda bi, hi, pt, top: (bi, 0, 0))
    s_specs = [pl.BlockSpec((1, 1, h, page),
                            lambda bi, hi, pt, top, s=s: (bi, logical(s, bi, hi, top), 0, 0))
               for s in range(n_s)]
    v_specs = [pl.BlockSpec((1, 1, 1, hd, page),
                            lambda bi, hi, pt, top, s=s: (layer, pt[bi * n_pages + logical(s, bi, hi, top)],
                                                          hi, 0, 0))
               for s in range(n_s)]
    return pl.pallas_call(
        functools.partial(_paged_attn_kernel, n_s=n_s),
        grid_spec=pltpu.PrefetchScalarGridSpec(
            num_scalar_prefetch=2,
            grid=(b, H_A),
            in_specs=[tok, tok, tok_t] + s_specs + v_specs,
            out_specs=tok_t,
        ),
        out_shape=jax.ShapeDtypeStruct((b, hd, h), F32),
        compiler_params=_cparams(("parallel", "arbitrary")),
        name="paged_attn",
    )(pt_flat, top_flat, q3, kn3, vn_t, *([scores] * n_s), *([cache_vt] * n_s))


def _split_weights(w_in_l, d):
    o = 0
    parts = {}
    for name, n in (("qa", W_A), ("ka", W_A), ("va", W_A), ("za", W_A), ("qb", H_B * DK_B),
                    ("kb", H_B * DK_B), ("vb", W_B), ("zb", W_B), ("beta", H_B), ("alpha", H_B),
                    ("ga", d), ("gb", d)):
        parts[name] = w_in_l[:, o:o + n]
        o += n
    w_attn = jnp.concatenate([parts["qa"], parts["ka"], parts["va"]], axis=1).astype(BF16)
    w_qkv = jnp.concatenate([parts["qb"], parts["kb"], parts["vb"]], axis=1).astype(BF16)
    w_ba = jnp.concatenate([parts["beta"], parts["alpha"],
                            jnp.zeros((d, LANES - 2 * H_B), F32)], axis=1).astype(BF16)
    w_zg = jnp.concatenate([parts["za"], parts["zb"], parts["ga"], parts["gb"]], axis=1).astype(BF16)
    return w_attn, w_qkv, w_ba, w_zg


def _lane_row(vec, offset):
    out = jnp.zeros((1, LANES), F32)
    return out.at[0, offset:offset + vec.shape[0]].set(vec.astype(F32))


def _pick_tile(n, pref):
    t = pref
    while n % t:
        t //= 2
    return t


def kernel(x_prompt, x_sample, p_prompt, p_sample, cache_k, cache_v, page_table, state_gdn_s, state_gdn_conv, g_mix, w_in, conv_w, a_log, dt_bias, g_onorm, w_pa, w_pb, w_o, g_ple, w_ple_gate, w_ple, g_final):
    bp, tp, d = x_prompt.shape
    bs, ts, _ = x_sample.shape
    depth = w_in.shape[0]
    n_pages = page_table.shape[1]
    page = cache_k.shape[2]
    assert ts == 1 and tp % MOBA_BLOCK == 0 and (n_pages * page) % MOBA_BLOCK == 0
    assert tp % GDN_CHUNK == 0

    pos_p = jnp.arange(tp, dtype=jnp.int32)
    pos_s = jnp.full((bs,), n_pages * page, dtype=jnp.int32)
    tabs_p = _rope_tables(pos_p)
    tabs_s = _rope_tables(pos_s)
    pt_flat = page_table.reshape(-1).astype(jnp.int32)
    cache_kt = jnp.transpose(cache_k, (0, 1, 3, 4, 2))
    cache_vt = jnp.transpose(cache_v, (0, 1, 3, 4, 2))

    xp = x_prompt.reshape(bp * tp, d)
    xs = x_sample.reshape(bs, d)
    outs = {k: [] for k in ("kp", "vp", "sp", "cp", "ks", "vs", "ss", "cs")}
    for l in range(depth):
        final = l == depth - 1
        w_attn, w_qkv, w_ba, w_zg = _split_weights(w_in[l], d)
        gmix = g_mix[l].reshape(1, d)
        al_row = _lane_row(a_log[l], H_B)
        dtb_row = _lane_row(dt_bias[l], H_B)
        gon = g_onorm[l].reshape(1, DV_B)
        wpa, wpb, wo = w_pa[l].astype(BF16), w_pb[l].astype(BF16), w_o[l].astype(BF16)
        wpg, wple = w_ple_gate[l].astype(BF16), w_ple[l].astype(BF16)
        gple = g_ple[l].reshape(1, d)
        gfin = g_final.reshape(1, d)

        qa, kb, kt, vt, kbar = _attn_proj_seq(xp, gmix, w_attn, tabs_p, _pick_tile(tp, 512), bp)
        conv0 = jnp.zeros((bp, SUBLANES, C_CONV), F32)
        qn, kn, vn, bg, tail = _gdn_proj(xp.reshape(bp, tp, d), gmix, w_qkv, w_ba, conv_w[l], conv0,
                                         al_row, dtb_row, _pick_tile(tp, 256), tp)
        oa = _moba_prompt(qa.reshape(bp, tp, W_A), kb.reshape(bp, tp, W_A), vt,
                          kbar.reshape(bp, tp // MOBA_BLOCK, W_A))
        s0 = jnp.zeros((bp, H_B, DK_B, DV_B), F32)
        ob, s_fin = _gdn_seq(qn, kn, vn, bg, s0, gon)
        xp = _out_stage(xp, oa.reshape(bp * tp, W_A), ob.reshape(bp * tp, W_B),
                        p_prompt[l].reshape(bp * tp, -1), gmix, w_zg, wpa, wpb, wo, gple, wpg, wple,
                        gfin, _pick_tile(bp * tp, 256), final)
        outs["kp"].append(jnp.transpose(kt, (0, 3, 1, 2)))
        outs["vp"].append(jnp.transpose(vt, (0, 3, 1, 2)))
        outs["sp"].append(s_fin.astype(state_gdn_s.dtype))
        outs["cp"].append(tail[:, SUBLANES - (CONV_K - 1):, :])

        qa_s, ka_s, va_s = _attn_proj(xs, gmix, w_attn, tabs_s, bs)
        conv0_s = jnp.transpose(state_gdn_conv[l].astype(F32), (1, 0, 2))
        qn_s, kn_s, vn_s, bg_s, tail_s = _gdn_tok(xs, gmix, w_qkv, w_ba, conv_w[l], conv0_s, al_row, dtb_row)
        q3 = qa_s.reshape(bs, H_A, HD_A)
        scores = _page_scores(jnp.broadcast_to(q3[..., None], (bs, H_A, HD_A, page)), cache_kt, l, pt_flat,
                              n_pages)
        top = _sample_select(scores, MOBA_BLOCK // page)
        top_flat = top[:, :, :, 0].reshape(-1)
        vn_t = jnp.transpose(va_s.reshape(bs, H_A, HD_A), (0, 2, 1))
        oa_t = _paged_attn(q3, ka_s.reshape(bs, H_A, HD_A), vn_t, scores, cache_vt, l, pt_flat, top_flat,
                           n_pages)
        oa_s = jnp.transpose(oa_t, (0, 2, 1))
        cols = lambda a: jnp.transpose(a.reshape(bs, H_B, DK_B), (0, 2, 1))
        kq_cols = jnp.concatenate([cols(kn_s), cols(qn_s)], axis=2)
        ob_s, s_fin_s = _gdn_step(kq_cols, vn_s.reshape(bs, 1, W_B), bg_s.reshape(bs, 1, LANES),
                                  state_gdn_s[l].astype(F32), gon)
        xs = _out_stage(xs, oa_s.reshape(bs, W_A), ob_s[:, 0, :], p_sample[l].reshape(bs, -1), gmix, w_zg,
                        wpa, wpb, wo, gple, wpg, wple, gfin, bs, final)
        outs["ks"].append(ka_s.reshape(bs, 1, H_A, HD_A))
        outs["vs"].append(va_s.reshape(bs, 1, H_A, HD_A))
        outs["ss"].append(s_fin_s.astype(state_gdn_s.dtype))
        outs["cs"].append(jnp.transpose(tail_s, (1, 0, 2)))

    y_prompt = xp.reshape(bp, tp, d)
    y_sample = xs.reshape(bs, ts, d)
    st = lambda k: jnp.stack(outs[k])
    return (y_prompt, y_sample, st("kp"), st("vp"), st("sp"), st("cp"),
            st("ks"), st("vs"), st("ss"), st("cs"))
```

```python
import functools
import math

import jax
import jax.numpy as jnp
from jax import lax
from jax.experimental import pallas as pl
from jax.experimental.pallas import tpu as pltpu

F32 = jnp.float32
BF16 = jnp.bfloat16
HIGHEST = lax.Precision.HIGHEST

H_A = 8
HD_A = 64
W_A = H_A * HD_A
ROT_DIM = HD_A // 4
ROPE_THETA = 500000.0
MOBA_BLOCK = 256
MOBA_TOPK = 3
H_B = 8
DK_B = 128
DV_B = 128
W_B = H_B * DV_B
CONV_K = 4
C_CONV = H_B * (2 * DK_B + DV_B)
GDN_CHUNK = 64
EPS = 1e-6

LANES = 128
SUBLANES = 8
MXU_DIM = 256
VMEM_LIMIT_BYTES = 56 * 1024 * 1024

NEG_BIG = -1e30
KV_GROUP = 4
PROJ_CHUNK = 512
V_ROWS = HD_A + 16


def _cparams(sem):
    return pltpu.CompilerParams(dimension_semantics=sem, vmem_limit_bytes=VMEM_LIMIT_BYTES)


def _rms(x, g):
    return x * lax.rsqrt(jnp.mean(x * x, axis=-1, keepdims=True) + EPS) * g


def _silu(x):
    return x * jax.nn.sigmoid(x)


def _dot(a, b):
    return jnp.dot(a.astype(BF16), b.astype(BF16), preferred_element_type=F32)


def _dot_nt(a, b):
    return lax.dot_general(a.astype(BF16), b.astype(BF16), (((1,), (1,)), ((), ())),
                           preferred_element_type=F32)


def _dot_tn(a, b):
    return lax.dot_general(a.astype(BF16), b.astype(BF16), (((0,), (0,)), ((), ())),
                           preferred_element_type=F32)


def _dot_hi(a, b):
    return jnp.dot(a, b, precision=HIGHEST, preferred_element_type=F32)


def _dot_nt_hi(a, b):
    return lax.dot_general(a, b, (((1,), (1,)), ((), ())), precision=HIGHEST,
                           preferred_element_type=F32)


def _attn_qkv(x_ref, g_ref, w_ref, c_ref, sa_ref, sb_ref):
    h = _rms(x_ref[...], g_ref[...]).astype(BF16)
    y = jnp.dot(h, w_ref[...], preferred_element_type=F32)
    c, sa, sb = c_ref[...], sa_ref[...], sb_ref[...]
    half = ROT_DIM // 2
    slabs = lambda base: [y[:, base + s * LANES: base + (s + 1) * LANES] for s in range(W_A // LANES)]
    rope = lambda z: z * c + pltpu.roll(z, LANES - half, 1) * sa + pltpu.roll(z, half, 1) * sb
    return [rope(z) for z in slabs(0)], [rope(z) for z in slabs(W_A)], slabs(2 * W_A)


def _attn_proj_kernel(x_ref, g_ref, w_ref, c_ref, sa_ref, sb_ref, q_ref, k_ref, v_ref):
    qs, ks, vs = _attn_qkv(x_ref, g_ref, w_ref, c_ref, sa_ref, sb_ref)
    for s in range(W_A // LANES):
        q_ref[:, s * LANES:(s + 1) * LANES] = qs[s]
        k_ref[:, s * LANES:(s + 1) * LANES] = ks[s]
        v_ref[:, s * LANES:(s + 1) * LANES] = vs[s]


def _attn_proj_seq_kernel(x_ref, g_ref, w_ref, c_ref, sa_ref, sb_ref, q_ref, kb_ref, kt_ref, vt_ref, kbar_ref):
    qs, ks, vs = _attn_qkv(x_ref, g_ref, w_ref, c_ref, sa_ref, sb_ref)
    tm = x_ref.shape[0]
    hps = LANES // HD_A
    for s in range(W_A // LANES):
        q_ref[:, s * LANES:(s + 1) * LANES] = qs[s]
        kb_ref[:, s * LANES:(s + 1) * LANES] = ks[s].astype(BF16)
        kt_ref[0, s * hps:(s + 1) * hps] = ks[s].T.reshape(hps, HD_A, tm)
        vt_ref[0, s * hps:(s + 1) * hps] = vs[s].T.reshape(hps, HD_A, tm)
        kbar_ref[0, :, s * LANES:(s + 1) * LANES] = jnp.sum(
            ks[s].reshape(tm // MOBA_BLOCK, MOBA_BLOCK, LANES), axis=1) * (1.0 / MOBA_BLOCK)


def _rope_tables(pos):
    half = ROT_DIM // 2
    inv = jnp.power(ROPE_THETA, -jnp.arange(half, dtype=F32) / half)
    ang = pos.astype(F32)[:, None] * inv[None, :]
    cos, sin = jnp.cos(ang), jnp.sin(ang)
    t = pos.shape[0]
    ones = jnp.ones((t, HD_A - ROT_DIM), F32)
    zeros = jnp.zeros((t, HD_A - ROT_DIM), F32)
    zh = jnp.zeros((t, half), F32)
    c = jnp.concatenate([cos, cos, ones], axis=1)
    sa = jnp.concatenate([-sin, zh, zeros], axis=1)
    sb = jnp.concatenate([zh, sin, zeros], axis=1)
    rep = LANES // HD_A
    return tuple(jnp.tile(a, (1, rep)) for a in (c, sa, sb))


def _attn_proj(x2d, g, w, tabs, tm):
    n, d = x2d.shape
    t_tab = tabs[0].shape[0]
    period = t_tab // tm
    tab_spec = pl.BlockSpec((tm, LANES), lambda i: (i % period, 0))
    out_spec = pl.BlockSpec((tm, W_A), lambda i: (i, 0))
    out = jax.ShapeDtypeStruct((n, W_A), F32)
    return pl.pallas_call(
        _attn_proj_kernel,
        grid=(n // tm,),
        in_specs=[pl.BlockSpec((tm, d), lambda i: (i, 0)),
                  pl.BlockSpec((1, d), lambda i: (0, 0)),
                  pl.BlockSpec((d, 3 * W_A), lambda i: (0, 0)),
                  tab_spec, tab_spec, tab_spec],
        out_specs=[out_spec, out_spec, out_spec],
        out_shape=[out, out, out],
        compiler_params=_cparams(("parallel",)),
        name="attn_proj",
    )(x2d, g, w, *tabs)


def _attn_proj_seq(x2d, g, w, tabs, tm, batch):
    n, d = x2d.shape
    t = n // batch
    period = t // tm
    bpt = tm // MOBA_BLOCK
    tab_spec = pl.BlockSpec((tm, LANES), lambda i: (i % period, 0))
    row_spec = pl.BlockSpec((tm, W_A), lambda i: (i, 0))
    t_spec = pl.BlockSpec((1, H_A, HD_A, tm), lambda i: (i // period, 0, 0, i % period))
    t_shape = jax.ShapeDtypeStruct((batch, H_A, HD_A, t), F32)
    return pl.pallas_call(
        _attn_proj_seq_kernel,
        grid=(n // tm,),
        in_specs=[pl.BlockSpec((tm, d), lambda i: (i, 0)),
                  pl.BlockSpec((1, d), lambda i: (0, 0)),
                  pl.BlockSpec((d, 3 * W_A), lambda i: (0, 0)),
                  tab_spec, tab_spec, tab_spec],
        out_specs=[row_spec, row_spec, t_spec, t_spec, pl.BlockSpec((1, bpt, W_A), lambda i: (i, 0, 0))],
        out_shape=[jax.ShapeDtypeStruct((n, W_A), F32), jax.ShapeDtypeStruct((n, W_A), BF16), t_shape, t_shape,
                   jax.ShapeDtypeStruct((n // tm, bpt, W_A), F32)],
        compiler_params=_cparams(("parallel",)),
        name="attn_proj_seq",
    )(x2d, g, w, *tabs)


def _gdn_proj_kernel(x_ref, g_ref, w_ref, wba_ref, cw_ref, c0_ref, al_ref, dtb_ref,
                     q_ref, k_ref, v_ref, bg_ref, tail_ref, ubuf, *, tm, t_valid, tail_row):
    i = pl.program_id(1)

    @pl.when(i == 0)
    def _():
        ubuf[0:SUBLANES, :] = c0_ref[0]

    @pl.when(i > 0)
    def _():
        ubuf[0:SUBLANES, :] = ubuf[tm:tm + SUBLANES, :]

    h = _rms(x_ref[0], g_ref[...]).astype(BF16)
    if t_valid % tm != 0:
        row = i * tm + lax.broadcasted_iota(jnp.int32, (tm, 1), 0)
        valid = (row < t_valid).astype(F32)
    else:
        valid = None

    def put(ref, s, val):
        ref[0, :, s * LANES:(s + 1) * LANES] = val if valid is None else val * valid

    cw = cw_ref[...]
    n_chunks = C_CONV // PROJ_CHUNK
    dot_chunk = lambda c: jnp.dot(h, w_ref[:, c * PROJ_CHUNK:(c + 1) * PROJ_CHUNK], preferred_element_type=F32)
    nxt = dot_chunk(0)
    for c in range(n_chunks):
        u = nxt
        if c + 1 < n_chunks:
            nxt = dot_chunk(c + 1)
        cols = slice(c * PROJ_CHUNK, (c + 1) * PROJ_CHUNK)
        ubuf[SUBLANES:SUBLANES + tm, cols] = u
        conv = u * cw[CONV_K - 1:CONV_K, cols]
        for j in range(1, CONV_K):
            conv = conv + ubuf[SUBLANES - j:SUBLANES - j + tm, cols] * cw[CONV_K - 1 - j:CONV_K - j, cols]
        act = _silu(conv)
        for s in range(PROJ_CHUNK // LANES):
            slab = c * (PROJ_CHUNK // LANES) + s
            z = act[:, s * LANES:(s + 1) * LANES]
            if slab < 2 * H_B:
                z = z * lax.rsqrt(jnp.sum(z * z, axis=-1, keepdims=True) + EPS)
            put((q_ref, k_ref, v_ref)[slab // H_B], slab % H_B, z)

    ba = jnp.dot(h, wba_ref[...], preferred_element_type=F32)
    lane = lax.broadcasted_iota(jnp.int32, (1, LANES), 1)
    z = ba + dtb_ref[...]
    softplus = jnp.maximum(z, 0.0) + jnp.log1p(jnp.exp(-jnp.abs(z)))
    bg = jnp.where(lane < H_B, jax.nn.sigmoid(ba), -jnp.exp(al_ref[...]) * softplus)
    bg_ref[0] = bg if valid is None else bg * valid
    tail_ref[0] = ubuf[tail_row:tail_row + SUBLANES, :]


def _gdn_proj(x3d, g, w_qkv, w_ba, conv_w, conv0_pad, al_row, dtb_row, tm, t_valid):
    b, t, d = x3d.shape
    n_tiles = -(-t_valid // tm)
    tail_row = t_valid - (n_tiles - 1) * tm
    kern = functools.partial(_gdn_proj_kernel, tm=tm, t_valid=t_valid, tail_row=tail_row)
    seq = lambda w: pl.BlockSpec((1, tm, w), lambda bi, i: (bi, i, 0))
    const = lambda r, c: pl.BlockSpec((r, c), lambda bi, i: (0, 0))
    return pl.pallas_call(
        kern,
        grid=(b, n_tiles),
        in_specs=[seq(d), const(1, d), const(d, C_CONV), const(d, LANES), const(CONV_K, C_CONV),
                  pl.BlockSpec((1, SUBLANES, C_CONV), lambda bi, i: (bi, 0, 0)),
                  const(1, LANES), const(1, LANES)],
        out_specs=[seq(H_B * DK_B), seq(H_B * DK_B), seq(W_B), seq(LANES),
                   pl.BlockSpec((1, SUBLANES, C_CONV), lambda bi, i: (bi, 0, 0))],
        out_shape=[jax.ShapeDtypeStruct((b, n_tiles * tm, H_B * DK_B), F32),
                   jax.ShapeDtypeStruct((b, n_tiles * tm, H_B * DK_B), F32),
                   jax.ShapeDtypeStruct((b, n_tiles * tm, W_B), F32),
                   jax.ShapeDtypeStruct((b, n_tiles * tm, LANES), F32),
                   jax.ShapeDtypeStruct((b, SUBLANES, C_CONV), F32)],
        scratch_shapes=[pltpu.VMEM((tm + 2 * SUBLANES, C_CONV), F32)],
        compiler_params=_cparams(("parallel", "arbitrary")),
        name="gdn_proj",
    )(x3d, g, w_qkv, w_ba, conv_w, conv0_pad, al_row, dtb_row)


def _gdn_tok_kernel(x_ref, g_ref, w_ref, wba_ref, cw_ref, c0_ref, al_ref, dtb_ref,
                    q_ref, k_ref, v_ref, bg_ref, tail_ref):
    h = _rms(x_ref[...], g_ref[...]).astype(BF16)
    cw = cw_ref[...]
    for c in range(C_CONV // PROJ_CHUNK):
        cols = slice(c * PROJ_CHUNK, (c + 1) * PROJ_CHUNK)
        u = jnp.dot(h, w_ref[:, cols], preferred_element_type=F32)
        conv = u * cw[CONV_K - 1:CONV_K, cols]
        for r in range(CONV_K - 1):
            conv = conv + c0_ref[r, :, cols] * cw[r:r + 1, cols]
            if r > 0:
                tail_ref[r - 1, :, cols] = c0_ref[r, :, cols]
        tail_ref[CONV_K - 2, :, cols] = u
        act = _silu(conv)
        for s in range(PROJ_CHUNK // LANES):
            slab = c * (PROJ_CHUNK // LANES) + s
            z = act[:, s * LANES:(s + 1) * LANES]
            if slab < 2 * H_B:
                z = z * lax.rsqrt(jnp.sum(z * z, axis=-1, keepdims=True) + EPS)
            (q_ref, k_ref, v_ref)[slab // H_B][:, (slab % H_B) * LANES:(slab % H_B + 1) * LANES] = z
    ba = jnp.dot(h, wba_ref[...], preferred_element_type=F32)
    lane = lax.broadcasted_iota(jnp.int32, (1, LANES), 1)
    z = ba + dtb_ref[...]
    softplus = jnp.maximum(z, 0.0) + jnp.log1p(jnp.exp(-jnp.abs(z)))
    bg_ref[...] = jnp.where(lane < H_B, jax.nn.sigmoid(ba), -jnp.exp(al_ref[...]) * softplus)


def _gdn_tok(x2d, g, w_qkv, w_ba, conv_w, c0, al_row, dtb_row):
    b, d = x2d.shape
    full = lambda *shape: pl.BlockSpec(shape, lambda i: (0,) * len(shape))
    return pl.pallas_call(
        _gdn_tok_kernel,
        grid=(1,),
        in_specs=[full(b, d), full(1, d), full(d, C_CONV), full(d, LANES), full(CONV_K, C_CONV),
                  full(CONV_K - 1, b, C_CONV), full(1, LANES), full(1, LANES)],
        out_specs=[full(b, H_B * DK_B), full(b, H_B * DK_B), full(b, W_B), full(b, LANES),
                   full(CONV_K - 1, b, C_CONV)],
        out_shape=[jax.ShapeDtypeStruct((b, H_B * DK_B), F32), jax.ShapeDtypeStruct((b, H_B * DK_B), F32),
                   jax.ShapeDtypeStruct((b, W_B), F32), jax.ShapeDtypeStruct((b, LANES), F32),
                   jax.ShapeDtypeStruct((CONV_K - 1, b, C_CONV), F32)],
        compiler_params=_cparams(("arbitrary",)),
        name="gdn_tok",
    )(x2d, g, w_qkv, w_ba, conv_w, c0, al_row, dtb_row)


def _top3(gate, idx, axis, big):
    sels = []
    g = gate
    for _ in range(MOBA_TOPK):
        m = jnp.max(g, axis=axis, keepdims=True)
        ii = jnp.min(jnp.where(g == m, idx, big), axis=axis, keepdims=True)
        sels.append(ii)
        g = jnp.where(idx == ii, -jnp.inf, g)
    return sels


def _moba_prompt_kernel(q_ref, k_ref, vt_ref, kbar_ref, o_ref, vt_scr, *, nb):
    i = pl.program_id(2)
    tq = MOBA_BLOCK
    n_heads = LANES // HD_A

    @pl.when(i == 0)
    def _():
        ones = jnp.ones((V_ROWS - HD_A, MOBA_BLOCK), BF16)
        for j in range(vt_scr.shape[0]):
            for hh in range(n_heads):
                if j < nb:
                    vt_scr[j, hh, 0:HD_A, :] = vt_ref[0, hh, :, j * MOBA_BLOCK:(j + 1) * MOBA_BLOCK].astype(BF16)
                    vt_scr[j, hh, HD_A:V_ROWS, :] = ones
                else:
                    vt_scr[j, hh] = jnp.zeros((V_ROWS, MOBA_BLOCK), BF16)

    def k_block(j):
        start = pl.multiple_of(jnp.minimum(j, nb - 1) * MOBA_BLOCK, MOBA_BLOCK)
        return k_ref[0, pl.ds(start, MOBA_BLOCK), :]

    q_t = q_ref[0].T
    kbar = kbar_ref[0]
    dim_head = lax.broadcasted_iota(jnp.int32, (LANES, 1), 0) // HD_A
    blk = lax.broadcasted_iota(jnp.int32, (nb, 1), 0)
    blk_f = blk.astype(F32)
    key_i = lax.broadcasted_iota(jnp.int32, (tq, tq), 0)
    qry_i = lax.broadcasted_iota(jnp.int32, (tq, tq), 1)
    heads = range(n_heads)

    def attend(j0, n_blk, keep, prev):
        raws = [[jnp.dot(kg, qs[hh], preferred_element_type=F32) for hh in heads]
                for kg in [k_block(j0 + g) for g in range(n_blk)]]
        ms = [None if prev is None else prev[hh][0] for hh in heads]
        m_at = [[None] * n_blk for _ in heads]
        pvs = [[None] * n_blk for _ in heads]
        for g in range(n_blk):
            for hh in heads:
                s = jnp.where(keep[hh][g], raws[g][hh].astype(BF16), NEG_BIG)
                m = jnp.max(s, axis=0, keepdims=True)
                if ms[hh] is not None:
                    m = jnp.maximum(ms[hh], m)
                p = jnp.exp2(s - m)
                pvs[hh][g] = jnp.dot(vt_scr[j0 + g, hh], p, preferred_element_type=F32)
                m_at[hh][g] = ms[hh] = m
        out = []
        for hh in heads:
            m_fin = ms[hh]
            rescale = lambda m_old: jnp.exp2(m_old.astype(F32) - m_fin.astype(F32))
            accl = None if prev is None else rescale(prev[hh][0]) * prev[hh][1]
            for g in range(n_blk):
                term = pvs[hh][g] if g == n_blk - 1 else rescale(m_at[hh][g]) * pvs[hh][g]
                accl = term if accl is None else accl + term
            out.append((m_fin, accl))
        return out

    qts = [jnp.where(dim_head == hh, q_t, 0.0) for hh in heads]
    qs = [(qts[hh] * (HD_A ** -0.5 * math.log2(math.e))).astype(BF16) for hh in heads]
    causal = key_i <= qry_i
    state0 = attend(i, 1, [[causal]] * n_heads, None)
    sels = []
    for hh in heads:
        gate = jnp.where(blk < i, _dot_hi(kbar, qts[hh]), -jnp.inf)
        sels.append(_top3(gate, blk_f, 0, float(nb)))

    def body(t, carry):
        j0 = t * KV_GROUP
        keep = []
        for hh in heads:
            i1, i2, i3 = sels[hh]
            jfs = [(j0 + g).astype(F32) for g in range(KV_GROUP)]
            keep.append([(i1 == jf) | (i2 == jf) | (i3 == jf) for jf in jfs])
        state = attend(j0, KV_GROUP, keep, [carry[2 * hh:2 * hh + 2] for hh in heads])
        return tuple(x for st in state for x in st)

    carry = lax.fori_loop(0, (i + KV_GROUP - 1) // KV_GROUP, body, tuple(x for st in state0 for x in st))
    o_t = jnp.concatenate([carry[2 * hh + 1][:HD_A] / carry[2 * hh + 1][HD_A:HD_A + 1] for hh in heads],
                          axis=0)
    o_ref[0] = o_t.T


def _moba_prompt(q3d, kb3d, vt4d, kbar):
    b, t, w = q3d.shape
    nb = t // MOBA_BLOCK
    nb_pad = -(-nb // KV_GROUP) * KV_GROUP
    n_pairs = w // LANES
    hpp = LANES // HD_A
    qspec = pl.BlockSpec((1, MOBA_BLOCK, LANES), lambda bi, hp, i: (bi, i, hp))
    return pl.pallas_call(
        functools.partial(_moba_prompt_kernel, nb=nb),
        grid=(b, n_pairs, nb),
        in_specs=[qspec,
                  pl.BlockSpec((1, t, LANES), lambda bi, hp, i: (bi, 0, hp)),
                  pl.BlockSpec((1, hpp, HD_A, t), lambda bi, hp, i: (bi, hp, 0, 0)),
                  pl.BlockSpec((1, nb, LANES), lambda bi, hp, i: (bi, 0, hp))],
        out_specs=qspec,
        out_shape=jax.ShapeDtypeStruct((b, t, w), F32),
        scratch_shapes=[pltpu.VMEM((nb_pad, hpp, V_ROWS, MOBA_BLOCK), BF16)],
        compiler_params=_cparams(("parallel", "parallel", "arbitrary")),
        name="moba_prompt",
    )(q3d, kb3d, vt4d, kbar)


def _gdn_seq_kernel(q_ref, k_ref, v_ref, bg_ref, s0_ref, gn_ref, o_ref, s_out_ref, s_scr, *, rows):
    step = pl.program_id(1)
    n_steps = pl.num_programs(1)
    cc = GDN_CHUNK
    ncs = rows // cc

    @pl.when(step == 0)
    def _():
        s_scr[...] = s0_ref[0]

    bg = bg_ref[0]
    gn = gn_ref[...]
    ri = lax.broadcasted_iota(jnp.int32, (rows, rows), 0)
    ci = lax.broadcasted_iota(jnp.int32, (rows, rows), 1)
    same = (ri // cc) == (ci // cc)
    sum_mats = jnp.concatenate([(same & (ci <= ri)).astype(BF16), same.astype(BF16)], axis=0)
    b_hi = bg.astype(BF16)
    r_1 = bg - b_hi.astype(F32)
    b_mid = r_1.astype(BF16)
    b_lo = (r_1 - b_mid.astype(F32)).astype(BF16)
    sums = sum(jnp.dot(sum_mats, piece, preferred_element_type=F32) for piece in (b_hi, b_mid, b_lo))
    gc = sums[:rows]
    gl = sums[rows:]
    gc_t = gc.T
    e_gc = jnp.exp(gc)
    e_tail = jnp.exp(gl - gc)
    e_tot = jnp.exp(gl)
    pr = lax.broadcasted_iota(jnp.int32, (cc, rows), 0)
    pl_i = lax.broadcasted_iota(jnp.int32, (cc, rows), 1)
    pc = pl_i % cc
    lane_blk = pl_i // cc
    causal_p = pr >= pc
    strict_p = pr > pc

    def pack(full):
        out = full[(ncs - 1) * cc:ncs * cc]
        for c in range(ncs - 2, -1, -1):
            out = jnp.where(lane_blk == c, full[c * cc:(c + 1) * cc], out)
        return out

    def pack_col(col):
        out = col[(ncs - 1) * cc:ncs * cc]
        for c in range(ncs - 2, -1, -1):
            out = jnp.where(lane_blk == c, col[c * cc:(c + 1) * cc], out)
        return out

    def bdiag(p):
        return jnp.where(same, jnp.concatenate([p] * ncs, axis=0), 0.0)

    heads = range(H_B)
    col = lambda a, h: a[:, H_B + h:H_B + h + 1]
    qs = [q_ref[0, :, h * DK_B:(h + 1) * DK_B] * (DK_B ** -0.5) for h in heads]
    ks = [k_ref[0, :, h * DK_B:(h + 1) * DK_B] for h in heads]
    kbetas = [ks[h] * bg[:, h:h + 1] for h in heads]
    fulls = [_dot_nt(jnp.concatenate([kbetas[h], qs[h]], axis=0), ks[h]) for h in heads]
    decays = [jnp.exp(jnp.where(causal_p, pack_col(col(gc, h)) - gc_t[H_B + h:H_B + h + 1, :], -jnp.inf))
              for h in heads]
    attns = [jnp.where(causal_p, pack(fulls[h][rows:]) * decays[h], 0.0) for h in heads]
    es = [-jnp.where(strict_p, pack(fulls[h][:rows]) * decays[h], 0.0) for h in heads]
    pws = [_dot(es[h], bdiag(es[h])) for h in heads]
    for _ in range(int(math.log2(cc)) - 2):
        rs_ = [_dot(jnp.concatenate([es[h], pws[h]], axis=0), bdiag(pws[h])) for h in heads]
        es = [es[h] + pws[h] + rs_[h][:cc] for h in heads]
        pws = [rs_[h][cc:] for h in heads]
    es = [es[h] + pws[h] + _dot(es[h], bdiag(pws[h])) for h in heads]
    rhss = [jnp.concatenate([v_ref[0, :, h * DV_B:(h + 1) * DV_B] * bg[:, h:h + 1],
                             kbetas[h] * col(e_gc, h)], axis=1) for h in heads]
    sols = [rhss[h] + _dot(bdiag(es[h]), rhss[h]) for h in heads]
    q_decs = [qs[h] * col(e_gc, h) for h in heads]
    k_tails = [ks[h] * col(e_tail, h) for h in heads]
    ss = [s_scr[h] for h in heads]
    for c in range(ncs):
        rs = slice(c * cc, (c + 1) * cc)
        wqs = [_dot(jnp.concatenate([sols[h][rs, DV_B:], q_decs[h][rs]], axis=0), ss[h]) for h in heads]
        v_news = [sols[h][rs, :DV_B] - wqs[h][:cc] for h in heads]
        os_ = [wqs[h][cc:] + _dot(attns[h][:, rs], v_news[h]) for h in heads]
        kvs = [_dot_tn(k_tails[h][rs], v_news[h]) for h in heads]
        ss = [ss[h] * e_tot[c * cc:c * cc + 1, H_B + h:H_B + h + 1] + kvs[h] for h in heads]
        for h in heads:
            o_ref[0, rs, h * DV_B:(h + 1) * DV_B] = _rms(os_[h], gn)
    for h in heads:
        s_scr[h] = ss[h]


    @pl.when(step == n_steps - 1)
    def _():
        s_out_ref[0] = s_scr[...]


def _gdn_seq(q, k, v, bg, s0, g_onorm_row):
    b, t, _ = q.shape
    rows = MXU_DIM
    while t % rows:
        rows //= 2
    seq = lambda w: pl.BlockSpec((1, rows, w), lambda bi, c: (bi, c, 0))
    sspec = pl.BlockSpec((1, H_B, DK_B, DV_B), lambda bi, c: (bi, 0, 0, 0))
    return pl.pallas_call(
        functools.partial(_gdn_seq_kernel, rows=rows),
        grid=(b, t // rows),
        in_specs=[seq(H_B * DK_B), seq(H_B * DK_B), seq(W_B), seq(LANES), sspec,
                  pl.BlockSpec((1, DV_B), lambda bi, c: (0, 0))],
        out_specs=[seq(W_B), sspec],
        out_shape=[jax.ShapeDtypeStruct((b, t, W_B), F32),
                   jax.ShapeDtypeStruct((b, H_B, DK_B, DV_B), F32)],
        scratch_shapes=[pltpu.VMEM((H_B, DK_B, DV_B), F32)],
        compiler_params=_cparams(("parallel", "arbitrary")),
        name="gdn_seq",
    )(q, k, v, bg, s0, g_onorm_row)


def _gdn_step_kernel(kq_ref, v_ref, bg_ref, s0_ref, gn_ref, o_ref, s_out_ref):
    bg = bg_ref[0]
    gn = gn_ref[...]
    kq = kq_ref[0]
    for h in range(H_B):
        beta = bg[:, h:h + 1]
        eg = jnp.exp(bg[:, H_B + h:H_B + h + 1])
        k_col = kq[:, h:h + 1]
        q_col = kq[:, H_B + h:H_B + h + 1] * (DK_B ** -0.5)
        v_row = v_ref[0, :, h * DV_B:(h + 1) * DV_B]
        s0 = s0_ref[0, h]
        k_s = jnp.sum(k_col * s0, axis=0, keepdims=True)
        q_s = jnp.sum(q_col * s0, axis=0, keepdims=True)
        v_new = beta * (v_row - eg * k_s)
        qk = jnp.sum(q_col * k_col, axis=0, keepdims=True)
        o = eg * q_s + qk * v_new
        s_out_ref[0, h] = s0 * eg + k_col * v_new
        o_ref[0, :, h * DV_B:(h + 1) * DV_B] = _rms(o, gn)


def _gdn_step(kq_cols, v, bg, s0, g_onorm_row):
    b = v.shape[0]
    tok = lambda w: pl.BlockSpec((1, 1, w), lambda bi: (bi, 0, 0))
    sspec = pl.BlockSpec((1, H_B, DK_B, DV_B), lambda bi: (bi, 0, 0, 0))
    return pl.pallas_call(
        _gdn_step_kernel,
        grid=(b,),
        in_specs=[pl.BlockSpec((1, DK_B, 2 * H_B), lambda bi: (bi, 0, 0)), tok(W_B), tok(LANES), sspec,
                  pl.BlockSpec((1, DV_B), lambda bi: (0, 0))],
        out_specs=[tok(W_B), sspec],
        out_shape=[jax.ShapeDtypeStruct((b, 1, W_B), F32),
                   jax.ShapeDtypeStruct((b, H_B, DK_B, DV_B), F32)],
        compiler_params=_cparams(("parallel",)),
        name="gdn_step",
    )(kq_cols, v, bg, s0, g_onorm_row)


def _out_stage_kernel(x_ref, oa_ref, ob_ref, p_ref, gmix_ref, wzg_ref, wpa_ref, wpb_ref, wo_ref,
                      gple_ref, wpg_ref, wple_ref, gfin_ref, y_ref, *, final):
    x = x_ref[...]
    h = _rms(x, gmix_ref[...]).astype(BF16)
    zg = jnp.dot(h, wzg_ref[...], preferred_element_type=F32)
    za = zg[:, :W_A]
    zb = zg[:, W_A:W_A + W_B]
    d = x.shape[1]
    ga = zg[:, W_A + W_B:W_A + W_B + d]
    gb = zg[:, W_A + W_B + d:]
    ya = _dot(oa_ref[...] * _silu(za), wpa_ref[...])
    yb = _dot(ob_ref[...] * _silu(zb), wpb_ref[...])
    mixed = jax.nn.sigmoid(ga) * ya + jax.nn.sigmoid(gb) * yb
    x = x + _dot(mixed, wo_ref[...])
    gate = jax.nn.sigmoid(_dot(_rms(x, gple_ref[...]), wpg_ref[...]))
    x = x + gate * _dot(p_ref[...], wple_ref[...])
    y_ref[...] = _rms(x, gfin_ref[...]) if final else x


def _out_stage(x2d, oa, ob, p2d, gmix, wzg, wpa, wpb, wo, gple, wpg, wple, gfin, tm, final):
    n, d = x2d.shape
    row = lambda w: pl.BlockSpec((tm, w), lambda i: (i, 0))
    const = lambda a: pl.BlockSpec(a.shape, lambda i: (0, 0))
    return pl.pallas_call(
        functools.partial(_out_stage_kernel, final=final),
        grid=(n // tm,),
        in_specs=[row(d), row(W_A), row(W_B), row(p2d.shape[1]), const(gmix), const(wzg), const(wpa),
                  const(wpb), const(wo), const(gple), const(wpg), const(wple), const(gfin)],
        out_specs=row(d),
        out_shape=jax.ShapeDtypeStruct((n, d), F32),
        compiler_params=_cparams(("parallel",)),
        name="out_stage",
    )(x2d, oa, ob, p2d, gmix, wzg, wpa, wpb, wo, gple, wpg, wple, gfin)


PAGES_PER_STEP = 32


def _page_scores_kernel(pt_ref, q_ref, *refs):
    k_refs, o_ref = refs[:-1], refs[-1]
    qb = q_ref[0]
    for r, k_ref in enumerate(k_refs):
        o_ref[0, r] = jnp.sum(k_ref[0, 0] * qb, axis=1)


def _page_scores(q_b, cache_kt, layer, pt_flat, n_pages):
    b = q_b.shape[0]
    _, _, h, hd, page = cache_kt.shape
    nps = PAGES_PER_STEP if n_pages % PAGES_PER_STEP == 0 else 1
    in_specs = [pl.BlockSpec((1, h, hd, page), lambda bi, g, pt: (bi, 0, 0, 0))]
    in_specs += [pl.BlockSpec((1, 1, h, hd, page),
                              lambda bi, g, pt, r=r: (layer, pt[bi * n_pages + g * nps + r], 0, 0, 0))
                 for r in range(nps)]
    return pl.pallas_call(
        _page_scores_kernel,
        grid_spec=pltpu.PrefetchScalarGridSpec(
            num_scalar_prefetch=1,
            grid=(b, n_pages // nps),
            in_specs=in_specs,
            out_specs=pl.BlockSpec((1, nps, h, page), lambda bi, g, pt: (bi, g, 0, 0)),
        ),
        out_shape=jax.ShapeDtypeStruct((b, n_pages, h, page), F32),
        compiler_params=_cparams(("parallel", "parallel")),
        name="page_scores",
    )(pt_flat, q_b, *([cache_kt] * nps))


def _sample_select_kernel(p_ref, o_ref, *, nblk, ppb):
    tot = jnp.sum(p_ref[0], axis=2, keepdims=True)
    gate = jnp.sum(tot.reshape(nblk, ppb, H_A, 1), axis=1) * (1.0 / MOBA_BLOCK)
    blk_f = lax.broadcasted_iota(jnp.int32, (nblk, 1, 1), 0).astype(F32)
    for r, sel in enumerate(_top3(gate, blk_f, 0, float(nblk))):
        o_ref[0, r] = jnp.broadcast_to(jnp.minimum(sel[0], nblk - 1.0), (H_A, LANES)).astype(jnp.int32)


def _sample_select(scores, ppb):
    b, n_pages, h, page = scores.shape
    return pl.pallas_call(
        functools.partial(_sample_select_kernel, nblk=n_pages // ppb, ppb=ppb),
        grid=(b,),
        in_specs=[pl.BlockSpec((1, n_pages, h, page), lambda bi: (bi, 0, 0, 0))],
        out_specs=pl.BlockSpec((1, MOBA_TOPK, h, LANES), lambda bi: (bi, 0, 0, 0)),
        out_shape=jax.ShapeDtypeStruct((b, MOBA_TOPK, h, LANES), jnp.int32),
        compiler_params=_cparams(("parallel",)),
        name="sample_select",
    )(scores)


PAGED_HEADS_PER_STEP = 4


def _paged_attn_kernel(pt_ref, top_ref, q_ref, kn_ref, vn_ref, s_ref, *refs, n_s, ppb):
    v_refs, o_ref = refs[:-1], refs[-1]
    bi = pl.program_id(0)
    hg = pl.program_id(1)
    scale = HD_A ** -0.5
    lane = lax.broadcasted_iota(jnp.int32, (1, H_A), 1)
    row = lax.broadcasted_iota(jnp.int32, (H_A, 1), 0)
    out = jnp.zeros((HD_A, H_A), F32)
    for j in range(PAGED_HEADS_PER_STEP):
        h = hg * PAGED_HEADS_PER_STEP + j
        pick = lambda a, h=h: jnp.sum(jnp.where(row == h, a, 0.0), axis=0, keepdims=True)
        own = jnp.sum(pick(q_ref[0] * kn_ref[0]), axis=1, keepdims=True) * scale
        rows = []
        for s in range(n_s):
            page = top_ref[(bi * MOBA_TOPK + s // ppb) * H_A + h] * ppb + s % ppb
            rows.append(pick(s_ref[0, page]) * scale)
        m = own
        for r in rows:
            m = jnp.maximum(m, jnp.max(r, axis=1, keepdims=True))
        p_own = jnp.exp(own - m)
        ps = [jnp.exp(r - m) for r in rows]
        l = p_own + sum(jnp.sum(p, axis=1, keepdims=True) for p in ps)
        acc = sum(p * v_refs[j * n_s + s][0, 0, 0] for s, p in enumerate(ps))
        vn_col = jnp.sum(jnp.where(lane == h, vn_ref[0], 0.0), axis=1, keepdims=True)
        res = (jnp.sum(acc, axis=1, keepdims=True) + p_own * vn_col) / l
        out = jnp.where(lane == h, res, out)

    @pl.when(hg == 0)
    def _():
        o_ref[0] = out

    @pl.when(hg > 0)
    def _():
        o_ref[0] = o_ref[0] + out


def _paged_attn(q3, kn3, vn_t, scores, cache_vt, layer, pt_flat, top_flat, n_pages):
    b = q3.shape[0]
    _, _, h, hd, page = cache_vt.shape
    ppb = MOBA_BLOCK // page
    n_s = MOBA_TOPK * ppb

    hps = PAGED_HEADS_PER_STEP

    def v_spec(j, s):
        def index(bi, hg, pt, top):
            hi = hg * hps + j
            logical = top[(bi * MOBA_TOPK + s // ppb) * H_A + hi] * ppb + s % ppb
            return (layer, pt[bi * n_pages + logical], hi, 0, 0)
        return pl.BlockSpec((1, 1, 1, hd, page), index)

    tok = pl.BlockSpec((1, h, hd), lambda bi, hg, pt, top: (bi, 0, 0))
    tok_t = pl.BlockSpec((1, hd, h), lambda bi, hg, pt, top: (bi, 0, 0))
    return pl.pallas_call(
        functools.partial(_paged_attn_kernel, n_s=n_s, ppb=ppb),
        grid_spec=pltpu.PrefetchScalarGridSpec(
            num_scalar_prefetch=2,
            grid=(b, h // hps),
            in_specs=[tok, tok, tok_t,
                      pl.BlockSpec((1, n_pages, h, page), lambda bi, hg, pt, top: (bi, 0, 0, 0))]
            + [v_spec(j, s) for j in range(hps) for s in range(n_s)],
            out_specs=tok_t,
        ),
        out_shape=jax.ShapeDtypeStruct((b, hd, h), F32),
        compiler_params=_cparams(("parallel", "arbitrary")),
        name="paged_attn",
    )(pt_flat, top_flat, q3, kn3, vn_t, scores, *([cache_vt] * (hps * n_s)))


def _split_weights(w_in_l, d):
    o = 0
    parts = {}
    for name, n in (("qa", W_A), ("ka", W_A), ("va", W_A), ("za", W_A), ("qb", H_B * DK_B),
                    ("kb", H_B * DK_B), ("vb", W_B), ("zb", W_B), ("beta", H_B), ("alpha", H_B),
                    ("ga", d), ("gb", d)):
        parts[name] = w_in_l[:, o:o + n]
        o += n
    w_attn = jnp.concatenate([parts["qa"], parts["ka"], parts["va"]], axis=1).astype(BF16)
    w_qkv = jnp.concatenate([parts["qb"], parts["kb"], parts["vb"]], axis=1).astype(BF16)
    w_ba = jnp.concatenate([parts["beta"], parts["alpha"],
                            jnp.zeros((d, LANES - 2 * H_B), F32)], axis=1).astype(BF16)
    w_zg = jnp.concatenate([parts["za"], parts["zb"], parts["ga"], parts["gb"]], axis=1).astype(BF16)
    return w_attn, w_qkv, w_ba, w_zg


def _lane_row(vec, offset):
    out = jnp.zeros((1, LANES), F32)
    return out.at[0, offset:offset + vec.shape[0]].set(vec.astype(F32))


def _pick_tile(n, pref):
    t = pref
    while n % t:
        t //= 2
    return t


def kernel(x_prompt, x_sample, p_prompt, p_sample, cache_k, cache_v, page_table, state_gdn_s, state_gdn_conv, g_mix, w_in, conv_w, a_log, dt_bias, g_onorm, w_pa, w_pb, w_o, g_ple, w_ple_gate, w_ple, g_final):
    bp, tp, d = x_prompt.shape
    bs, ts, _ = x_sample.shape
    depth = w_in.shape[0]
    n_pages = page_table.shape[1]
    page = cache_k.shape[2]
    assert ts == 1 and tp % MOBA_BLOCK == 0 and (n_pages * page) % MOBA_BLOCK == 0
    assert tp % GDN_CHUNK == 0

    pos_p = jnp.arange(tp, dtype=jnp.int32)
    pos_s = jnp.full((bs,), n_pages * page, dtype=jnp.int32)
    tabs_p = _rope_tables(pos_p)
    tabs_s = _rope_tables(pos_s)
    pt_flat = page_table.reshape(-1).astype(jnp.int32)
    cache_kt = jnp.transpose(cache_k, (0, 1, 3, 4, 2))
    cache_vt = jnp.transpose(cache_v, (0, 1, 3, 4, 2))

    xp = x_prompt.reshape(bp * tp, d)
    xs = x_sample.reshape(bs, d)
    outs = {k: [] for k in ("kp", "vp", "sp", "cp", "ks", "vs", "ss", "cs")}
    for l in range(depth):
        final = l == depth - 1
        w_attn, w_qkv, w_ba, w_zg = _split_weights(w_in[l], d)
        gmix = g_mix[l].reshape(1, d)
        al_row = _lane_row(a_log[l], H_B)
        dtb_row = _lane_row(dt_bias[l], H_B)
        gon = g_onorm[l].reshape(1, DV_B)
        wpa, wpb, wo = w_pa[l].astype(BF16), w_pb[l].astype(BF16), w_o[l].astype(BF16)
        wpg, wple = w_ple_gate[l].astype(BF16), w_ple[l].astype(BF16)
        gple = g_ple[l].reshape(1, d)
        gfin = g_final.reshape(1, d)

        qa_s, ka_s, va_s = _attn_proj(xs, gmix, w_attn, tabs_s, bs)
        q3 = qa_s.reshape(bs, H_A, HD_A)
        q_b = jnp.broadcast_to(q3[..., None], (bs, H_A, HD_A, page))

        qa, kb, kt, vt, kbar = _attn_proj_seq(xp, gmix, w_attn, tabs_p, _pick_tile(tp, 512), bp)
        conv0 = jnp.zeros((bp, SUBLANES, C_CONV), F32)
        qn, kn, vn, bg, tail = _gdn_proj(xp.reshape(bp, tp, d), gmix, w_qkv, w_ba, conv_w[l], conv0,
                                         al_row, dtb_row, _pick_tile(tp, 256), tp)
        oa = _moba_prompt(qa.reshape(bp, tp, W_A), kb.reshape(bp, tp, W_A), vt,
                          kbar.reshape(bp, tp // MOBA_BLOCK, W_A))
        s0 = jnp.zeros((bp, H_B, DK_B, DV_B), F32)
        ob, s_fin = _gdn_seq(qn, kn, vn, bg, s0, gon)
        xp = _out_stage(xp, oa.reshape(bp * tp, W_A), ob.reshape(bp * tp, W_B),
                        p_prompt[l].reshape(bp * tp, -1), gmix, w_zg, wpa, wpb, wo, gple, wpg, wple,
                        gfin, _pick_tile(bp * tp, 256), final)
        outs["kp"].append(jnp.transpose(kt, (0, 3, 1, 2)))
        outs["vp"].append(jnp.transpose(vt, (0, 3, 1, 2)))
        outs["sp"].append(s_fin.astype(state_gdn_s.dtype))
        outs["cp"].append(tail[:, SUBLANES - (CONV_K - 1):, :])

        conv0_s = jnp.transpose(state_gdn_conv[l].astype(F32), (1, 0, 2))
        qn_s, kn_s, vn_s, bg_s, tail_s = _gdn_tok(xs, gmix, w_qkv, w_ba, conv_w[l], conv0_s, al_row, dtb_row)
        scores = _page_scores(q_b, cache_kt, l, pt_flat, n_pages)
        top =_sample_select(scores, MOBA_BLOCK // page)
        top_flat = top[:, :, :, 0].reshape(-1)
        vn_t = jnp.transpose(va_s.reshape(bs, H_A, HD_A), (0, 2, 1))
        oa_t = _paged_attn(q3, ka_s.reshape(bs, H_A, HD_A), vn_t, scores, cache_vt, l, pt_flat, top_flat,
                           n_pages)
        oa_s = jnp.transpose(oa_t, (0, 2, 1))
        cols = lambda a: jnp.transpose(a.reshape(bs, H_B, DK_B), (0, 2, 1))
        kq_cols = jnp.concatenate([cols(kn_s), cols(qn_s)], axis=2)
        ob_s, s_fin_s = _gdn_step(kq_cols, vn_s.reshape(bs, 1, W_B), bg_s.reshape(bs, 1, LANES),
                                  state_gdn_s[l].astype(F32), gon)
        xs = _out_stage(xs, oa_s.reshape(bs, W_A), ob_s[:, 0, :], p_sample[l].reshape(bs, -1), gmix, w_zg,
                        wpa, wpb, wo, gple, wpg, wple, gfin, bs, final)
        outs["ks"].append(ka_s.reshape(bs, 1, H_A, HD_A))
        outs["vs"].append(va_s.reshape(bs, 1, H_A, HD_A))
        outs["ss"].append(s_fin_s.astype(state_gdn_s.dtype))
        outs["cs"].append(jnp.transpose(tail_s, (1, 0, 2)))

    y_prompt = xp.reshape(bp, tp, d)
    y_sample = xs.reshape(bs, ts, d)
    st = lambda k: jnp.stack(outs[k])
    return (y_prompt, y_sample, st("kp"), st("vp"), st("sp"), st("cp"),
            st("ks"), st("vs"), st("ss"), st("cs"))
```

```python
import functools
import math

import jax
import jax.numpy as jnp
from jax import lax
from jax.experimental import pallas as pl
from jax.experimental.pallas import tpu as pltpu

F32 = jnp.float32
BF16 = jnp.bfloat16
HIGHEST = lax.Precision.HIGHEST

H_A = 8
HD_A = 64
W_A = H_A * HD_A
ROT_DIM = HD_A // 4
ROPE_THETA = 500000.0
MOBA_BLOCK = 256
MOBA_TOPK = 3
H_B = 8
DK_B = 128
DV_B = 128
W_B = H_B * DV_B
CONV_K = 4
C_CONV = H_B * (2 * DK_B + DV_B)
GDN_CHUNK = 64
EPS = 1e-6

LANES = 128
SUBLANES = 8
MXU_DIM = 256
VMEM_LIMIT_BYTES = 56 * 1024 * 1024

NEG_BIG = -1e30
KV_GROUP = 4
PROJ_CHUNK = 512
V_ROWS = HD_A + 16


def _cparams(sem):
    return pltpu.CompilerParams(dimension_semantics=sem, vmem_limit_bytes=VMEM_LIMIT_BYTES)


def _rms(x, g):
    return x * lax.rsqrt(jnp.mean(x * x, axis=-1, keepdims=True) + EPS) * g


def _silu(x):
    return x * jax.nn.sigmoid(x)


def _dot(a, b):
    return jnp.dot(a.astype(BF16), b.astype(BF16), preferred_element_type=F32)


def _dot_nt(a, b):
    return lax.dot_general(a.astype(BF16), b.astype(BF16), (((1,), (1,)), ((), ())),
                           preferred_element_type=F32)


def _dot_tn(a, b):
    return lax.dot_general(a.astype(BF16), b.astype(BF16), (((0,), (0,)), ((), ())),
                           preferred_element_type=F32)


def _dot_hi(a, b):
    return jnp.dot(a, b, precision=HIGHEST, preferred_element_type=F32)


def _dot_nt_hi(a, b):
    return lax.dot_general(a, b, (((1,), (1,)), ((), ())), precision=HIGHEST,
                           preferred_element_type=F32)


def _attn_qkv(x_ref, g_ref, w_ref, c_ref, sa_ref, sb_ref):
    h = _rms(x_ref[...], g_ref[...]).astype(BF16)
    y = jnp.dot(h, w_ref[...], preferred_element_type=F32)
    c, sa, sb = c_ref[...], sa_ref[...], sb_ref[...]
    half = ROT_DIM // 2
    slabs = lambda base: [y[:, base + s * LANES: base + (s + 1) * LANES] for s in range(W_A // LANES)]
    rope = lambda z: z * c + pltpu.roll(z, LANES - half, 1) * sa + pltpu.roll(z, half, 1) * sb
    return [rope(z) for z in slabs(0)], [rope(z) for z in slabs(W_A)], slabs(2 * W_A)


def _attn_proj_kernel(x_ref, g_ref, w_ref, c_ref, sa_ref, sb_ref, q_ref, k_ref, v_ref):
    qs, ks, vs = _attn_qkv(x_ref, g_ref, w_ref, c_ref, sa_ref, sb_ref)
    for s in range(W_A // LANES):
        q_ref[:, s * LANES:(s + 1) * LANES] = qs[s]
        k_ref[:, s * LANES:(s + 1) * LANES] = ks[s]
        v_ref[:, s * LANES:(s + 1) * LANES] = vs[s]


def _attn_proj_seq_kernel(x_ref, g_ref, w_ref, c_ref, sa_ref, sb_ref, q_ref, kb_ref, kt_ref, vt_ref, kbar_ref):
    qs, ks, vs = _attn_qkv(x_ref, g_ref, w_ref, c_ref, sa_ref, sb_ref)
    tm = x_ref.shape[0]
    hps = LANES // HD_A
    for s in range(W_A // LANES):
        q_ref[:, s * LANES:(s + 1) * LANES] = qs[s]
        kb_ref[:, s * LANES:(s + 1) * LANES] = ks[s].astype(BF16)
        kt_ref[0, s * hps:(s + 1) * hps] = ks[s].T.reshape(hps, HD_A, tm)
        vt_ref[0, s * hps:(s + 1) * hps] = vs[s].T.reshape(hps, HD_A, tm)
        kbar_ref[0, :, s * LANES:(s + 1) * LANES] = jnp.sum(
            ks[s].reshape(tm // MOBA_BLOCK, MOBA_BLOCK, LANES), axis=1) * (1.0 / MOBA_BLOCK)


def _rope_tables(pos):
    half = ROT_DIM // 2
    inv = jnp.power(ROPE_THETA, -jnp.arange(half, dtype=F32) / half)
    ang = pos.astype(F32)[:, None] * inv[None, :]
    cos, sin = jnp.cos(ang), jnp.sin(ang)
    t = pos.shape[0]
    ones = jnp.ones((t, HD_A - ROT_DIM), F32)
    zeros = jnp.zeros((t, HD_A - ROT_DIM), F32)
    zh = jnp.zeros((t, half), F32)
    c = jnp.concatenate([cos, cos, ones], axis=1)
    sa = jnp.concatenate([-sin, zh, zeros], axis=1)
    sb = jnp.concatenate([zh, sin, zeros], axis=1)
    rep = LANES // HD_A
    return tuple(jnp.tile(a, (1, rep)) for a in (c, sa, sb))


def _attn_proj(x2d, g, w, tabs, tm):
    n, d = x2d.shape
    t_tab = tabs[0].shape[0]
    period = t_tab // tm
    tab_spec = pl.BlockSpec((tm, LANES), lambda i: (i % period, 0))
    out_spec = pl.BlockSpec((tm, W_A), lambda i: (i, 0))
    out = jax.ShapeDtypeStruct((n, W_A), F32)
    return pl.pallas_call(
        _attn_proj_kernel,
        grid=(n // tm,),
        in_specs=[pl.BlockSpec((tm, d), lambda i: (i, 0)),
                  pl.BlockSpec((1, d), lambda i: (0, 0)),
                  pl.BlockSpec((d, 3 * W_A), lambda i: (0, 0)),
                  tab_spec, tab_spec, tab_spec],
        out_specs=[out_spec, out_spec, out_spec],
        out_shape=[out, out, out],
        compiler_params=_cparams(("parallel",)),
        name="attn_proj",
    )(x2d, g, w, *tabs)


def _attn_proj_seq(x2d, g, w, tabs, tm, batch):
    n, d = x2d.shape
    t = n // batch
    period = t // tm
    bpt = tm // MOBA_BLOCK
    tab_spec = pl.BlockSpec((tm, LANES), lambda i: (i % period, 0))
    row_spec = pl.BlockSpec((tm, W_A), lambda i: (i, 0))
    t_spec = pl.BlockSpec((1, H_A, HD_A, tm), lambda i: (i // period, 0, 0, i % period))
    t_shape = jax.ShapeDtypeStruct((batch, H_A, HD_A, t), F32)
    return pl.pallas_call(
        _attn_proj_seq_kernel,
        grid=(n // tm,),
        in_specs=[pl.BlockSpec((tm, d), lambda i: (i, 0)),
                  pl.BlockSpec((1, d), lambda i: (0, 0)),
                  pl.BlockSpec((d, 3 * W_A), lambda i: (0, 0)),
                  tab_spec, tab_spec, tab_spec],
        out_specs=[row_spec, row_spec, t_spec, t_spec, pl.BlockSpec((1, bpt, W_A), lambda i: (i, 0, 0))],
        out_shape=[jax.ShapeDtypeStruct((n, W_A), F32), jax.ShapeDtypeStruct((n, W_A), BF16), t_shape, t_shape,
                   jax.ShapeDtypeStruct((n // tm, bpt, W_A), F32)],
        compiler_params=_cparams(("parallel",)),
        name="attn_proj_seq",
    )(x2d, g, w, *tabs)


def _gdn_proj_kernel(x_ref, g_ref, w_ref, wba_ref, cw_ref, c0_ref, al_ref, dtb_ref,
                     q_ref, k_ref, v_ref, bg_ref, tail_ref, ubuf, *, tm, t_valid, tail_row):
    i = pl.program_id(1)

    @pl.when(i == 0)
    def _():
        ubuf[0:SUBLANES, :] = c0_ref[0]

    @pl.when(i > 0)
    def _():
        ubuf[0:SUBLANES, :] = ubuf[tm:tm + SUBLANES, :]

    h = _rms(x_ref[0], g_ref[...]).astype(BF16)
    if t_valid % tm != 0:
        row = i * tm + lax.broadcasted_iota(jnp.int32, (tm, 1), 0)
        valid = (row < t_valid).astype(F32)
    else:
        valid = None

    def put(ref, s, val):
        ref[0, :, s * LANES:(s + 1) * LANES] = val if valid is None else val * valid

    cw = cw_ref[...]
    n_chunks = C_CONV // PROJ_CHUNK
    dot_chunk = lambda c: jnp.dot(h, w_ref[:, c * PROJ_CHUNK:(c + 1) * PROJ_CHUNK], preferred_element_type=F32)
    nxt = dot_chunk(0)
    for c in range(n_chunks):
        u = nxt
        if c + 1 < n_chunks:
            nxt = dot_chunk(c + 1)
        cols = slice(c * PROJ_CHUNK, (c + 1) * PROJ_CHUNK)
        ubuf[SUBLANES:SUBLANES + tm, cols] = u
        conv = u * cw[CONV_K - 1:CONV_K, cols]
        for j in range(1, CONV_K):
            conv = conv + ubuf[SUBLANES - j:SUBLANES - j + tm, cols] * cw[CONV_K - 1 - j:CONV_K - j, cols]
        act = _silu(conv)
        for s in range(PROJ_CHUNK // LANES):
            slab = c * (PROJ_CHUNK // LANES) + s
            z = act[:, s * LANES:(s + 1) * LANES]
            if slab < 2 * H_B:
                z = z * lax.rsqrt(jnp.sum(z * z, axis=-1, keepdims=True) + EPS)
            put((q_ref, k_ref, v_ref)[slab // H_B], slab % H_B, z)

    ba = jnp.dot(h, wba_ref[...], preferred_element_type=F32)
    lane = lax.broadcasted_iota(jnp.int32, (1, LANES), 1)
    z = ba + dtb_ref[...]
    softplus = jnp.maximum(z, 0.0) + jnp.log1p(jnp.exp(-jnp.abs(z)))
    bg = jnp.where(lane < H_B, jax.nn.sigmoid(ba), -jnp.exp(al_ref[...]) * softplus)
    bg_ref[0] = bg if valid is None else bg * valid
    tail_ref[0] = ubuf[tail_row:tail_row + SUBLANES, :]


def _gdn_proj(x3d, g, w_qkv, w_ba, conv_w, conv0_pad, al_row, dtb_row, tm, t_valid):
    b, t, d = x3d.shape
    n_tiles = -(-t_valid // tm)
    tail_row = t_valid - (n_tiles - 1) * tm
    kern = functools.partial(_gdn_proj_kernel, tm=tm, t_valid=t_valid, tail_row=tail_row)
    seq = lambda w: pl.BlockSpec((1, tm, w), lambda bi, i: (bi, i, 0))
    const = lambda r, c: pl.BlockSpec((r, c), lambda bi, i: (0, 0))
    return pl.pallas_call(
        kern,
        grid=(b, n_tiles),
        in_specs=[seq(d), const(1, d), const(d, C_CONV), const(d, LANES), const(CONV_K, C_CONV),
                  pl.BlockSpec((1, SUBLANES, C_CONV), lambda bi, i: (bi, 0, 0)),
                  const(1, LANES), const(1, LANES)],
        out_specs=[seq(H_B * DK_B), seq(H_B * DK_B), seq(W_B), seq(LANES),
                   pl.BlockSpec((1, SUBLANES, C_CONV), lambda bi, i: (bi, 0, 0))],
        out_shape=[jax.ShapeDtypeStruct((b, n_tiles * tm, H_B * DK_B), F32),
                   jax.ShapeDtypeStruct((b, n_tiles * tm, H_B * DK_B), F32),
                   jax.ShapeDtypeStruct((b, n_tiles * tm, W_B), F32),
                   jax.ShapeDtypeStruct((b, n_tiles * tm, LANES), F32),
                   jax.ShapeDtypeStruct((b, SUBLANES, C_CONV), F32)],
        scratch_shapes=[pltpu.VMEM((tm + 2 * SUBLANES, C_CONV), F32)],
        compiler_params=_cparams(("parallel", "arbitrary")),
        name="gdn_proj",
    )(x3d, g, w_qkv, w_ba, conv_w, conv0_pad, al_row, dtb_row)


def _gdn_tok_kernel(x_ref, g_ref, w_ref, wba_ref, cw_ref, c0_ref, al_ref, dtb_ref,
                    q_ref, k_ref, v_ref, bg_ref, tail_ref):
    h = _rms(x_ref[...], g_ref[...]).astype(BF16)
    cw = cw_ref[...]
    for c in range(C_CONV // PROJ_CHUNK):
        cols = slice(c * PROJ_CHUNK, (c + 1) * PROJ_CHUNK)
        u = jnp.dot(h, w_ref[:, cols], preferred_element_type=F32)
        conv = u * cw[CONV_K - 1:CONV_K, cols]
        for r in range(CONV_K - 1):
            conv = conv + c0_ref[r, :, cols] * cw[r:r + 1, cols]
            if r > 0:
                tail_ref[r - 1, :, cols] = c0_ref[r, :, cols]
        tail_ref[CONV_K - 2, :, cols] = u
        act = _silu(conv)
        for s in range(PROJ_CHUNK // LANES):
            slab = c * (PROJ_CHUNK // LANES) + s
            z = act[:, s * LANES:(s + 1) * LANES]
            if slab < 2 * H_B:
                z = z * lax.rsqrt(jnp.sum(z * z, axis=-1, keepdims=True) + EPS)
            (q_ref, k_ref, v_ref)[slab // H_B][:, (slab % H_B) * LANES:(slab % H_B + 1) * LANES] = z
    ba = jnp.dot(h, wba_ref[...], preferred_element_type=F32)
    lane = lax.broadcasted_iota(jnp.int32, (1, LANES), 1)
    z = ba + dtb_ref[...]
    softplus = jnp.maximum(z, 0.0) + jnp.log1p(jnp.exp(-jnp.abs(z)))
    bg_ref[...] = jnp.where(lane < H_B, jax.nn.sigmoid(ba), -jnp.exp(al_ref[...]) * softplus)


def _gdn_tok(x2d, g, w_qkv, w_ba, conv_w, c0, al_row, dtb_row):
    b, d = x2d.shape
    full = lambda *shape: pl.BlockSpec(shape, lambda i: (0,) * len(shape))
    return pl.pallas_call(
        _gdn_tok_kernel,
        grid=(1,),
        in_specs=[full(b, d), full(1, d), full(d, C_CONV), full(d, LANES), full(CONV_K, C_CONV),
                  full(CONV_K - 1, b, C_CONV), full(1, LANES), full(1, LANES)],
        out_specs=[full(b, H_B * DK_B), full(b, H_B * DK_B), full(b, W_B), full(b, LANES),
                   full(CONV_K - 1, b, C_CONV)],
        out_shape=[jax.ShapeDtypeStruct((b, H_B * DK_B), F32), jax.ShapeDtypeStruct((b, H_B * DK_B), F32),
                   jax.ShapeDtypeStruct((b, W_B), F32), jax.ShapeDtypeStruct((b, LANES), F32),
                   jax.ShapeDtypeStruct((CONV_K - 1, b, C_CONV), F32)],
        compiler_params=_cparams(("arbitrary",)),
        name="gdn_tok",
    )(x2d, g, w_qkv, w_ba, conv_w, c0, al_row, dtb_row)


def _top3(gate, idx, axis, big):
    sels = []
    g = gate
    for _ in range(MOBA_TOPK):
        m = jnp.max(g, axis=axis, keepdims=True)
        ii = jnp.min(jnp.where(g == m, idx, big), axis=axis, keepdims=True)
        sels.append(ii)
        g = jnp.where(idx == ii, -jnp.inf, g)
    return sels


def _moba_prompt_kernel(pt_ref, q_ref, k_ref, vt_ref, kbar_ref, *rest, nb, n_pg):
    if n_pg:
        qb_ref, page_refs, (o_ref, sc_ref, vt_scr) = rest[0], rest[1:1 + n_pg], rest[1 + n_pg:]
    else:
        o_ref, vt_scr = rest
    i = pl.program_id(2)
    tq = MOBA_BLOCK
    n_heads = LANES // HD_A

    @pl.when(i == 0)
    def _():
        ones = jnp.ones((V_ROWS - HD_A, MOBA_BLOCK), BF16)
        for j in range(vt_scr.shape[0]):
            for hh in range(n_heads):
                if j < nb:
                    vt_scr[j, hh, 0:HD_A, :] = vt_ref[0, hh, :, j * MOBA_BLOCK:(j + 1) * MOBA_BLOCK].astype(BF16)
                    vt_scr[j, hh, HD_A:V_ROWS, :] = ones
                else:
                    vt_scr[j, hh] = jnp.zeros((V_ROWS, MOBA_BLOCK), BF16)

    def k_block(j):
        start = pl.multiple_of(jnp.minimum(j, nb - 1) * MOBA_BLOCK, MOBA_BLOCK)
        return k_ref[0, pl.ds(start, MOBA_BLOCK), :]

    q_t = q_ref[0].T
    kbar = kbar_ref[0]
    dim_head = lax.broadcasted_iota(jnp.int32, (LANES, 1), 0) // HD_A
    blk = lax.broadcasted_iota(jnp.int32, (nb, 1), 0)
    blk_f = blk.astype(F32)
    key_i = lax.broadcasted_iota(jnp.int32, (tq, tq), 0)
    qry_i = lax.broadcasted_iota(jnp.int32, (tq, tq), 1)
    heads = range(n_heads)

    def attend(j0, n_blk, keep, prev):
        raws = [[jnp.dot(kg, qs[hh], preferred_element_type=F32) for hh in heads]
                for kg in [k_block(j0 + g) for g in range(n_blk)]]
        ms = [None if prev is None else prev[hh][0] for hh in heads]
        m_at = [[None] * n_blk for _ in heads]
        pvs = [[None] * n_blk for _ in heads]
        for g in range(n_blk):
            for hh in heads:
                s = jnp.where(keep[hh][g], raws[g][hh].astype(BF16), NEG_BIG)
                m = jnp.max(s, axis=0, keepdims=True)
                if ms[hh] is not None:
                    m = jnp.maximum(ms[hh], m)
                p = jnp.exp2(s - m)
                pvs[hh][g] = jnp.dot(vt_scr[j0 + g, hh], p, preferred_element_type=F32)
                m_at[hh][g] = ms[hh] = m
        out = []
        for hh in heads:
            m_fin = ms[hh]
            rescale = lambda m_old: jnp.exp2(m_old.astype(F32) - m_fin.astype(F32))
            accl = None if prev is None else rescale(prev[hh][0]) * prev[hh][1]
            for g in range(n_blk):
                term = pvs[hh][g] if g == n_blk - 1 else rescale(m_at[hh][g]) * pvs[hh][g]
                accl = term if accl is None else accl + term
            out.append((m_fin, accl))
        return out

    qts = [jnp.where(dim_head == hh, q_t, 0.0) for hh in heads]
    qs = [(qts[hh] * (HD_A ** -0.5 * math.log2(math.e))).astype(BF16) for hh in heads]
    causal = key_i <= qry_i
    state0 = attend(i, 1, [[causal]] * n_heads, None)
    sels = []
    for hh in heads:
        gate = jnp.where(blk < i, _dot_hi(kbar, qts[hh]), -jnp.inf)
        sels.append(_top3(gate, blk_f, 0, float(nb)))
    for hh in range(H_A if n_pg else 0):
        q_h = qb_ref[0, hh]
        for r in range(n_pg):
            sc_ref[0, r, hh:hh + 1, :] = jnp.sum(page_refs[r][0, 0, hh] * q_h, axis=0, keepdims=True)

    def body(t, carry):
        j0 = t * KV_GROUP
        keep = []
        for hh in heads:
            i1, i2, i3 = sels[hh]
            jfs = [(j0 + g).astype(F32) for g in range(KV_GROUP)]
            keep.append([(i1 == jf) | (i2 == jf) | (i3 == jf) for jf in jfs])
        state = attend(j0, KV_GROUP, keep, [carry[2 * hh:2 * hh + 2] for hh in heads])
        return tuple(x for st in state for x in st)

    carry = lax.fori_loop(0, (i + KV_GROUP - 1) // KV_GROUP, body, tuple(x for st in state0 for x in st))
    o_t = jnp.concatenate([carry[2 * hh + 1][:HD_A] / carry[2 * hh + 1][HD_A:HD_A + 1] for hh in heads],
                          axis=0)
    o_ref[0] = o_t.T


MAX_RIDE_PAGES = 16


def _moba_prompt(q3d, kb3d, vt4d, kbar, paged=None):
    b, t, w = q3d.shape
    nb = t // MOBA_BLOCK
    nb_pad = -(-nb // KV_GROUP) * KV_GROUP
    n_pairs = w // LANES
    hpp = LANES // HD_A
    n_steps = b * n_pairs * nb
    n_pg, pt_flat = 0, jnp.zeros((1,), jnp.int32)
    if paged is not None:
        q_b, cache_kt, layer, pt_flat_p, n_pages = paged
        bs, h, hd, page = q_b.shape
        per_step = (bs * n_pages) // n_steps
        if (bs * n_pages) % n_steps == 0 and 0 < per_step <= MAX_RIDE_PAGES and n_pages % per_step == 0:
            n_pg, pt_flat = per_step, pt_flat_p
    qspec = pl.BlockSpec((1, MOBA_BLOCK, LANES), lambda bi, hp, i, pt: (bi, i, hp))
    in_specs = [qspec,
                pl.BlockSpec((1, t, LANES), lambda bi, hp, i, pt: (bi, 0, hp)),
                pl.BlockSpec((1, hpp, HD_A, t), lambda bi, hp, i, pt: (bi, hp, 0, 0)),
                pl.BlockSpec((1, nb, LANES), lambda bi, hp, i, pt: (bi, 0, hp))]
    out_specs = [qspec]
    out_shape = [jax.ShapeDtypeStruct((b, t, w), F32)]
    operands = [q3d, kb3d, vt4d, kbar]
    if n_pg:
        spr = n_pages // n_pg
        flat = lambda bi, hp, i: (bi * n_pairs + hp) * nb + i
        in_specs.append(pl.BlockSpec((1, h, hd, page), lambda bi, hp, i, pt: (flat(bi, hp, i) // spr, 0, 0, 0)))
        in_specs += [pl.BlockSpec((1, 1, h, hd, page),
                                  lambda bi, hp, i, pt, r=r: (layer, pt[(flat(bi, hp, i) // spr) * n_pages
                                                                        + (flat(bi, hp, i) % spr) * n_pg + r],
                                                              0, 0, 0))
                     for r in range(n_pg)]
        out_specs.append(pl.BlockSpec((1, n_pg, h, page),
                                      lambda bi, hp, i, pt: (flat(bi, hp, i) // spr, flat(bi, hp, i) % spr, 0, 0)))
        out_shape.append(jax.ShapeDtypeStruct((bs, n_pages, h, page), F32))
        operands += [q_b] + [cache_kt] * n_pg
    res = pl.pallas_call(
        functools.partial(_moba_prompt_kernel, nb=nb, n_pg=n_pg),
        grid_spec=pltpu.PrefetchScalarGridSpec(
            num_scalar_prefetch=1,
            grid=(b, n_pairs, nb),
            in_specs=in_specs,
            out_specs=out_specs,
            scratch_shapes=[pltpu.VMEM((nb_pad, hpp, V_ROWS, MOBA_BLOCK), BF16)],
        ),
        out_shape=out_shape,
        compiler_params=_cparams(("arbitrary", "arbitrary", "arbitrary")),
        name="moba_prompt",
    )(pt_flat, *operands)
    return res[0], (res[1] if n_pg else None)


def _gdn_seq_kernel(q_ref, k_ref, v_ref, bg_ref, s0_ref, gn_ref, o_ref, s_out_ref, s_scr, *, rows):
    step = pl.program_id(1)
    n_steps = pl.num_programs(1)
    cc = GDN_CHUNK
    ncs = rows // cc

    @pl.when(step == 0)
    def _():
        s_scr[...] = s0_ref[0]

    bg = bg_ref[0]
    gn = gn_ref[...]
    ri = lax.broadcasted_iota(jnp.int32, (rows, rows), 0)
    ci = lax.broadcasted_iota(jnp.int32, (rows, rows), 1)
    same = (ri // cc) == (ci // cc)
    sum_mats = jnp.concatenate([(same & (ci <= ri)).astype(BF16), same.astype(BF16)], axis=0)
    b_hi = bg.astype(BF16)
    r_1 = bg - b_hi.astype(F32)
    b_mid = r_1.astype(BF16)
    b_lo = (r_1 - b_mid.astype(F32)).astype(BF16)
    sums = sum(jnp.dot(sum_mats, piece, preferred_element_type=F32) for piece in (b_hi, b_mid, b_lo))
    gc = sums[:rows]
    gl = sums[rows:]
    gc_t = gc.T
    e_gc = jnp.exp(gc)
    e_tail = jnp.exp(gl - gc)
    e_tot = jnp.exp(gl)
    pr = lax.broadcasted_iota(jnp.int32, (cc, rows), 0)
    pl_i = lax.broadcasted_iota(jnp.int32, (cc, rows), 1)
    pc = pl_i % cc
    lane_blk = pl_i // cc
    causal_p = pr >= pc
    strict_p = pr > pc

    def pack(full):
        out = full[(ncs - 1) * cc:ncs * cc]
        for c in range(ncs - 2, -1, -1):
            out = jnp.where(lane_blk == c, full[c * cc:(c + 1) * cc], out)
        return out

    def pack_col(col):
        out = col[(ncs - 1) * cc:ncs * cc]
        for c in range(ncs - 2, -1, -1):
            out = jnp.where(lane_blk == c, col[c * cc:(c + 1) * cc], out)
        return out

    def bdiag(p):
        return jnp.where(same, jnp.concatenate([p] * ncs, axis=0), 0.0)

    heads = range(H_B)
    col = lambda a, h: a[:, H_B + h:H_B + h + 1]
    qs = [q_ref[0, :, h * DK_B:(h + 1) * DK_B] * (DK_B ** -0.5) for h in heads]
    ks = [k_ref[0, :, h * DK_B:(h + 1) * DK_B] for h in heads]
    kbetas = [ks[h] * bg[:, h:h + 1] for h in heads]
    fulls = [_dot_nt(jnp.concatenate([kbetas[h], qs[h]], axis=0), ks[h]) for h in heads]
    decays = [jnp.exp(jnp.where(causal_p, pack_col(col(gc, h)) - gc_t[H_B + h:H_B + h + 1, :], -jnp.inf))
              for h in heads]
    attns = [jnp.where(causal_p, pack(fulls[h][rows:]) * decays[h], 0.0) for h in heads]
    es = [-jnp.where(strict_p, pack(fulls[h][:rows]) * decays[h], 0.0) for h in heads]
    pws = [_dot(es[h], bdiag(es[h])) for h in heads]
    for _ in range(int(math.log2(cc)) - 2):
        rs_ = [_dot(jnp.concatenate([es[h], pws[h]], axis=0), bdiag(pws[h])) for h in heads]
        es = [es[h] + pws[h] + rs_[h][:cc] for h in heads]
        pws = [rs_[h][cc:] for h in heads]
    es = [es[h] + pws[h] + _dot(es[h], bdiag(pws[h])) for h in heads]
    rhss = [jnp.concatenate([v_ref[0, :, h * DV_B:(h + 1) * DV_B] * bg[:, h:h + 1],
                             kbetas[h] * col(e_gc, h)], axis=1) for h in heads]
    sols = [rhss[h] + _dot(bdiag(es[h]), rhss[h]) for h in heads]
    q_decs = [qs[h] * col(e_gc, h) for h in heads]
    k_tails = [ks[h] * col(e_tail, h) for h in heads]
    ss = [s_scr[h] for h in heads]
    for c in range(ncs):
        rs = slice(c * cc, (c + 1) * cc)
        wqs = [_dot(jnp.concatenate([sols[h][rs, DV_B:], q_decs[h][rs]], axis=0), ss[h]) for h in heads]
        v_news = [sols[h][rs, :DV_B] - wqs[h][:cc] for h in heads]
        os_ = [wqs[h][cc:] + _dot(attns[h][:, rs], v_news[h]) for h in heads]
        kvs = [_dot_tn(k_tails[h][rs], v_news[h]) for h in heads]
        ss = [ss[h] * e_tot[c * cc:c * cc + 1, H_B + h:H_B + h + 1] + kvs[h] for h in heads]
        for h in heads:
            o_ref[0, rs, h * DV_B:(h + 1) * DV_B] = _rms(os_[h], gn)
    for h in heads:
        s_scr[h] = ss[h]


    @pl.when(step == n_steps - 1)
    def _():
        s_out_ref[0] = s_scr[...]


def _gdn_seq(q, k, v, bg, s0, g_onorm_row):
    b, t, _ = q.shape
    rows = MXU_DIM
    while t % rows:
        rows //= 2
    seq = lambda w: pl.BlockSpec((1, rows, w), lambda bi, c: (bi, c, 0))
    sspec = pl.BlockSpec((1, H_B, DK_B, DV_B), lambda bi, c: (bi, 0, 0, 0))
    return pl.pallas_call(
        functools.partial(_gdn_seq_kernel, rows=rows),
        grid=(b, t // rows),
        in_specs=[seq(H_B * DK_B), seq(H_B * DK_B), seq(W_B), seq(LANES), sspec,
                  pl.BlockSpec((1, DV_B), lambda bi, c: (0, 0))],
        out_specs=[seq(W_B), sspec],
        out_shape=[jax.ShapeDtypeStruct((b, t, W_B), F32),
                   jax.ShapeDtypeStruct((b, H_B, DK_B, DV_B), F32)],
        scratch_shapes=[pltpu.VMEM((H_B, DK_B, DV_B), F32)],
        compiler_params=_cparams(("parallel", "arbitrary")),
        name="gdn_seq",
    )(q, k, v, bg, s0, g_onorm_row)


def _gdn_step_kernel(kq_ref, v_ref, bg_ref, s0_ref, gn_ref, o_ref, s_out_ref):
    bg = bg_ref[0]
    gn = gn_ref[...]
    kq = kq_ref[0]
    for h in range(H_B):
        beta = bg[:, h:h + 1]
        eg = jnp.exp(bg[:, H_B + h:H_B + h + 1])
        k_col = kq[:, h:h + 1]
        q_col = kq[:, H_B + h:H_B + h + 1] * (DK_B ** -0.5)
        v_row = v_ref[0, :, h * DV_B:(h + 1) * DV_B]
        s0 = s0_ref[0, h]
        k_s = jnp.sum(k_col * s0, axis=0, keepdims=True)
        q_s = jnp.sum(q_col * s0, axis=0, keepdims=True)
        v_new = beta * (v_row - eg * k_s)
        qk = jnp.sum(q_col * k_col, axis=0, keepdims=True)
        o = eg * q_s + qk * v_new
        s_out_ref[0, h] = s0 * eg + k_col * v_new
        o_ref[0, :, h * DV_B:(h + 1) * DV_B] = _rms(o, gn)


def _gdn_step(kq_cols, v, bg, s0, g_onorm_row):
    b = v.shape[0]
    tok = lambda w: pl.BlockSpec((1, 1, w), lambda bi: (bi, 0, 0))
    sspec = pl.BlockSpec((1, H_B, DK_B, DV_B), lambda bi: (bi, 0, 0, 0))
    return pl.pallas_call(
        _gdn_step_kernel,
        grid=(b,),
        in_specs=[pl.BlockSpec((1, DK_B, 2 * H_B), lambda bi: (bi, 0, 0)), tok(W_B), tok(LANES), sspec,
                  pl.BlockSpec((1, DV_B), lambda bi: (0, 0))],
        out_specs=[tok(W_B), sspec],
        out_shape=[jax.ShapeDtypeStruct((b, 1, W_B), F32),
                   jax.ShapeDtypeStruct((b, H_B, DK_B, DV_B), F32)],
        compiler_params=_cparams(("parallel",)),
        name="gdn_step",
    )(kq_cols, v, bg, s0, g_onorm_row)


def _out_stage_kernel(x_ref, oa_ref, ob_ref, p_ref, gmix_ref, wzg_ref, wpa_ref, wpb_ref, wo_ref,
                      gple_ref, wpg_ref, wple_ref, gfin_ref, y_ref, *, final):
    x = x_ref[...]
    h = _rms(x, gmix_ref[...]).astype(BF16)
    zg = jnp.dot(h, wzg_ref[...], preferred_element_type=F32)
    za = zg[:, :W_A]
    zb = zg[:, W_A:W_A + W_B]
    d = x.shape[1]
    ga = zg[:, W_A + W_B:W_A + W_B + d]
    gb = zg[:, W_A + W_B + d:]
    ya = _dot(oa_ref[...] * _silu(za), wpa_ref[...])
    yb = _dot(ob_ref[...] * _silu(zb), wpb_ref[...])
    mixed = jax.nn.sigmoid(ga) * ya + jax.nn.sigmoid(gb) * yb
    x = x + _dot(mixed, wo_ref[...])
    gate = jax.nn.sigmoid(_dot(_rms(x, gple_ref[...]), wpg_ref[...]))
    x = x + gate * _dot(p_ref[...], wple_ref[...])
    y_ref[...] = _rms(x, gfin_ref[...]) if final else x


def _out_stage(x2d, oa, ob, p2d, gmix, wzg, wpa, wpb, wo, gple, wpg, wple, gfin, tm, final):
    n, d = x2d.shape
    row = lambda w: pl.BlockSpec((tm, w), lambda i: (i, 0))
    const = lambda a: pl.BlockSpec(a.shape, lambda i: (0, 0))
    return pl.pallas_call(
        functools.partial(_out_stage_kernel, final=final),
        grid=(n // tm,),
        in_specs=[row(d), row(W_A), row(W_B), row(p2d.shape[1]), const(gmix), const(wzg), const(wpa),
                  const(wpb), const(wo), const(gple), const(wpg), const(wple), const(gfin)],
        out_specs=row(d),
        out_shape=jax.ShapeDtypeStruct((n, d), F32),
        compiler_params=_cparams(("parallel",)),
        name="out_stage",
    )(x2d, oa, ob, p2d, gmix, wzg, wpa, wpb, wo, gple, wpg, wple, gfin)


PAGES_PER_STEP = 32


def _page_scores_kernel(pt_ref, q_ref, *refs):
    k_refs, o_ref = refs[:-1], refs[-1]
    qb = q_ref[0]
    for r, k_ref in enumerate(k_refs):
        o_ref[0, r] = jnp.sum(k_ref[0, 0] * qb, axis=1)


def _page_scores(q_b, cache_kt, layer, pt_flat, n_pages):
    b = q_b.shape[0]
    _, _, h, hd, page = cache_kt.shape
    nps = PAGES_PER_STEP if n_pages % PAGES_PER_STEP == 0 else 1
    in_specs = [pl.BlockSpec((1, h, hd, page), lambda bi, g, pt: (bi, 0, 0, 0))]
    in_specs += [pl.BlockSpec((1, 1, h, hd, page),
                              lambda bi, g, pt, r=r: (layer, pt[bi * n_pages + g * nps + r], 0, 0, 0))
                 for r in range(nps)]
    return pl.pallas_call(
        _page_scores_kernel,
        grid_spec=pltpu.PrefetchScalarGridSpec(
            num_scalar_prefetch=1,
            grid=(b, n_pages // nps),
            in_specs=in_specs,
            out_specs=pl.BlockSpec((1, nps, h, page), lambda bi, g, pt: (bi, g, 0, 0)),
        ),
        out_shape=jax.ShapeDtypeStruct((b, n_pages, h, page), F32),
        compiler_params=_cparams(("parallel", "parallel")),
        name="page_scores",
    )(pt_flat, q_b, *([cache_kt] * nps))


def _sample_select_kernel(p_ref, o_ref, *, nblk, ppb):
    tot = jnp.sum(p_ref[0], axis=2, keepdims=True)
    gate = jnp.sum(tot.reshape(nblk, ppb, H_A, 1), axis=1) * (1.0 / MOBA_BLOCK)
    blk_f = lax.broadcasted_iota(jnp.int32, (nblk, 1, 1), 0).astype(F32)
    for r, sel in enumerate(_top3(gate, blk_f, 0, float(nblk))):
        o_ref[0, r] = jnp.broadcast_to(jnp.minimum(sel[0], nblk - 1.0), (H_A, LANES)).astype(jnp.int32)


def _sample_select(scores, ppb):
    b, n_pages, h, page = scores.shape
    return pl.pallas_call(
        functools.partial(_sample_select_kernel, nblk=n_pages // ppb, ppb=ppb),
        grid=(b,),
        in_specs=[pl.BlockSpec((1, n_pages, h, page), lambda bi: (bi, 0, 0, 0))],
        out_specs=pl.BlockSpec((1, MOBA_TOPK, h, LANES), lambda bi: (bi, 0, 0, 0)),
        out_shape=jax.ShapeDtypeStruct((b, MOBA_TOPK, h, LANES), jnp.int32),
        compiler_params=_cparams(("parallel",)),
        name="sample_select",
    )(scores)


PAGED_HEADS_PER_STEP = 4


def _paged_attn_kernel(pt_ref, top_ref, q_ref, kn_ref, vn_ref, s_ref, *refs, n_s, ppb):
    v_refs, o_ref = refs[:-1], refs[-1]
    bi = pl.program_id(0)
    hg = pl.program_id(1)
    scale = HD_A ** -0.5
    lane = lax.broadcasted_iota(jnp.int32, (1, H_A), 1)
    row = lax.broadcasted_iota(jnp.int32, (H_A, 1), 0)
    out = jnp.zeros((HD_A, H_A), F32)
    for j in range(PAGED_HEADS_PER_STEP):
        h = hg * PAGED_HEADS_PER_STEP + j
        pick = lambda a, h=h: jnp.sum(jnp.where(row == h, a, 0.0), axis=0, keepdims=True)
        own = jnp.sum(pick(q_ref[0] * kn_ref[0]), axis=1, keepdims=True) * scale
        rows = []
        for s in range(n_s):
            page = top_ref[(bi * MOBA_TOPK + s // ppb) * H_A + h] * ppb + s % ppb
            rows.append(pick(s_ref[0, page]) * scale)
        m = own
        for r in rows:
            m = jnp.maximum(m, jnp.max(r, axis=1, keepdims=True))
        p_own = jnp.exp(own - m)
        ps = [jnp.exp(r - m) for r in rows]
        l = p_own + sum(jnp.sum(p, axis=1, keepdims=True) for p in ps)
        acc = sum(p * v_refs[j * n_s + s][0, 0, 0] for s, p in enumerate(ps))
        vn_col = jnp.sum(jnp.where(lane == h, vn_ref[0], 0.0), axis=1, keepdims=True)
        res = (jnp.sum(acc, axis=1, keepdims=True) + p_own * vn_col) / l
        out = jnp.where(lane == h, res, out)

    @pl.when(hg == 0)
    def _():
        o_ref[0] = out

    @pl.when(hg > 0)
    def _():
        o_ref[0] = o_ref[0] + out


def _paged_attn(q3, kn3, vn_t, scores, cache_vt, layer, pt_flat, top_flat, n_pages):
    b = q3.shape[0]
    _, _, h, hd, page = cache_vt.shape
    ppb = MOBA_BLOCK // page
    n_s = MOBA_TOPK * ppb

    hps = PAGED_HEADS_PER_STEP

    def v_spec(j, s):
        def index(bi, hg, pt, top):
            hi = hg * hps + j
            logical = top[(bi * MOBA_TOPK + s // ppb) * H_A + hi] * ppb + s % ppb
            return (layer, pt[bi * n_pages + logical], hi, 0, 0)
        return pl.BlockSpec((1, 1, 1, hd, page), index)

    tok = pl.BlockSpec((1, h, hd), lambda bi, hg, pt, top: (bi, 0, 0))
    tok_t = pl.BlockSpec((1, hd, h), lambda bi, hg, pt, top: (bi, 0, 0))
    return pl.pallas_call(
        functools.partial(_paged_attn_kernel, n_s=n_s, ppb=ppb),
        grid_spec=pltpu.PrefetchScalarGridSpec(
            num_scalar_prefetch=2,
            grid=(b, h // hps),
            in_specs=[tok, tok, tok_t,
                      pl.BlockSpec((1, n_pages, h, page), lambda bi, hg, pt, top: (bi, 0, 0, 0))]
            + [v_spec(j, s) for j in range(hps) for s in range(n_s)],
            out_specs=tok_t,
        ),
        out_shape=jax.ShapeDtypeStruct((b, hd, h), F32),
        compiler_params=_cparams(("parallel", "arbitrary")),
        name="paged_attn",
    )(pt_flat, top_flat, q3, kn3, vn_t, scores, *([cache_vt] * (hps * n_s)))


def _split_weights(w_in_l, d):
    o = 0
    parts = {}
    for name, n in (("qa", W_A), ("ka", W_A), ("va", W_A), ("za", W_A), ("qb", H_B * DK_B),
                    ("kb", H_B * DK_B), ("vb", W_B), ("zb", W_B), ("beta", H_B), ("alpha", H_B),
                    ("ga", d), ("gb", d)):
        parts[name] = w_in_l[:, o:o + n]
        o += n
    w_attn = jnp.concatenate([parts["qa"], parts["ka"], parts["va"]], axis=1).astype(BF16)
    w_qkv = jnp.concatenate([parts["qb"], parts["kb"], parts["vb"]], axis=1).astype(BF16)
    w_ba = jnp.concatenate([parts["beta"], parts["alpha"],
                            jnp.zeros((d, LANES - 2 * H_B), F32)], axis=1).astype(BF16)
    w_zg = jnp.concatenate([parts["za"], parts["zb"], parts["ga"], parts["gb"]], axis=1).astype(BF16)
    return w_attn, w_qkv, w_ba, w_zg


def _lane_row(vec, offset):
    out = jnp.zeros((1, LANES), F32)
    return out.at[0, offset:offset + vec.shape[0]].set(vec.astype(F32))


def _pick_tile(n, pref):
    t = pref
    while n % t:
        t //= 2
    return t


def kernel(x_prompt, x_sample, p_prompt, p_sample, cache_k, cache_v, page_table, state_gdn_s, state_gdn_conv, g_mix, w_in, conv_w, a_log, dt_bias, g_onorm, w_pa, w_pb, w_o, g_ple, w_ple_gate, w_ple, g_final):
    bp, tp, d = x_prompt.shape
    bs, ts, _ = x_sample.shape
    depth = w_in.shape[0]
    n_pages = page_table.shape[1]
    page = cache_k.shape[2]
    assert ts == 1 and tp % MOBA_BLOCK == 0 and (n_pages * page) % MOBA_BLOCK == 0
    assert tp % GDN_CHUNK == 0

    pos_p = jnp.arange(tp, dtype=jnp.int32)
    pos_s = jnp.full((bs,), n_pages * page, dtype=jnp.int32)
    tabs_p = _rope_tables(pos_p)
    tabs_s = _rope_tables(pos_s)
    pt_flat = page_table.reshape(-1).astype(jnp.int32)
    cache_kt = jnp.transpose(cache_k, (0, 1, 3, 4, 2))
    cache_vt = jnp.transpose(cache_v, (0, 1, 3, 4, 2))

    xp = x_prompt.reshape(bp * tp, d)
    xs = x_sample.reshape(bs, d)
    outs = {k: [] for k in ("kp", "vp", "sp", "cp", "ks", "vs", "ss", "cs")}
    for l in range(depth):
        final = l == depth - 1
        w_attn, w_qkv, w_ba, w_zg = _split_weights(w_in[l], d)
        gmix = g_mix[l].reshape(1, d)
        al_row = _lane_row(a_log[l], H_B)
        dtb_row = _lane_row(dt_bias[l], H_B)
        gon = g_onorm[l].reshape(1, DV_B)
        wpa, wpb, wo = w_pa[l].astype(BF16), w_pb[l].astype(BF16), w_o[l].astype(BF16)
        wpg, wple = w_ple_gate[l].astype(BF16), w_ple[l].astype(BF16)
        gple = g_ple[l].reshape(1, d)
        gfin = g_final.reshape(1, d)

        qa_s, ka_s, va_s = _attn_proj(xs, gmix, w_attn, tabs_s, bs)
        q3 = qa_s.reshape(bs, H_A, HD_A)
        q_b = jnp.broadcast_to(q3[..., None], (bs, H_A, HD_A, page))

        qa, kb, kt, vt, kbar = _attn_proj_seq(xp, gmix, w_attn, tabs_p, _pick_tile(tp, 512), bp)
        conv0 = jnp.zeros((bp, SUBLANES, C_CONV), F32)
        qn, kn, vn, bg, tail = _gdn_proj(xp.reshape(bp, tp, d), gmix, w_qkv, w_ba, conv_w[l], conv0,
                                         al_row, dtb_row, _pick_tile(tp, 256), tp)
        oa, scores = _moba_prompt(qa.reshape(bp, tp, W_A), kb.reshape(bp, tp, W_A), vt,
                                  kbar.reshape(bp, tp // MOBA_BLOCK, W_A), (q_b, cache_kt, l, pt_flat, n_pages))
        s0 = jnp.zeros((bp, H_B, DK_B, DV_B), F32)
        ob, s_fin = _gdn_seq(qn, kn, vn, bg, s0, gon)
        xp = _out_stage(xp, oa.reshape(bp * tp, W_A), ob.reshape(bp * tp, W_B),
                        p_prompt[l].reshape(bp * tp, -1), gmix, w_zg, wpa, wpb, wo, gple, wpg, wple,
                        gfin, _pick_tile(bp * tp, 256), final)
        outs["kp"].append(jnp.transpose(kt, (0, 3, 1, 2)))
        outs["vp"].append(jnp.transpose(vt, (0, 3, 1, 2)))
        outs["sp"].append(s_fin.astype(state_gdn_s.dtype))
        outs["cp"].append(tail[:, SUBLANES - (CONV_K - 1):, :])

        conv0_s = jnp.transpose(state_gdn_conv[l].astype(F32), (1, 0, 2))
        qn_s, kn_s, vn_s, bg_s, tail_s = _gdn_tok(xs, gmix, w_qkv, w_ba, conv_w[l], conv0_s, al_row, dtb_row)
        if scores is None:
            scores = _page_scores(q_b, cache_kt, l, pt_flat, n_pages)
        top = _sample_select(scores, MOBA_BLOCK // page)
        top_flat = top[:, :, :, 0].reshape(-1)
        vn_t = jnp.transpose(va_s.reshape(bs, H_A, HD_A), (0, 2, 1))
        oa_t = _paged_attn(q3, ka_s.reshape(bs, H_A, HD_A), vn_t, scores, cache_vt, l, pt_flat, top_flat,
                           n_pages)
        oa_s = jnp.transpose(oa_t, (0, 2, 1))
        cols = lambda a: jnp.transpose(a.reshape(bs, H_B, DK_B), (0, 2, 1))
        kq_cols = jnp.concatenate([cols(kn_s), cols(qn_s)], axis=2)
        ob_s, s_fin_s = _gdn_step(kq_cols, vn_s.reshape(bs, 1, W_B), bg_s.reshape(bs, 1, LANES),
                                  state_gdn_s[l].astype(F32), gon)
        xs = _out_stage(xs, oa_s.reshape(bs, W_A), ob_s[:, 0, :], p_sample[l].reshape(bs, -1), gmix, w_zg,
                        wpa, wpb, wo, gple, wpg, wple, gfin, bs, final)
        outs["ks"].append(ka_s.reshape(bs, 1, H_A, HD_A))
        outs["vs"].append(va_s.reshape(bs, 1, H_A, HD_A))
        outs["ss"].append(s_fin_s.astype(state_gdn_s.dtype))
        outs["cs"].append(jnp.transpose(tail_s, (1, 0, 2)))

    y_prompt = xp.reshape(bp, tp, d)
    y_sample = xs.reshape(bs, ts, d)
    st = lambda k: jnp.stack(outs[k])
    return (y_prompt, y_sample, st("kp"), st("vp"), st("sp"), st("cp"),
            st("ks"), st("vs"), st("ss"), st("cs"))
```

```python
import functools
import math

import jax
import jax.numpy as jnp
from jax import lax
from jax.experimental import pallas as pl
from jax.experimental.pallas import tpu as pltpu

F32 = jnp.float32
BF16 = jnp.bfloat16
HIGHEST = lax.Precision.HIGHEST

H_A = 8
HD_A = 64
W_A = H_A * HD_A
ROT_DIM = HD_A // 4
ROPE_THETA = 500000.0
MOBA_BLOCK = 256
MOBA_TOPK = 3
H_B = 8
DK_B = 128
DV_B = 128
W_B = H_B * DV_B
CONV_K = 4
C_CONV = H_B * (2 * DK_B + DV_B)
GDN_CHUNK = 64
EPS = 1e-6

LANES = 128
SUBLANES = 8
MXU_DIM = 256
VMEM_LIMIT_BYTES = 56 * 1024 * 1024

NEG_BIG = -1e30
KV_GROUP = 4
PROJ_CHUNK = 512
V_ROWS = HD_A + 16


def _cparams(sem):
    return pltpu.CompilerParams(dimension_semantics=sem, vmem_limit_bytes=VMEM_LIMIT_BYTES)


def _rms(x, g):
    return x * lax.rsqrt(jnp.mean(x * x, axis=-1, keepdims=True) + EPS) * g


def _silu(x):
    return x * jax.nn.sigmoid(x)


def _dot(a, b):
    return jnp.dot(a.astype(BF16), b.astype(BF16), preferred_element_type=F32)


def _dot_nt(a, b):
    return lax.dot_general(a.astype(BF16), b.astype(BF16), (((1,), (1,)), ((), ())),
                           preferred_element_type=F32)


def _dot_tn(a, b):
    return lax.dot_general(a.astype(BF16), b.astype(BF16), (((0,), (0,)), ((), ())),
                           preferred_element_type=F32)


def _dot_hi(a, b):
    return jnp.dot(a, b, precision=HIGHEST, preferred_element_type=F32)


def _dot_nt_hi(a, b):
    return lax.dot_general(a, b, (((1,), (1,)), ((), ())), precision=HIGHEST,
                           preferred_element_type=F32)


def _attn_qkv(x_ref, g_ref, w_ref, c_ref, sa_ref, sb_ref):
    h = _rms(x_ref[...], g_ref[...]).astype(BF16)
    y = jnp.dot(h, w_ref[...], preferred_element_type=F32)
    c, sa, sb = c_ref[...], sa_ref[...], sb_ref[...]
    half = ROT_DIM // 2
    slabs = lambda base: [y[:, base + s * LANES: base + (s + 1) * LANES] for s in range(W_A // LANES)]
    rope = lambda z: z * c + pltpu.roll(z, LANES - half, 1) * sa + pltpu.roll(z, half, 1) * sb
    return [rope(z) for z in slabs(0)], [rope(z) for z in slabs(W_A)], slabs(2 * W_A)


def _attn_proj_kernel(x_ref, g_ref, w_ref, c_ref, sa_ref, sb_ref, q_ref, k_ref, v_ref):
    qs, ks, vs = _attn_qkv(x_ref, g_ref, w_ref, c_ref, sa_ref, sb_ref)
    for s in range(W_A // LANES):
        q_ref[:, s * LANES:(s + 1) * LANES] = qs[s]
        k_ref[:, s * LANES:(s + 1) * LANES] = ks[s]
        v_ref[:, s * LANES:(s + 1) * LANES] = vs[s]


def _attn_proj_seq_kernel(x_ref, g_ref, w_ref, c_ref, sa_ref, sb_ref, q_ref, kb_ref, kt_ref, vt_ref, kbar_ref):
    qs, ks, vs = _attn_qkv(x_ref, g_ref, w_ref, c_ref, sa_ref, sb_ref)
    tm = x_ref.shape[0]
    hps = LANES // HD_A
    for s in range(W_A // LANES):
        q_ref[:, s * LANES:(s + 1) * LANES] = qs[s]
        kb_ref[:, s * LANES:(s + 1) * LANES] = ks[s].astype(BF16)
        kt_ref[0, s * hps:(s + 1) * hps] = ks[s].T.reshape(hps, HD_A, tm)
        vt_ref[0, s * hps:(s + 1) * hps] = vs[s].T.reshape(hps, HD_A, tm)
        kbar_ref[0, :, s * LANES:(s + 1) * LANES] = jnp.sum(
            ks[s].reshape(tm // MOBA_BLOCK, MOBA_BLOCK, LANES), axis=1) * (1.0 / MOBA_BLOCK)


def _rope_tables(pos):
    half = ROT_DIM // 2
    inv = jnp.power(ROPE_THETA, -jnp.arange(half, dtype=F32) / half)
    ang = pos.astype(F32)[:, None] * inv[None, :]
    cos, sin = jnp.cos(ang), jnp.sin(ang)
    t = pos.shape[0]
    ones = jnp.ones((t, HD_A - ROT_DIM), F32)
    zeros = jnp.zeros((t, HD_A - ROT_DIM), F32)
    zh = jnp.zeros((t, half), F32)
    c = jnp.concatenate([cos, cos, ones], axis=1)
    sa = jnp.concatenate([-sin, zh, zeros], axis=1)
    sb = jnp.concatenate([zh, sin, zeros], axis=1)
    rep = LANES // HD_A
    return tuple(jnp.tile(a, (1, rep)) for a in (c, sa, sb))


def _attn_proj(x2d, g, w, tabs, tm):
    n, d = x2d.shape
    t_tab = tabs[0].shape[0]
    period = t_tab // tm
    tab_spec = pl.BlockSpec((tm, LANES), lambda i: (i % period, 0))
    out_spec = pl.BlockSpec((tm, W_A), lambda i: (i, 0))
    out = jax.ShapeDtypeStruct((n, W_A), F32)
    return pl.pallas_call(
        _attn_proj_kernel,
        grid=(n // tm,),
        in_specs=[pl.BlockSpec((tm, d), lambda i: (i, 0)),
                  pl.BlockSpec((1, d), lambda i: (0, 0)),
                  pl.BlockSpec((d, 3 * W_A), lambda i: (0, 0)),
                  tab_spec, tab_spec, tab_spec],
        out_specs=[out_spec, out_spec, out_spec],
        out_shape=[out, out, out],
        compiler_params=_cparams(("parallel",)),
        name="attn_proj",
    )(x2d, g, w, *tabs)


def _attn_proj_seq(x2d, g, w, tabs, tm, batch):
    n, d = x2d.shape
    t = n // batch
    period = t // tm
    bpt = tm // MOBA_BLOCK
    tab_spec = pl.BlockSpec((tm, LANES), lambda i: (i % period, 0))
    row_spec = pl.BlockSpec((tm, W_A), lambda i: (i, 0))
    t_spec = pl.BlockSpec((1, H_A, HD_A, tm), lambda i: (i // period, 0, 0, i % period))
    t_shape = jax.ShapeDtypeStruct((batch, H_A, HD_A, t), F32)
    return pl.pallas_call(
        _attn_proj_seq_kernel,
        grid=(n // tm,),
        in_specs=[pl.BlockSpec((tm, d), lambda i: (i, 0)),
                  pl.BlockSpec((1, d), lambda i: (0, 0)),
                  pl.BlockSpec((d, 3 * W_A), lambda i: (0, 0)),
                  tab_spec, tab_spec, tab_spec],
        out_specs=[row_spec, row_spec, t_spec, t_spec, pl.BlockSpec((1, bpt, W_A), lambda i: (i, 0, 0))],
        out_shape=[jax.ShapeDtypeStruct((n, W_A), F32), jax.ShapeDtypeStruct((n, W_A), BF16), t_shape, t_shape,
                   jax.ShapeDtypeStruct((n // tm, bpt, W_A), F32)],
        compiler_params=_cparams(("parallel",)),
        name="attn_proj_seq",
    )(x2d, g, w, *tabs)


def _gdn_proj_kernel(x_ref, g_ref, w_ref, wba_ref, cw_ref, c0_ref, al_ref, dtb_ref,
                     q_ref, k_ref, v_ref, bg_ref, tail_ref, ubuf, *, tm, t_valid, tail_row):
    i = pl.program_id(1)

    @pl.when(i == 0)
    def _():
        ubuf[0:SUBLANES, :] = c0_ref[0]

    @pl.when(i > 0)
    def _():
        ubuf[0:SUBLANES, :] = ubuf[tm:tm + SUBLANES, :]

    h = _rms(x_ref[0], g_ref[...]).astype(BF16)
    if t_valid % tm != 0:
        row = i * tm + lax.broadcasted_iota(jnp.int32, (tm, 1), 0)
        valid = (row < t_valid).astype(F32)
    else:
        valid = None

    def put(ref, s, val):
        ref[0, :, s * LANES:(s + 1) * LANES] = val if valid is None else val * valid

    cw = cw_ref[...]
    n_chunks = C_CONV // PROJ_CHUNK
    dot_chunk = lambda c: jnp.dot(h, w_ref[:, c * PROJ_CHUNK:(c + 1) * PROJ_CHUNK], preferred_element_type=F32)
    nxt = dot_chunk(0)
    for c in range(n_chunks):
        u = nxt
        if c + 1 < n_chunks:
            nxt = dot_chunk(c + 1)
        cols = slice(c * PROJ_CHUNK, (c + 1) * PROJ_CHUNK)
        ubuf[SUBLANES:SUBLANES + tm, cols] = u
        conv = u * cw[CONV_K - 1:CONV_K, cols]
        for j in range(1, CONV_K):
            conv = conv + ubuf[SUBLANES - j:SUBLANES - j + tm, cols] * cw[CONV_K - 1 - j:CONV_K - j, cols]
        act = _silu(conv)
        for s in range(PROJ_CHUNK // LANES):
            slab = c * (PROJ_CHUNK // LANES) + s
            z = act[:, s * LANES:(s + 1) * LANES]
            if slab < 2 * H_B:
                z = z * lax.rsqrt(jnp.sum(z * z, axis=-1, keepdims=True) + EPS)
            put((q_ref, k_ref, v_ref)[slab // H_B], slab % H_B, z)

    ba = jnp.dot(h, wba_ref[...], preferred_element_type=F32)
    lane = lax.broadcasted_iota(jnp.int32, (1, LANES), 1)
    z = ba + dtb_ref[...]
    softplus = jnp.maximum(z, 0.0) + jnp.log1p(jnp.exp(-jnp.abs(z)))
    bg = jnp.where(lane < H_B, jax.nn.sigmoid(ba), -jnp.exp(al_ref[...]) * softplus)
    bg_ref[0] = bg if valid is None else bg * valid
    tail_ref[0] = ubuf[tail_row:tail_row + SUBLANES, :]


def _gdn_proj(x3d, g, w_qkv, w_ba, conv_w, conv0_pad, al_row, dtb_row, tm, t_valid):
    b, t, d = x3d.shape
    n_tiles = -(-t_valid // tm)
    tail_row = t_valid - (n_tiles - 1) * tm
    kern = functools.partial(_gdn_proj_kernel, tm=tm, t_valid=t_valid, tail_row=tail_row)
    seq = lambda w: pl.BlockSpec((1, tm, w), lambda bi, i: (bi, i, 0))
    const = lambda r, c: pl.BlockSpec((r, c), lambda bi, i: (0, 0))
    return pl.pallas_call(
        kern,
        grid=(b, n_tiles),
        in_specs=[seq(d), const(1, d), const(d, C_CONV), const(d, LANES), const(CONV_K, C_CONV),
                  pl.BlockSpec((1, SUBLANES, C_CONV), lambda bi, i: (bi, 0, 0)),
                  const(1, LANES), const(1, LANES)],
        out_specs=[seq(H_B * DK_B), seq(H_B * DK_B), seq(W_B), seq(LANES),
                   pl.BlockSpec((1, SUBLANES, C_CONV), lambda bi, i: (bi, 0, 0))],
        out_shape=[jax.ShapeDtypeStruct((b, n_tiles * tm, H_B * DK_B), F32),
                   jax.ShapeDtypeStruct((b, n_tiles * tm, H_B * DK_B), F32),
                   jax.ShapeDtypeStruct((b, n_tiles * tm, W_B), F32),
                   jax.ShapeDtypeStruct((b, n_tiles * tm, LANES), F32),
                   jax.ShapeDtypeStruct((b, SUBLANES, C_CONV), F32)],
        scratch_shapes=[pltpu.VMEM((tm + 2 * SUBLANES, C_CONV), F32)],
        compiler_params=_cparams(("parallel", "arbitrary")),
        name="gdn_proj",
    )(x3d, g, w_qkv, w_ba, conv_w, conv0_pad, al_row, dtb_row)


def _gdn_tok_kernel(x_ref, g_ref, w_ref, wba_ref, cw_ref, c0_ref, al_ref, dtb_ref,
                    q_ref, k_ref, v_ref, bg_ref, tail_ref):
    h = _rms(x_ref[...], g_ref[...]).astype(BF16)
    cw = cw_ref[...]
    for c in range(C_CONV // PROJ_CHUNK):
        cols = slice(c * PROJ_CHUNK, (c + 1) * PROJ_CHUNK)
        u = jnp.dot(h, w_ref[:, cols], preferred_element_type=F32)
        conv = u * cw[CONV_K - 1:CONV_K, cols]
        for r in range(CONV_K - 1):
            conv = conv + c0_ref[r, :, cols] * cw[r:r + 1, cols]
            if r > 0:
                tail_ref[r - 1, :, cols] = c0_ref[r, :, cols]
        tail_ref[CONV_K - 2, :, cols] = u
        act = _silu(conv)
        for s in range(PROJ_CHUNK // LANES):
            slab = c * (PROJ_CHUNK // LANES) + s
            z = act[:, s * LANES:(s + 1) * LANES]
            if slab < 2 * H_B:
                z = z * lax.rsqrt(jnp.sum(z * z, axis=-1, keepdims=True) + EPS)
            (q_ref, k_ref, v_ref)[slab // H_B][:, (slab % H_B) * LANES:(slab % H_B + 1) * LANES] = z
    ba = jnp.dot(h, wba_ref[...], preferred_element_type=F32)
    lane = lax.broadcasted_iota(jnp.int32, (1, LANES), 1)
    z = ba + dtb_ref[...]
    softplus = jnp.maximum(z, 0.0) + jnp.log1p(jnp.exp(-jnp.abs(z)))
    bg_ref[...] = jnp.where(lane < H_B, jax.nn.sigmoid(ba), -jnp.exp(al_ref[...]) * softplus)


def _gdn_tok(x2d, g, w_qkv, w_ba, conv_w, c0, al_row, dtb_row):
    b, d = x2d.shape
    full = lambda *shape: pl.BlockSpec(shape, lambda i: (0,) * len(shape))
    return pl.pallas_call(
        _gdn_tok_kernel,
        grid=(1,),
        in_specs=[full(b, d), full(1, d), full(d, C_CONV), full(d, LANES), full(CONV_K, C_CONV),
                  full(CONV_K - 1, b, C_CONV), full(1, LANES), full(1, LANES)],
        out_specs=[full(b, H_B * DK_B), full(b, H_B * DK_B), full(b, W_B), full(b, LANES),
                   full(CONV_K - 1, b, C_CONV)],
        out_shape=[jax.ShapeDtypeStruct((b, H_B * DK_B), F32), jax.ShapeDtypeStruct((b, H_B * DK_B), F32),
                   jax.ShapeDtypeStruct((b, W_B), F32), jax.ShapeDtypeStruct((b, LANES), F32),
                   jax.ShapeDtypeStruct((CONV_K - 1, b, C_CONV), F32)],
        compiler_params=_cparams(("arbitrary",)),
        name="gdn_tok",
    )(x2d, g, w_qkv, w_ba, conv_w, c0, al_row, dtb_row)


def _score_page(q_bd, page_tile):
    kt = page_tile.reshape(q_bd.shape[1], page_tile.shape[-1]).astype(BF16)
    return jnp.dot(q_bd, kt, preferred_element_type=F32)


def _top3(gate, idx, axis, big):
    sels = []
    g = gate
    for _ in range(MOBA_TOPK):
        m = jnp.max(g, axis=axis, keepdims=True)
        ii = jnp.min(jnp.where(g == m, idx, big), axis=axis, keepdims=True)
        sels.append(ii)
        g = jnp.where(idx == ii, -jnp.inf, g)
    return sels


def _moba_prompt_kernel(pt_ref, q_ref, k_ref, vt_ref, kbar_ref, *rest, nb, n_pg):
    if n_pg:
        qb_ref, page_refs, (o_ref, sc_ref, vt_scr) = rest[0], rest[1:1 + n_pg], rest[1 + n_pg:]
    else:
        o_ref, vt_scr = rest
    i = pl.program_id(2)
    tq = MOBA_BLOCK
    n_heads = LANES // HD_A

    @pl.when(i == 0)
    def _():
        ones = jnp.ones((V_ROWS - HD_A, MOBA_BLOCK), BF16)
        for j in range(vt_scr.shape[0]):
            for hh in range(n_heads):
                if j < nb:
                    vt_scr[j, hh, 0:HD_A, :] = vt_ref[0, hh, :, j * MOBA_BLOCK:(j + 1) * MOBA_BLOCK].astype(BF16)
                    vt_scr[j, hh, HD_A:V_ROWS, :] = ones
                else:
                    vt_scr[j, hh] = jnp.zeros((V_ROWS, MOBA_BLOCK), BF16)

    def k_block(j):
        start = pl.multiple_of(jnp.minimum(j, nb - 1) * MOBA_BLOCK, MOBA_BLOCK)
        return k_ref[0, pl.ds(start, MOBA_BLOCK), :]

    q_t = q_ref[0].T
    kbar = kbar_ref[0]
    dim_head = lax.broadcasted_iota(jnp.int32, (LANES, 1), 0) // HD_A
    blk = lax.broadcasted_iota(jnp.int32, (nb, 1), 0)
    blk_f = blk.astype(F32)
    key_i = lax.broadcasted_iota(jnp.int32, (tq, tq), 0)
    qry_i = lax.broadcasted_iota(jnp.int32, (tq, tq), 1)
    heads = range(n_heads)

    def attend(j0, n_blk, keep, prev):
        raws = [[jnp.dot(kg, qs[hh], preferred_element_type=F32) for hh in heads]
                for kg in [k_block(j0 + g) for g in range(n_blk)]]
        ms = [None if prev is None else prev[hh][0] for hh in heads]
        m_at = [[None] * n_blk for _ in heads]
        pvs = [[None] * n_blk for _ in heads]
        for g in range(n_blk):
            for hh in heads:
                s = jnp.where(keep[hh][g], raws[g][hh].astype(BF16), NEG_BIG)
                m = jnp.max(s, axis=0, keepdims=True)
                if ms[hh] is not None:
                    m = jnp.maximum(ms[hh], m)
                p = jnp.exp2(s - m)
                pvs[hh][g] = jnp.dot(vt_scr[j0 + g, hh], p, preferred_element_type=F32)
                m_at[hh][g] = ms[hh] = m
        out = []
        for hh in heads:
            m_fin = ms[hh]
            rescale = lambda m_old: jnp.exp2(m_old.astype(F32) - m_fin.astype(F32))
            accl = None if prev is None else rescale(prev[hh][0]) * prev[hh][1]
            for g in range(n_blk):
                term = pvs[hh][g] if g == n_blk - 1 else rescale(m_at[hh][g]) * pvs[hh][g]
                accl = term if accl is None else accl + term
            out.append((m_fin, accl))
        return out

    qts = [jnp.where(dim_head == hh, q_t, 0.0) for hh in heads]
    qs = [(qts[hh] * (HD_A ** -0.5 * math.log2(math.e))).astype(BF16) for hh in heads]
    causal = key_i <= qry_i
    state0 = attend(i, 1, [[causal]] * n_heads, None)
    sels = []
    for hh in heads:
        gate = jnp.where(blk < i, _dot_hi(kbar, qts[hh]), -jnp.inf)
        sels.append(_top3(gate, blk_f, 0, float(nb)))
    for r in range(n_pg):
        sc_ref[0, r] = _score_page(qb_ref[0].astype(BF16), page_refs[r][0, 0])

    def body(t, carry):
        j0 = t * KV_GROUP
        keep = []
        for hh in heads:
            i1, i2, i3 = sels[hh]
            jfs = [(j0 + g).astype(F32) for g in range(KV_GROUP)]
            keep.append([(i1 == jf) | (i2 == jf) | (i3 == jf) for jf in jfs])
        state = attend(j0, KV_GROUP, keep, [carry[2 * hh:2 * hh + 2] for hh in heads])
        return tuple(x for st in state for x in st)

    carry = lax.fori_loop(0, (i + KV_GROUP - 1) // KV_GROUP, body, tuple(x for st in state0 for x in st))
    o_t = jnp.concatenate([carry[2 * hh + 1][:HD_A] / carry[2 * hh + 1][HD_A:HD_A + 1] for hh in heads],
                          axis=0)
    o_ref[0] = o_t.T


MAX_RIDE_PAGES = 16


def _moba_prompt(q3d, kb3d, vt4d, kbar, paged=None):
    b, t, w = q3d.shape
    nb = t // MOBA_BLOCK
    nb_pad = -(-nb // KV_GROUP) * KV_GROUP
    n_pairs = w // LANES
    hpp = LANES // HD_A
    n_steps = b * n_pairs * nb
    n_pg, pt_flat = 0, jnp.zeros((1,), jnp.int32)
    if paged is not None:
        q_b, cache_kt, layer, pt_flat_p, n_pages = paged
        bs = q_b.shape[0]
        _, _, h, hd, page = cache_kt.shape
        per_step = (bs * n_pages) // n_steps
        if (bs * n_pages) % n_steps == 0 and 0 < per_step <= MAX_RIDE_PAGES and n_pages % per_step == 0:
            n_pg, pt_flat = per_step, pt_flat_p
    qspec = pl.BlockSpec((1, MOBA_BLOCK, LANES), lambda bi, hp, i, pt: (bi, i, hp))
    in_specs = [qspec,
                pl.BlockSpec((1, t, LANES), lambda bi, hp, i, pt: (bi, 0, hp)),
                pl.BlockSpec((1, hpp, HD_A, t), lambda bi, hp, i, pt: (bi, hp, 0, 0)),
                pl.BlockSpec((1, nb, LANES), lambda bi, hp, i, pt: (bi, 0, hp))]
    out_specs = [qspec]
    out_shape = [jax.ShapeDtypeStruct((b, t, w), F32)]
    operands = [q3d, kb3d, vt4d, kbar]
    if n_pg:
        spr = n_pages // n_pg
        flat = lambda bi, hp, i: (bi * n_pairs + hp) * nb + i
        in_specs.append(pl.BlockSpec((1, h, h * hd), lambda bi, hp, i, pt: (flat(bi, hp, i) // spr, 0, 0)))
        in_specs += [pl.BlockSpec((1, 1, h, hd, page),
                                  lambda bi, hp, i, pt, r=r: (layer, pt[(flat(bi, hp, i) // spr) * n_pages
                                                                        + (flat(bi, hp, i) % spr) * n_pg + r],
                                                              0, 0, 0))
                     for r in range(n_pg)]
        out_specs.append(pl.BlockSpec((1, n_pg, h, page),
                                      lambda bi, hp, i, pt: (flat(bi, hp, i) // spr, flat(bi, hp, i) % spr, 0, 0)))
        out_shape.append(jax.ShapeDtypeStruct((bs, n_pages, h, page), F32))
        operands += [q_b] + [cache_kt] * n_pg
    res = pl.pallas_call(
        functools.partial(_moba_prompt_kernel, nb=nb, n_pg=n_pg),
        grid_spec=pltpu.PrefetchScalarGridSpec(
            num_scalar_prefetch=1,
            grid=(b, n_pairs, nb),
            in_specs=in_specs,
            out_specs=out_specs,
            scratch_shapes=[pltpu.VMEM((nb_pad, hpp, V_ROWS, MOBA_BLOCK), BF16)],
        ),
        out_shape=out_shape,
        compiler_params=_cparams(("arbitrary", "arbitrary", "arbitrary")),
        name="moba_prompt",
    )(pt_flat, *operands)
    return res[0], (res[1] if n_pg else None)


def _gdn_seq_kernel(q_ref, k_ref, v_ref, bg_ref, s0_ref, gn_ref, o_ref, s_out_ref, s_scr, *, rows):
    step = pl.program_id(1)
    n_steps = pl.num_programs(1)
    cc = GDN_CHUNK
    ncs = rows // cc

    @pl.when(step == 0)
    def _():
        s_scr[...] = s0_ref[0]

    bg = bg_ref[0]
    gn = gn_ref[...]
    ri = lax.broadcasted_iota(jnp.int32, (rows, rows), 0)
    ci = lax.broadcasted_iota(jnp.int32, (rows, rows), 1)
    same = (ri // cc) == (ci // cc)
    sum_mats = jnp.concatenate([(same & (ci <= ri)).astype(BF16), same.astype(BF16)], axis=0)
    b_hi = bg.astype(BF16)
    r_1 = bg - b_hi.astype(F32)
    b_mid = r_1.astype(BF16)
    b_lo = (r_1 - b_mid.astype(F32)).astype(BF16)
    sums = sum(jnp.dot(sum_mats, piece, preferred_element_type=F32) for piece in (b_hi, b_mid, b_lo))
    gc = sums[:rows]
    gl = sums[rows:]
    gc_t = gc.T
    e_gc = jnp.exp(gc)
    e_tail = jnp.exp(gl - gc)
    e_tot = jnp.exp(gl)
    pr = lax.broadcasted_iota(jnp.int32, (cc, rows), 0)
    pl_i = lax.broadcasted_iota(jnp.int32, (cc, rows), 1)
    pc = pl_i % cc
    lane_blk = pl_i // cc
    causal_p = pr >= pc
    strict_p = pr > pc

    def pack(full):
        out = full[(ncs - 1) * cc:ncs * cc]
        for c in range(ncs - 2, -1, -1):
            out = jnp.where(lane_blk == c, full[c * cc:(c + 1) * cc], out)
        return out

    def pack_col(col):
        out = col[(ncs - 1) * cc:ncs * cc]
        for c in range(ncs - 2, -1, -1):
            out = jnp.where(lane_blk == c, col[c * cc:(c + 1) * cc], out)
        return out

    def bdiag(p):
        return jnp.where(same, jnp.concatenate([p] * ncs, axis=0), 0.0)

    heads = range(H_B)
    col = lambda a, h: a[:, H_B + h:H_B + h + 1]
    qs = [q_ref[0, :, h * DK_B:(h + 1) * DK_B] * (DK_B ** -0.5) for h in heads]
    ks = [k_ref[0, :, h * DK_B:(h + 1) * DK_B] for h in heads]
    kbetas = [ks[h] * bg[:, h:h + 1] for h in heads]
    fulls = [_dot_nt(jnp.concatenate([kbetas[h], qs[h]], axis=0), ks[h]) for h in heads]
    decays = [jnp.exp(jnp.where(causal_p, pack_col(col(gc, h)) - gc_t[H_B + h:H_B + h + 1, :], -jnp.inf))
              for h in heads]
    attns = [jnp.where(causal_p, pack(fulls[h][rows:]) * decays[h], 0.0) for h in heads]
    es = [-jnp.where(strict_p, pack(fulls[h][:rows]) * decays[h], 0.0) for h in heads]
    pws = [_dot(es[h], bdiag(es[h])) for h in heads]
    for _ in range(int(math.log2(cc)) - 2):
        rs_ = [_dot(jnp.concatenate([es[h], pws[h]], axis=0), bdiag(pws[h])) for h in heads]
        es = [es[h] + pws[h] + rs_[h][:cc] for h in heads]
        pws = [rs_[h][cc:] for h in heads]
    es = [es[h] + pws[h] + _dot(es[h], bdiag(pws[h])) for h in heads]
    rhss = [jnp.concatenate([v_ref[0, :, h * DV_B:(h + 1) * DV_B] * bg[:, h:h + 1],
                             kbetas[h] * col(e_gc, h)], axis=1) for h in heads]
    sols = [rhss[h] + _dot(bdiag(es[h]), rhss[h]) for h in heads]
    q_decs = [qs[h] * col(e_gc, h) for h in heads]
    k_tails = [ks[h] * col(e_tail, h) for h in heads]
    ss = [s_scr[h] for h in heads]
    for c in range(ncs):
        rs = slice(c * cc, (c + 1) * cc)
        wqs = [_dot(jnp.concatenate([sols[h][rs, DV_B:], q_decs[h][rs]], axis=0), ss[h]) for h in heads]
        v_news = [sols[h][rs, :DV_B] - wqs[h][:cc] for h in heads]
        os_ = [wqs[h][cc:] + _dot(attns[h][:, rs], v_news[h]) for h in heads]
        kvs = [_dot_tn(k_tails[h][rs], v_news[h]) for h in heads]
        ss = [ss[h] * e_tot[c * cc:c * cc + 1, H_B + h:H_B + h + 1] + kvs[h] for h in heads]
        for h in heads:
            o_ref[0, rs, h * DV_B:(h + 1) * DV_B] = _rms(os_[h], gn)
    for h in heads:
        s_scr[h] = ss[h]


    @pl.when(step == n_steps - 1)
    def _():
        s_out_ref[0] = s_scr[...]


def _gdn_seq(q, k, v, bg, s0, g_onorm_row):
    b, t, _ = q.shape
    rows = MXU_DIM
    while t % rows:
        rows //= 2
    seq = lambda w: pl.BlockSpec((1, rows, w), lambda bi, c: (bi, c, 0))
    sspec = pl.BlockSpec((1, H_B, DK_B, DV_B), lambda bi, c: (bi, 0, 0, 0))
    return pl.pallas_call(
        functools.partial(_gdn_seq_kernel, rows=rows),
        grid=(b, t // rows),
        in_specs=[seq(H_B * DK_B), seq(H_B * DK_B), seq(W_B), seq(LANES), sspec,
                  pl.BlockSpec((1, DV_B), lambda bi, c: (0, 0))],
        out_specs=[seq(W_B), sspec],
        out_shape=[jax.ShapeDtypeStruct((b, t, W_B), F32),
                   jax.ShapeDtypeStruct((b, H_B, DK_B, DV_B), F32)],
        scratch_shapes=[pltpu.VMEM((H_B, DK_B, DV_B), F32)],
        compiler_params=_cparams(("parallel", "arbitrary")),
        name="gdn_seq",
    )(q, k, v, bg, s0, g_onorm_row)


def _gdn_step_kernel(kq_ref, v_ref, bg_ref, s0_ref, gn_ref, o_ref, s_out_ref):
    bg = bg_ref[0]
    gn = gn_ref[...]
    kq = kq_ref[0]
    for h in range(H_B):
        beta = bg[:, h:h + 1]
        eg = jnp.exp(bg[:, H_B + h:H_B + h + 1])
        k_col = kq[:, h:h + 1]
        q_col = kq[:, H_B + h:H_B + h + 1] * (DK_B ** -0.5)
        v_row = v_ref[0, :, h * DV_B:(h + 1) * DV_B]
        s0 = s0_ref[0, h]
        k_s = jnp.sum(k_col * s0, axis=0, keepdims=True)
        q_s = jnp.sum(q_col * s0, axis=0, keepdims=True)
        v_new = beta * (v_row - eg * k_s)
        qk = jnp.sum(q_col * k_col, axis=0, keepdims=True)
        o = eg * q_s + qk * v_new
        s_out_ref[0, h] = s0 * eg + k_col * v_new
        o_ref[0, :, h * DV_B:(h + 1) * DV_B] = _rms(o, gn)


def _gdn_step(kq_cols, v, bg, s0, g_onorm_row):
    b = v.shape[0]
    tok = lambda w: pl.BlockSpec((1, 1, w), lambda bi: (bi, 0, 0))
    sspec = pl.BlockSpec((1, H_B, DK_B, DV_B), lambda bi: (bi, 0, 0, 0))
    return pl.pallas_call(
        _gdn_step_kernel,
        grid=(b,),
        in_specs=[pl.BlockSpec((1, DK_B, 2 * H_B), lambda bi: (bi, 0, 0)), tok(W_B), tok(LANES), sspec,
                  pl.BlockSpec((1, DV_B), lambda bi: (0, 0))],
        out_specs=[tok(W_B), sspec],
        out_shape=[jax.ShapeDtypeStruct((b, 1, W_B), F32),
                   jax.ShapeDtypeStruct((b, H_B, DK_B, DV_B), F32)],
        compiler_params=_cparams(("parallel",)),
        name="gdn_step",
    )(kq_cols, v, bg, s0, g_onorm_row)


def _out_stage_kernel(x_ref, oa_ref, ob_ref, p_ref, gmix_ref, wzg_ref, wpa_ref, wpb_ref, wo_ref,
                      gple_ref, wpg_ref, wple_ref, gfin_ref, y_ref, *, final):
    x = x_ref[...]
    h = _rms(x, gmix_ref[...]).astype(BF16)
    zg = jnp.dot(h, wzg_ref[...], preferred_element_type=F32)
    za = zg[:, :W_A]
    zb = zg[:, W_A:W_A + W_B]
    d = x.shape[1]
    ga = zg[:, W_A + W_B:W_A + W_B + d]
    gb = zg[:, W_A + W_B + d:]
    ya = _dot(oa_ref[...] * _silu(za), wpa_ref[...])
    yb = _dot(ob_ref[...] * _silu(zb), wpb_ref[...])
    mixed = jax.nn.sigmoid(ga) * ya + jax.nn.sigmoid(gb) * yb
    x = x + _dot(mixed, wo_ref[...])
    gate = jax.nn.sigmoid(_dot(_rms(x, gple_ref[...]), wpg_ref[...]))
    x = x + gate * _dot(p_ref[...], wple_ref[...])
    y_ref[...] = _rms(x, gfin_ref[...]) if final else x


def _out_stage(x2d, oa, ob, p2d, gmix, wzg, wpa, wpb, wo, gple, wpg, wple, gfin, tm, final):
    n, d = x2d.shape
    row = lambda w: pl.BlockSpec((tm, w), lambda i: (i, 0))
    const = lambda a: pl.BlockSpec(a.shape, lambda i: (0, 0))
    return pl.pallas_call(
        functools.partial(_out_stage_kernel, final=final),
        grid=(n // tm,),
        in_specs=[row(d), row(W_A), row(W_B), row(p2d.shape[1]), const(gmix), const(wzg), const(wpa),
                  const(wpb), const(wo), const(gple), const(wpg), const(wple), const(gfin)],
        out_specs=row(d),
        out_shape=jax.ShapeDtypeStruct((n, d), F32),
        compiler_params=_cparams(("parallel",)),
        name="out_stage",
    )(x2d, oa, ob, p2d, gmix, wzg, wpa, wpb, wo, gple, wpg, wple, gfin)


PAGES_PER_STEP = 32


def _page_scores_kernel(pt_ref, q_ref, *refs):
    k_refs, o_ref = refs[:-1], refs[-1]
    q_bd = q_ref[0].astype(BF16)
    for r, k_ref in enumerate(k_refs):
        o_ref[0, r] = _score_page(q_bd, k_ref[0, 0])


def _page_scores(q_bd, cache_kt, layer, pt_flat, n_pages):
    b = q_bd.shape[0]
    _, _, h, hd, page = cache_kt.shape
    nps = PAGES_PER_STEP if n_pages % PAGES_PER_STEP == 0 else 1
    in_specs = [pl.BlockSpec((1, h, h * hd), lambda bi, g, pt: (bi, 0, 0))]
    in_specs += [pl.BlockSpec((1, 1, h, hd, page),
                              lambda bi, g, pt, r=r: (layer, pt[bi * n_pages + g * nps + r], 0, 0, 0))
                 for r in range(nps)]
    return pl.pallas_call(
        _page_scores_kernel,
        grid_spec=pltpu.PrefetchScalarGridSpec(
            num_scalar_prefetch=1,
            grid=(b, n_pages // nps),
            in_specs=in_specs,
            out_specs=pl.BlockSpec((1, nps, h, page), lambda bi, g, pt: (bi, g, 0, 0)),
        ),
        out_shape=jax.ShapeDtypeStruct((b, n_pages, h, page), F32),
        compiler_params=_cparams(("parallel", "parallel")),
        name="page_scores",
    )(pt_flat, q_bd, *([cache_kt] * nps))


def _sample_select_kernel(p_ref, o_ref, *, nblk, ppb):
    tot = jnp.sum(p_ref[0], axis=2, keepdims=True)
    gate = jnp.sum(tot.reshape(nblk, ppb, H_A, 1), axis=1) * (1.0 / MOBA_BLOCK)
    blk_f = lax.broadcasted_iota(jnp.int32, (nblk, 1, 1), 0).astype(F32)
    for r, sel in enumerate(_top3(gate, blk_f, 0, float(nblk))):
        o_ref[0, r] = jnp.broadcast_to(jnp.minimum(sel[0], nblk - 1.0), (H_A, LANES)).astype(jnp.int32)


def _sample_select(scores, ppb):
    b, n_pages, h, page = scores.shape
    return pl.pallas_call(
        functools.partial(_sample_select_kernel, nblk=n_pages // ppb, ppb=ppb),
        grid=(b,),
        in_specs=[pl.BlockSpec((1, n_pages, h, page), lambda bi: (bi, 0, 0, 0))],
        out_specs=pl.BlockSpec((1, MOBA_TOPK, h, LANES), lambda bi: (bi, 0, 0, 0)),
        out_shape=jax.ShapeDtypeStruct((b, MOBA_TOPK, h, LANES), jnp.int32),
        compiler_params=_cparams(("parallel",)),
        name="sample_select",
    )(scores)


PAGED_HEADS_PER_STEP = 4


def _paged_attn_kernel(pt_ref, top_ref, q_ref, kn_ref, vn_ref, s_ref, *refs, n_s, ppb):
    v_refs, o_ref = refs[:-1], refs[-1]
    bi = pl.program_id(0)
    hg = pl.program_id(1)
    scale = HD_A ** -0.5
    lane = lax.broadcasted_iota(jnp.int32, (1, H_A), 1)
    row = lax.broadcasted_iota(jnp.int32, (H_A, 1), 0)
    out = jnp.zeros((HD_A, H_A), F32)
    for j in range(PAGED_HEADS_PER_STEP):
        h = hg * PAGED_HEADS_PER_STEP + j
        pick = lambda a, h=h: jnp.sum(jnp.where(row == h, a, 0.0), axis=0, keepdims=True)
        own = jnp.sum(pick(q_ref[0] * kn_ref[0]), axis=1, keepdims=True) * scale
        rows = []
        for s in range(n_s):
            page = top_ref[(bi * MOBA_TOPK + s // ppb) * H_A + h] * ppb + s % ppb
            rows.append(pick(s_ref[0, page]) * scale)
        m = own
        for r in rows:
            m = jnp.maximum(m, jnp.max(r, axis=1, keepdims=True))
        p_own = jnp.exp(own - m)
        ps = [jnp.exp(r - m) for r in rows]
        l = p_own + sum(jnp.sum(p, axis=1, keepdims=True) for p in ps)
        acc = sum(p * v_refs[j * n_s + s][0, 0, 0] for s, p in enumerate(ps))
        vn_col = jnp.sum(jnp.where(lane == h, vn_ref[0], 0.0), axis=1, keepdims=True)
        res = (jnp.sum(acc, axis=1, keepdims=True) + p_own * vn_col) / l
        out = jnp.where(lane == h, res, out)

    @pl.when(hg == 0)
    def _():
        o_ref[0] = out

    @pl.when(hg > 0)
    def _():
        o_ref[0] = o_ref[0] + out


def _paged_attn(q3, kn3, vn_t, scores, cache_vt, layer, pt_flat, top_flat, n_pages):
    b = q3.shape[0]
    _, _, h, hd, page = cache_vt.shape
    ppb = MOBA_BLOCK // page
    n_s = MOBA_TOPK * ppb

    hps = PAGED_HEADS_PER_STEP

    def v_spec(j, s):
        def index(bi, hg, pt, top):
            hi = hg * hps + j
            logical = top[(bi * MOBA_TOPK + s // ppb) * H_A + hi] * ppb + s % ppb
            return (layer, pt[bi * n_pages + logical], hi, 0, 0)
        return pl.BlockSpec((1, 1, 1, hd, page), index)

    tok = pl.BlockSpec((1, h, hd), lambda bi, hg, pt, top: (bi, 0, 0))
    tok_t = pl.BlockSpec((1, hd, h), lambda bi, hg, pt, top: (bi, 0, 0))
    return pl.pallas_call(
        functools.partial(_paged_attn_kernel, n_s=n_s, ppb=ppb),
        grid_spec=pltpu.PrefetchScalarGridSpec(
            num_scalar_prefetch=2,
            grid=(b, h // hps),
            in_specs=[tok, tok, tok_t,
                      pl.BlockSpec((1, n_pages, h, page), lambda bi, hg, pt, top: (bi, 0, 0, 0))]
            + [v_spec(j, s) for j in range(hps) for s in range(n_s)],
            out_specs=tok_t,
        ),
        out_shape=jax.ShapeDtypeStruct((b, hd, h), F32),
        compiler_params=_cparams(("parallel", "arbitrary")),
        name="paged_attn",
    )(pt_flat, top_flat, q3, kn3, vn_t, scores, *([cache_vt] * (hps * n_s)))


def _split_weights(w_in_l, d):
    o = 0
    parts = {}
    for name, n in (("qa", W_A), ("ka", W_A), ("va", W_A), ("za", W_A), ("qb", H_B * DK_B),
                    ("kb", H_B * DK_B), ("vb", W_B), ("zb", W_B), ("beta", H_B), ("alpha", H_B),
                    ("ga", d), ("gb", d)):
        parts[name] = w_in_l[:, o:o + n]
        o += n
    w_attn = jnp.concatenate([parts["qa"], parts["ka"], parts["va"]], axis=1).astype(BF16)
    w_qkv = jnp.concatenate([parts["qb"], parts["kb"], parts["vb"]], axis=1).astype(BF16)
    w_ba = jnp.concatenate([parts["beta"], parts["alpha"],
                            jnp.zeros((d, LANES - 2 * H_B), F32)], axis=1).astype(BF16)
    w_zg = jnp.concatenate([parts["za"], parts["zb"], parts["ga"], parts["gb"]], axis=1).astype(BF16)
    return w_attn, w_qkv, w_ba, w_zg


def _lane_row(vec, offset):
    out = jnp.zeros((1, LANES), F32)
    return out.at[0, offset:offset + vec.shape[0]].set(vec.astype(F32))


def _pick_tile(n, pref):
    t = pref
    while n % t:
        t //= 2
    return t


def kernel(x_prompt, x_sample, p_prompt, p_sample, cache_k, cache_v, page_table, state_gdn_s, state_gdn_conv, g_mix, w_in, conv_w, a_log, dt_bias, g_onorm, w_pa, w_pb, w_o, g_ple, w_ple_gate, w_ple, g_final):
    bp, tp, d = x_prompt.shape
    bs, ts, _ = x_sample.shape
    depth = w_in.shape[0]
    n_pages = page_table.shape[1]
    page = cache_k.shape[2]
    assert ts == 1 and tp % MOBA_BLOCK == 0 and (n_pages * page) % MOBA_BLOCK == 0
    assert tp % GDN_CHUNK == 0

    pos_p = jnp.arange(tp, dtype=jnp.int32)
    pos_s = jnp.full((bs,), n_pages * page, dtype=jnp.int32)
    tabs_p = _rope_tables(pos_p)
    tabs_s = _rope_tables(pos_s)
    pt_flat = page_table.reshape(-1).astype(jnp.int32)
    cache_kt = jnp.transpose(cache_k, (0, 1, 3, 4, 2))
    cache_vt = jnp.transpose(cache_v, (0, 1, 3, 4, 2))

    xp = x_prompt.reshape(bp * tp, d)
    xs = x_sample.reshape(bs, d)
    outs = {k: [] for k in ("kp", "vp", "sp", "cp", "ks", "vs", "ss", "cs")}
    for l in range(depth):
        final = l == depth - 1
        w_attn, w_qkv, w_ba, w_zg = _split_weights(w_in[l], d)
        gmix = g_mix[l].reshape(1, d)
        al_row = _lane_row(a_log[l], H_B)
        dtb_row = _lane_row(dt_bias[l], H_B)
        gon = g_onorm[l].reshape(1, DV_B)
        wpa, wpb, wo = w_pa[l].astype(BF16), w_pb[l].astype(BF16), w_o[l].astype(BF16)
        wpg, wple = w_ple_gate[l].astype(BF16), w_ple[l].astype(BF16)
        gple = g_ple[l].reshape(1, d)
        gfin = g_final.reshape(1, d)

        qa_s, ka_s, va_s = _attn_proj(xs, gmix, w_attn, tabs_s, bs)
        q3 = qa_s.reshape(bs, H_A, HD_A)
        q_b = (jnp.eye(H_A, dtype=F32)[None, :, :, None] * q3[:, None, :, :]).reshape(bs, H_A, W_A)

        qa, kb, kt, vt, kbar = _attn_proj_seq(xp, gmix, w_attn, tabs_p, _pick_tile(tp, 512), bp)
        conv0 = jnp.zeros((bp, SUBLANES, C_CONV), F32)
        qn, kn, vn, bg, tail = _gdn_proj(xp.reshape(bp, tp, d), gmix, w_qkv, w_ba, conv_w[l], conv0,
                                         al_row, dtb_row, _pick_tile(tp, 256), tp)
        oa, scores = _moba_prompt(qa.reshape(bp, tp, W_A), kb.reshape(bp, tp, W_A), vt,
                                  kbar.reshape(bp, tp // MOBA_BLOCK, W_A), (q_b, cache_kt, l, pt_flat, n_pages))
        s0 = jnp.zeros((bp, H_B, DK_B, DV_B), F32)
        ob, s_fin = _gdn_seq(qn, kn, vn, bg, s0, gon)
        xp = _out_stage(xp, oa.reshape(bp * tp, W_A), ob.reshape(bp * tp, W_B),
                        p_prompt[l].reshape(bp * tp, -1), gmix, w_zg, wpa, wpb, wo, gple, wpg, wple,
                        gfin, _pick_tile(bp * tp, 256), final)
        outs["kp"].append(jnp.transpose(kt, (0, 3, 1, 2)))
        outs["vp"].append(jnp.transpose(vt, (0, 3, 1, 2)))
        outs["sp"].append(s_fin.astype(state_gdn_s.dtype))
        outs["cp"].append(tail[:, SUBLANES - (CONV_K - 1):, :])

        conv0_s = jnp.transpose(state_gdn_conv[l].astype(F32), (1, 0, 2))
        qn_s, kn_s, vn_s, bg_s, tail_s = _gdn_tok(xs, gmix, w_qkv, w_ba, conv_w[l], conv0_s, al_row, dtb_row)
        if scores is None:
            scores = _page_scores(q_b, cache_kt, l, pt_flat, n_pages)
        top = _sample_select(scores, MOBA_BLOCK // page)
        top_flat = top[:, :, :, 0].reshape(-1)
        vn_t = jnp.transpose(va_s.reshape(bs, H_A, HD_A), (0, 2, 1))
        oa_t = _paged_attn(q3, ka_s.reshape(bs, H_A, HD_A), vn_t, scores, cache_vt, l, pt_flat, top_flat,
                           n_pages)
        oa_s = jnp.transpose(oa_t, (0, 2, 1))
        cols = lambda a: jnp.transpose(a.reshape(bs, H_B, DK_B), (0, 2, 1))
        kq_cols = jnp.concatenate([cols(kn_s), cols(qn_s)], axis=2)
        ob_s, s_fin_s = _gdn_step(kq_cols, vn_s.reshape(bs, 1, W_B), bg_s.reshape(bs, 1, LANES),
                                  state_gdn_s[l].astype(F32), gon)
        xs = _out_stage(xs, oa_s.reshape(bs, W_A), ob_s[:, 0, :], p_sample[l].reshape(bs, -1), gmix, w_zg,
                        wpa, wpb, wo, gple, wpg, wple, gfin, bs, final)
        outs["ks"].append(ka_s.reshape(bs, 1, H_A, HD_A))
        outs["vs"].append(va_s.reshape(bs, 1, H_A, HD_A))
        outs["ss"].append(s_fin_s.astype(state_gdn_s.dtype))
        outs["cs"].append(jnp.transpose(tail_s, (1, 0, 2)))

    y_prompt = xp.reshape(bp, tp, d)
    y_sample = xs.reshape(bs, ts, d)
    st = lambda k: jnp.stack(outs[k])
    return (y_prompt, y_sample, st("kp"), st("vp"), st("sp"), st("cp"),
            st("ks"), st("vs"), st("ss"), st("cs"))
```

```python
import functools
import math

import jax
import jax.numpy as jnp
from jax import lax
from jax.experimental import pallas as pl
from jax.experimental.pallas import tpu as pltpu

F32 = jnp.float32
BF16 = jnp.bfloat16
HIGHEST = lax.Precision.HIGHEST

H_A = 8
HD_A = 64
W_A = H_A * HD_A
ROT_DIM = HD_A // 4
ROPE_THETA = 500000.0
MOBA_BLOCK = 256
MOBA_TOPK = 3
H_B = 8
DK_B = 128
DV_B = 128
W_B = H_B * DV_B
CONV_K = 4
C_CONV = H_B * (2 * DK_B + DV_B)
GDN_CHUNK = 64
EPS = 1e-6

LANES = 128
SUBLANES = 8
MXU_DIM = 256
VMEM_LIMIT_BYTES = 56 * 1024 * 1024

NEG_BIG = -1e30
KV_GROUP = 4
PROJ_CHUNK = 512
V_ROWS = HD_A + 16


def _cparams(sem):
    return pltpu.CompilerParams(dimension_semantics=sem, vmem_limit_bytes=VMEM_LIMIT_BYTES)


def _rms(x, g):
    return x * lax.rsqrt(jnp.mean(x * x, axis=-1, keepdims=True) + EPS) * g


def _silu(x):
    return x * jax.nn.sigmoid(x)


def _dot(a, b):
    return jnp.dot(a.astype(BF16), b.astype(BF16), preferred_element_type=F32)


def _dot_nt(a, b):
    return lax.dot_general(a.astype(BF16), b.astype(BF16), (((1,), (1,)), ((), ())),
                           preferred_element_type=F32)


def _dot_tn(a, b):
    return lax.dot_general(a.astype(BF16), b.astype(BF16), (((0,), (0,)), ((), ())),
                           preferred_element_type=F32)


def _dot_hi(a, b):
    return jnp.dot(a, b, precision=HIGHEST, preferred_element_type=F32)


def _attn_qkv(x_ref, g_ref, w_ref, c_ref, sa_ref, sb_ref):
    h = _rms(x_ref[...], g_ref[...]).astype(BF16)
    y = jnp.dot(h, w_ref[...], preferred_element_type=F32)
    c, sa, sb = c_ref[...], sa_ref[...], sb_ref[...]
    half = ROT_DIM // 2
    slabs = lambda base: [y[:, base + s * LANES: base + (s + 1) * LANES] for s in range(W_A // LANES)]
    rope = lambda z: z * c + pltpu.roll(z, LANES - half, 1) * sa + pltpu.roll(z, half, 1) * sb
    return [rope(z) for z in slabs(0)], [rope(z) for z in slabs(W_A)], slabs(2 * W_A)


def _attn_proj_kernel(x_ref, g_ref, w_ref, c_ref, sa_ref, sb_ref, q_ref, k_ref, v_ref):
    qs, ks, vs = _attn_qkv(x_ref, g_ref, w_ref, c_ref, sa_ref, sb_ref)
    for s in range(W_A // LANES):
        q_ref[:, s * LANES:(s + 1) * LANES] = qs[s]
        k_ref[:, s * LANES:(s + 1) * LANES] = ks[s]
        v_ref[:, s * LANES:(s + 1) * LANES] = vs[s]


def _attn_proj_seq_kernel(x_ref, g_ref, w_ref, c_ref, sa_ref, sb_ref, q_ref, kb_ref, kt_ref, vt_ref, kbar_ref):
    qs, ks, vs = _attn_qkv(x_ref, g_ref, w_ref, c_ref, sa_ref, sb_ref)
    tm = x_ref.shape[0]
    hps = LANES // HD_A
    for s in range(W_A // LANES):
        q_ref[:, s * LANES:(s + 1) * LANES] = qs[s]
        kb_ref[:, s * LANES:(s + 1) * LANES] = ks[s].astype(BF16)
        kt_ref[0, s * hps:(s + 1) * hps] = ks[s].T.reshape(hps, HD_A, tm)
        vt_ref[0, s * hps:(s + 1) * hps] = vs[s].T.reshape(hps, HD_A, tm)
        kbar_ref[0, :, s * LANES:(s + 1) * LANES] = jnp.sum(
            ks[s].reshape(tm // MOBA_BLOCK, MOBA_BLOCK, LANES), axis=1) * (1.0 / MOBA_BLOCK)


def _rope_tables(pos):
    half = ROT_DIM // 2
    inv = jnp.power(ROPE_THETA, -jnp.arange(half, dtype=F32) / half)
    ang = pos.astype(F32)[:, None] * inv[None, :]
    cos, sin = jnp.cos(ang), jnp.sin(ang)
    t = pos.shape[0]
    ones = jnp.ones((t, HD_A - ROT_DIM), F32)
    zeros = jnp.zeros((t, HD_A - ROT_DIM), F32)
    zh = jnp.zeros((t, half), F32)
    c = jnp.concatenate([cos, cos, ones], axis=1)
    sa = jnp.concatenate([-sin, zh, zeros], axis=1)
    sb = jnp.concatenate([zh, sin, zeros], axis=1)
    rep = LANES // HD_A
    return tuple(jnp.tile(a, (1, rep)) for a in (c, sa, sb))


def _attn_proj(x2d, g, w, tabs, tm):
    n, d = x2d.shape
    t_tab = tabs[0].shape[0]
    period = t_tab // tm
    tab_spec = pl.BlockSpec((tm, LANES), lambda i: (i % period, 0))
    out_spec = pl.BlockSpec((tm, W_A), lambda i: (i, 0))
    out = jax.ShapeDtypeStruct((n, W_A), F32)
    return pl.pallas_call(
        _attn_proj_kernel,
        grid=(n // tm,),
        in_specs=[pl.BlockSpec((tm, d), lambda i: (i, 0)),
                  pl.BlockSpec((1, d), lambda i: (0, 0)),
                  pl.BlockSpec((d, 3 * W_A), lambda i: (0, 0)),
                  tab_spec, tab_spec, tab_spec],
        out_specs=[out_spec, out_spec, out_spec],
        out_shape=[out, out, out],
        compiler_params=_cparams(("parallel",)),
        name="attn_proj",
    )(x2d, g, w, *tabs)


def _attn_proj_seq(x2d, g, w, tabs, tm, batch):
    n, d = x2d.shape
    t = n // batch
    period = t // tm
    bpt = tm // MOBA_BLOCK
    tab_spec = pl.BlockSpec((tm, LANES), lambda i: (i % period, 0))
    row_spec = pl.BlockSpec((tm, W_A), lambda i: (i, 0))
    t_spec = pl.BlockSpec((1, H_A, HD_A, tm), lambda i: (i // period, 0, 0, i % period))
    t_shape = jax.ShapeDtypeStruct((batch, H_A, HD_A, t), F32)
    return pl.pallas_call(
        _attn_proj_seq_kernel,
        grid=(n // tm,),
        in_specs=[pl.BlockSpec((tm, d), lambda i: (i, 0)),
                  pl.BlockSpec((1, d), lambda i: (0, 0)),
                  pl.BlockSpec((d, 3 * W_A), lambda i: (0, 0)),
                  tab_spec, tab_spec, tab_spec],
        out_specs=[row_spec, row_spec, t_spec, t_spec, pl.BlockSpec((1, bpt, W_A), lambda i: (i, 0, 0))],
        out_shape=[jax.ShapeDtypeStruct((n, W_A), F32), jax.ShapeDtypeStruct((n, W_A), BF16), t_shape, t_shape,
                   jax.ShapeDtypeStruct((n // tm, bpt, W_A), F32)],
        compiler_params=_cparams(("parallel",)),
        name="attn_proj_seq",
    )(x2d, g, w, *tabs)


def _gdn_proj_kernel(x_ref, g_ref, w_ref, wba_ref, cw_ref, c0_ref, al_ref, dtb_ref,
                     q_ref, k_ref, v_ref, bg_ref, tail_ref, ubuf, *, tm, t_valid, tail_row):
    i = pl.program_id(1)

    @pl.when(i == 0)
    def _():
        ubuf[0:SUBLANES, :] = c0_ref[0]

    @pl.when(i > 0)
    def _():
        ubuf[0:SUBLANES, :] = ubuf[tm:tm + SUBLANES, :]

    h = _rms(x_ref[0], g_ref[...]).astype(BF16)
    if t_valid % tm != 0:
        row = i * tm + lax.broadcasted_iota(jnp.int32, (tm, 1), 0)
        valid = (row < t_valid).astype(F32)
    else:
        valid = None

    def put(ref, s, val):
        ref[0, :, s * LANES:(s + 1) * LANES] = val if valid is None else val * valid

    cw = cw_ref[...]
    n_chunks = C_CONV // PROJ_CHUNK
    dot_chunk = lambda c: jnp.dot(h, w_ref[:, c * PROJ_CHUNK:(c + 1) * PROJ_CHUNK], preferred_element_type=F32)
    nxt = dot_chunk(0)
    for c in range(n_chunks):
        u = nxt
        if c + 1 < n_chunks:
            nxt = dot_chunk(c + 1)
        cols = slice(c * PROJ_CHUNK, (c + 1) * PROJ_CHUNK)
        ubuf[SUBLANES:SUBLANES + tm, cols] = u
        conv = u * cw[CONV_K - 1:CONV_K, cols]
        for j in range(1, CONV_K):
            conv = conv + ubuf[SUBLANES - j:SUBLANES - j + tm, cols] * cw[CONV_K - 1 - j:CONV_K - j, cols]
        act = _silu(conv)
        for s in range(PROJ_CHUNK // LANES):
            slab = c * (PROJ_CHUNK // LANES) + s
            z = act[:, s * LANES:(s + 1) * LANES]
            if slab < 2 * H_B:
                z = z * lax.rsqrt(jnp.sum(z * z, axis=-1, keepdims=True) + EPS)
            put((q_ref, k_ref, v_ref)[slab // H_B], slab % H_B, z)

    ba = jnp.dot(h, wba_ref[...], preferred_element_type=F32)
    lane = lax.broadcasted_iota(jnp.int32, (1, LANES), 1)
    z = ba + dtb_ref[...]
    softplus = jnp.maximum(z, 0.0) + jnp.log1p(jnp.exp(-jnp.abs(z)))
    bg = jnp.where(lane < H_B, jax.nn.sigmoid(ba), -jnp.exp(al_ref[...]) * softplus)
    bg_ref[0] = bg if valid is None else bg * valid
    tail_ref[0] = ubuf[tail_row:tail_row + SUBLANES, :]


def _gdn_proj(x3d, g, w_qkv, w_ba, conv_w, conv0_pad, al_row, dtb_row, tm, t_valid):
    b, t, d = x3d.shape
    n_tiles = -(-t_valid // tm)
    tail_row = t_valid - (n_tiles - 1) * tm
    kern = functools.partial(_gdn_proj_kernel, tm=tm, t_valid=t_valid, tail_row=tail_row)
    seq = lambda w: pl.BlockSpec((1, tm, w), lambda bi, i: (bi, i, 0))
    const = lambda r, c: pl.BlockSpec((r, c), lambda bi, i: (0, 0))
    return pl.pallas_call(
        kern,
        grid=(b, n_tiles),
        in_specs=[seq(d), const(1, d), const(d, C_CONV), const(d, LANES), const(CONV_K, C_CONV),
                  pl.BlockSpec((1, SUBLANES, C_CONV), lambda bi, i: (bi, 0, 0)),
                  const(1, LANES), const(1, LANES)],
        out_specs=[seq(H_B * DK_B), seq(H_B * DK_B), seq(W_B), seq(LANES),
                   pl.BlockSpec((1, SUBLANES, C_CONV), lambda bi, i: (bi, 0, 0))],
        out_shape=[jax.ShapeDtypeStruct((b, n_tiles * tm, H_B * DK_B), F32),
                   jax.ShapeDtypeStruct((b, n_tiles * tm, H_B * DK_B), F32),
                   jax.ShapeDtypeStruct((b, n_tiles * tm, W_B), F32),
                   jax.ShapeDtypeStruct((b, n_tiles * tm, LANES), F32),
                   jax.ShapeDtypeStruct((b, SUBLANES, C_CONV), F32)],
        scratch_shapes=[pltpu.VMEM((tm + 2 * SUBLANES, C_CONV), F32)],
        compiler_params=_cparams(("parallel", "arbitrary")),
        name="gdn_proj",
    )(x3d, g, w_qkv, w_ba, conv_w, conv0_pad, al_row, dtb_row)


def _gdn_tok_kernel(x_ref, g_ref, w_ref, wba_ref, cw_ref, c0_ref, al_ref, dtb_ref,
                    q_ref, k_ref, v_ref, bg_ref, tail_ref):
    h = _rms(x_ref[...], g_ref[...]).astype(BF16)
    cw = cw_ref[...]
    for c in range(C_CONV // PROJ_CHUNK):
        cols = slice(c * PROJ_CHUNK, (c + 1) * PROJ_CHUNK)
        u = jnp.dot(h, w_ref[:, cols], preferred_element_type=F32)
        conv = u * cw[CONV_K - 1:CONV_K, cols]
        for r in range(CONV_K - 1):
            conv = conv + c0_ref[r, :, cols] * cw[r:r + 1, cols]
            if r > 0:
                tail_ref[r - 1, :, cols] = c0_ref[r, :, cols]
        tail_ref[CONV_K - 2, :, cols] = u
        act = _silu(conv)
        for s in range(PROJ_CHUNK // LANES):
            slab = c * (PROJ_CHUNK // LANES) + s
            z = act[:, s * LANES:(s + 1) * LANES]
            if slab < 2 * H_B:
                z = z * lax.rsqrt(jnp.sum(z * z, axis=-1, keepdims=True) + EPS)
            (q_ref, k_ref, v_ref)[slab // H_B][:, (slab % H_B) * LANES:(slab % H_B + 1) * LANES] = z
    ba = jnp.dot(h, wba_ref[...], preferred_element_type=F32)
    lane = lax.broadcasted_iota(jnp.int32, (1, LANES), 1)
    z = ba + dtb_ref[...]
    softplus = jnp.maximum(z, 0.0) + jnp.log1p(jnp.exp(-jnp.abs(z)))
    bg_ref[...] = jnp.where(lane < H_B, jax.nn.sigmoid(ba), -jnp.exp(al_ref[...]) * softplus)


def _gdn_tok(x2d, g, w_qkv, w_ba, conv_w, c0, al_row, dtb_row):
    b, d = x2d.shape
    full = lambda *shape: pl.BlockSpec(shape, lambda i: (0,) * len(shape))
    return pl.pallas_call(
        _gdn_tok_kernel,
        grid=(1,),
        in_specs=[full(b, d), full(1, d), full(d, C_CONV), full(d, LANES), full(CONV_K, C_CONV),
                  full(CONV_K - 1, b, C_CONV), full(1, LANES), full(1, LANES)],
        out_specs=[full(b, H_B * DK_B), full(b, H_B * DK_B), full(b, W_B), full(b, LANES),
                   full(CONV_K - 1, b, C_CONV)],
        out_shape=[jax.ShapeDtypeStruct((b, H_B * DK_B), F32), jax.ShapeDtypeStruct((b, H_B * DK_B), F32),
                   jax.ShapeDtypeStruct((b, W_B), F32), jax.ShapeDtypeStruct((b, LANES), F32),
                   jax.ShapeDtypeStruct((CONV_K - 1, b, C_CONV), F32)],
        compiler_params=_cparams(("arbitrary",)),
        name="gdn_tok",
    )(x2d, g, w_qkv, w_ba, conv_w, c0, al_row, dtb_row)


def _score_page(q_bd, page_tile):
    kt = page_tile.reshape(q_bd.shape[1], page_tile.shape[-1]).astype(BF16)
    return jnp.dot(q_bd, kt, preferred_element_type=F32)


def _top3(gate, idx, axis, big):
    sels = []
    g = gate
    for _ in range(MOBA_TOPK):
        m = jnp.max(g, axis=axis, keepdims=True)
        ii = jnp.min(jnp.where(g == m, idx, big), axis=axis, keepdims=True)
        sels.append(ii)
        g = jnp.where(idx == ii, -jnp.inf, g)
    return sels


def _moba_prompt_kernel(pt_ref, q_ref, k_ref, vt_ref, kbar_ref, *rest, nb, n_pg):
    if n_pg:
        qb_ref, page_refs, (o_ref, sc_ref, vt_scr) = rest[0], rest[1:1 + n_pg], rest[1 + n_pg:]
    else:
        o_ref, vt_scr = rest
    i = pl.program_id(2)
    tq = MOBA_BLOCK
    n_heads = LANES // HD_A

    @pl.when(i == 0)
    def _():
        ones = jnp.ones((V_ROWS - HD_A, MOBA_BLOCK), BF16)
        for j in range(vt_scr.shape[0]):
            for hh in range(n_heads):
                if j < nb:
                    vt_scr[j, hh, 0:HD_A, :] = vt_ref[0, hh, :, j * MOBA_BLOCK:(j + 1) * MOBA_BLOCK].astype(BF16)
                    vt_scr[j, hh, HD_A:V_ROWS, :] = ones
                else:
                    vt_scr[j, hh] = jnp.zeros((V_ROWS, MOBA_BLOCK), BF16)

    def k_block(j):
        start = pl.multiple_of(jnp.minimum(j, nb - 1) * MOBA_BLOCK, MOBA_BLOCK)
        return k_ref[0, pl.ds(start, MOBA_BLOCK), :]

    q_t = q_ref[0].T
    kbar = kbar_ref[0]
    dim_head = lax.broadcasted_iota(jnp.int32, (LANES, 1), 0) // HD_A
    blk = lax.broadcasted_iota(jnp.int32, (nb, 1), 0)
    blk_f = blk.astype(F32)
    key_i = lax.broadcasted_iota(jnp.int32, (tq, tq), 0)
    qry_i = lax.broadcasted_iota(jnp.int32, (tq, tq), 1)
    heads = range(n_heads)

    def attend(j0, n_blk, keep, prev):
        raws = [[jnp.dot(kg, qs[hh], preferred_element_type=F32) for hh in heads]
                for kg in [k_block(j0 + g) for g in range(n_blk)]]
        ms = [None if prev is None else prev[hh][0] for hh in heads]
        m_at = [[None] * n_blk for _ in heads]
        pvs = [[None] * n_blk for _ in heads]
        for g in range(n_blk):
            for hh in heads:
                s = jnp.where(keep[hh][g], raws[g][hh].astype(BF16), NEG_BIG)
                m = jnp.max(s, axis=0, keepdims=True)
                if ms[hh] is not None:
                    m = jnp.maximum(ms[hh], m)
                p = jnp.exp2(s - m)
                pvs[hh][g] = jnp.dot(vt_scr[j0 + g, hh], p, preferred_element_type=F32)
                m_at[hh][g] = ms[hh] = m
        out = []
        for hh in heads:
            m_fin = ms[hh]
            rescale = lambda m_old: jnp.exp2(m_old.astype(F32) - m_fin.astype(F32))
            accl = None if prev is None else rescale(prev[hh][0]) * prev[hh][1]
            for g in range(n_blk):
                term = pvs[hh][g] if g == n_blk - 1 else rescale(m_at[hh][g]) * pvs[hh][g]
                accl = term if accl is None else accl + term
            out.append((m_fin, accl))
        return out

    qts = [jnp.where(dim_head == hh, q_t, 0.0) for hh in heads]
    qs = [(qts[hh] * (HD_A ** -0.5 * math.log2(math.e))).astype(BF16) for hh in heads]
    causal = key_i <= qry_i
    state0 = attend(i, 1, [[causal]] * n_heads, None)
    sels = []
    for hh in heads:
        gate = jnp.where(blk < i, _dot_hi(kbar, qts[hh]), -jnp.inf)
        sels.append(_top3(gate, blk_f, 0, float(nb)))
    for r in range(n_pg):
        sc_ref[0, r] = _score_page(qb_ref[0].astype(BF16), page_refs[r][0, 0])

    def body(t, carry):
        j0 = t * KV_GROUP
        keep = []
        for hh in heads:
            i1, i2, i3 = sels[hh]
            jfs = [(j0 + g).astype(F32) for g in range(KV_GROUP)]
            keep.append([(i1 == jf) | (i2 == jf) | (i3 == jf) for jf in jfs])
        state = attend(j0, KV_GROUP, keep, [carry[2 * hh:2 * hh + 2] for hh in heads])
        return tuple(x for st in state for x in st)

    carry = lax.fori_loop(0, (i + KV_GROUP - 1) // KV_GROUP, body, tuple(x for st in state0 for x in st))
    o_t = jnp.concatenate([carry[2 * hh + 1][:HD_A] / carry[2 * hh + 1][HD_A:HD_A + 1] for hh in heads],
                          axis=0)
    o_ref[0] = o_t.T


MAX_RIDE_PAGES = 16


def _moba_prompt(q3d, kb3d, vt4d, kbar, paged=None):
    b, t, w = q3d.shape
    nb = t // MOBA_BLOCK
    nb_pad = -(-nb // KV_GROUP) * KV_GROUP
    n_pairs = w // LANES
    hpp = LANES // HD_A
    n_steps = b * n_pairs * nb
    n_pg, pt_flat = 0, jnp.zeros((1,), jnp.int32)
    if paged is not None:
        q_b, cache_kt, layer, pt_flat_p, n_pages = paged
        bs = q_b.shape[0]
        _, _, h, hd, page = cache_kt.shape
        per_step = (bs * n_pages) // n_steps
        if (bs * n_pages) % n_steps == 0 and 0 < per_step <= MAX_RIDE_PAGES and n_pages % per_step == 0:
            n_pg, pt_flat = per_step, pt_flat_p
    qspec = pl.BlockSpec((1, MOBA_BLOCK, LANES), lambda bi, hp, i, pt: (bi, i, hp))
    in_specs = [qspec,
                pl.BlockSpec((1, t, LANES), lambda bi, hp, i, pt: (bi, 0, hp)),
                pl.BlockSpec((1, hpp, HD_A, t), lambda bi, hp, i, pt: (bi, hp, 0, 0)),
                pl.BlockSpec((1, nb, LANES), lambda bi, hp, i, pt: (bi, 0, hp))]
    out_specs = [qspec]
    out_shape = [jax.ShapeDtypeStruct((b, t, w), F32)]
    operands = [q3d, kb3d, vt4d, kbar]
    if n_pg:
        spr = n_pages // n_pg
        flat = lambda bi, hp, i: (bi * n_pairs + hp) * nb + i
        in_specs.append(pl.BlockSpec((1, h, h * hd), lambda bi, hp, i, pt: (flat(bi, hp, i) // spr, 0, 0)))
        in_specs += [pl.BlockSpec((1, 1, h, hd, page),
                                  lambda bi, hp, i, pt, r=r: (layer, pt[(flat(bi, hp, i) // spr) * n_pages
                                                                        + (flat(bi, hp, i) % spr) * n_pg + r],
                                                              0, 0, 0))
                     for r in range(n_pg)]
        out_specs.append(pl.BlockSpec((1, n_pg, h, page),
                                      lambda bi, hp, i, pt: (flat(bi, hp, i) // spr, flat(bi, hp, i) % spr, 0, 0)))
        out_shape.append(jax.ShapeDtypeStruct((bs, n_pages, h, page), F32))
        operands += [q_b] + [cache_kt] * n_pg
    res = pl.pallas_call(
        functools.partial(_moba_prompt_kernel, nb=nb, n_pg=n_pg),
        grid_spec=pltpu.PrefetchScalarGridSpec(
            num_scalar_prefetch=1,
            grid=(b, n_pairs, nb),
            in_specs=in_specs,
            out_specs=out_specs,
            scratch_shapes=[pltpu.VMEM((nb_pad, hpp, V_ROWS, MOBA_BLOCK), BF16)],
        ),
        out_shape=out_shape,
        compiler_params=_cparams(("arbitrary", "arbitrary", "arbitrary")),
        name="moba_prompt",
    )(pt_flat, *operands)
    return res[0], (res[1] if n_pg else None)


def _gdn_seq_kernel(q_ref, k_ref, v_ref, bg_ref, s0_ref, gn_ref, o_ref, s_out_ref, s_scr, *, rows):
    step = pl.program_id(1)
    n_steps = pl.num_programs(1)
    cc = GDN_CHUNK
    ncs = rows // cc

    @pl.when(step == 0)
    def _():
        s_scr[...] = s0_ref[0]

    bg = bg_ref[0]
    gn = gn_ref[...]
    ri = lax.broadcasted_iota(jnp.int32, (rows, rows), 0)
    ci = lax.broadcasted_iota(jnp.int32, (rows, rows), 1)
    same = (ri // cc) == (ci // cc)
    sum_mats = jnp.concatenate([(same & (ci <= ri)).astype(BF16), same.astype(BF16)], axis=0)
    b_hi = bg.astype(BF16)
    r_1 = bg - b_hi.astype(F32)
    b_mid = r_1.astype(BF16)
    b_lo = (r_1 - b_mid.astype(F32)).astype(BF16)
    sums = sum(jnp.dot(sum_mats, piece, preferred_element_type=F32) for piece in (b_hi, b_mid, b_lo))
    gc = sums[:rows]
    gl = sums[rows:]
    gc_t = gc.T
    e_gc = jnp.exp(gc)
    e_tail = jnp.exp(gl - gc)
    e_tot = jnp.exp(gl)
    pr = lax.broadcasted_iota(jnp.int32, (cc, rows), 0)
    pl_i = lax.broadcasted_iota(jnp.int32, (cc, rows), 1)
    pc = pl_i % cc
    lane_blk = pl_i // cc
    causal_p = pr >= pc
    strict_p = pr > pc

    def pack(full):
        out = full[(ncs - 1) * cc:ncs * cc]
        for c in range(ncs - 2, -1, -1):
            out = jnp.where(lane_blk == c, full[c * cc:(c + 1) * cc], out)
        return out

    def pack_col(col):
        out = col[(ncs - 1) * cc:ncs * cc]
        for c in range(ncs - 2, -1, -1):
            out = jnp.where(lane_blk == c, col[c * cc:(c + 1) * cc], out)
        return out

    def bdiag(p):
        return jnp.where(same, jnp.concatenate([p] * ncs, axis=0), 0.0)

    heads = range(H_B)
    col = lambda a, h: a[:, H_B + h:H_B + h + 1]
    qs = [q_ref[0, :, h * DK_B:(h + 1) * DK_B] * (DK_B ** -0.5) for h in heads]
    ks = [k_ref[0, :, h * DK_B:(h + 1) * DK_B] for h in heads]
    kbetas = [ks[h] * bg[:, h:h + 1] for h in heads]
    fulls = [_dot_nt(jnp.concatenate([kbetas[h], qs[h]], axis=0), ks[h]) for h in heads]
    decays = [jnp.exp(jnp.where(causal_p, pack_col(col(gc, h)) - gc_t[H_B + h:H_B + h + 1, :], -jnp.inf))
              for h in heads]
    attns = [jnp.where(causal_p, pack(fulls[h][rows:]) * decays[h], 0.0) for h in heads]
    es = [-jnp.where(strict_p, pack(fulls[h][:rows]) * decays[h], 0.0) for h in heads]
    pws = [_dot(es[h], bdiag(es[h])) for h in heads]
    for _ in range(int(math.log2(cc)) - 2):
        rs_ = [_dot(jnp.concatenate([es[h], pws[h]], axis=0), bdiag(pws[h])) for h in heads]
        es = [es[h] + pws[h] + rs_[h][:cc] for h in heads]
        pws = [rs_[h][cc:] for h in heads]
    es = [es[h] + pws[h] + _dot(es[h], bdiag(pws[h])) for h in heads]
    rhss = [jnp.concatenate([v_ref[0, :, h * DV_B:(h + 1) * DV_B] * bg[:, h:h + 1],
                             kbetas[h] * col(e_gc, h)], axis=1) for h in heads]
    sols = [rhss[h] + _dot(bdiag(es[h]), rhss[h]) for h in heads]
    q_decs = [qs[h] * col(e_gc, h) for h in heads]
    k_tails = [ks[h] * col(e_tail, h) for h in heads]
    ss = [s_scr[h] for h in heads]
    for c in range(ncs):
        rs = slice(c * cc, (c + 1) * cc)
        wqs = [_dot(jnp.concatenate([sols[h][rs, DV_B:], q_decs[h][rs]], axis=0), ss[h]) for h in heads]
        v_news = [sols[h][rs, :DV_B] - wqs[h][:cc] for h in heads]
        os_ = [wqs[h][cc:] + _dot(attns[h][:, rs], v_news[h]) for h in heads]
        kvs = [_dot_tn(k_tails[h][rs], v_news[h]) for h in heads]
        ss = [ss[h] * e_tot[c * cc:c * cc + 1, H_B + h:H_B + h + 1] + kvs[h] for h in heads]
        for h in heads:
            o_ref[0, rs, h * DV_B:(h + 1) * DV_B] = _rms(os_[h], gn)
    for h in heads:
        s_scr[h] = ss[h]


    @pl.when(step == n_steps - 1)
    def _():
        s_out_ref[0] = s_scr[...]


def _gdn_seq(q, k, v, bg, s0, g_onorm_row):
    b, t, _ = q.shape
    rows = MXU_DIM
    while t % rows:
        rows //= 2
    seq = lambda w: pl.BlockSpec((1, rows, w), lambda bi, c: (bi, c, 0))
    sspec = pl.BlockSpec((1, H_B, DK_B, DV_B), lambda bi, c: (bi, 0, 0, 0))
    return pl.pallas_call(
        functools.partial(_gdn_seq_kernel, rows=rows),
        grid=(b, t // rows),
        in_specs=[seq(H_B * DK_B), seq(H_B * DK_B), seq(W_B), seq(LANES), sspec,
                  pl.BlockSpec((1, DV_B), lambda bi, c: (0, 0))],
        out_specs=[seq(W_B), sspec],
        out_shape=[jax.ShapeDtypeStruct((b, t, W_B), F32),
                   jax.ShapeDtypeStruct((b, H_B, DK_B, DV_B), F32)],
        scratch_shapes=[pltpu.VMEM((H_B, DK_B, DV_B), F32)],
        compiler_params=_cparams(("parallel", "arbitrary")),
        name="gdn_seq",
    )(q, k, v, bg, s0, g_onorm_row)


def _gdn_step_kernel(kq_ref, v_ref, bg_ref, s0_ref, gn_ref, o_ref, s_out_ref):
    bg = bg_ref[0]
    gn = gn_ref[...]
    kq = kq_ref[0]
    for h in range(H_B):
        beta = bg[:, h:h + 1]
        eg = jnp.exp(bg[:, H_B + h:H_B + h + 1])
        k_col = kq[:, h:h + 1]
        q_col = kq[:, H_B + h:H_B + h + 1] * (DK_B ** -0.5)
        v_row = v_ref[0, :, h * DV_B:(h + 1) * DV_B]
        s0 = s0_ref[0, h]
        k_s = jnp.sum(k_col * s0, axis=0, keepdims=True)
        q_s = jnp.sum(q_col * s0, axis=0, keepdims=True)
        v_new = beta * (v_row - eg * k_s)
        qk = jnp.sum(q_col * k_col, axis=0, keepdims=True)
        o = eg * q_s + qk * v_new
        s_out_ref[0, h] = s0 * eg + k_col * v_new
        o_ref[0, :, h * DV_B:(h + 1) * DV_B] = _rms(o, gn)


def _gdn_step(kq_cols, v, bg, s0, g_onorm_row):
    b = v.shape[0]
    tok = lambda w: pl.BlockSpec((1, 1, w), lambda bi: (bi, 0, 0))
    sspec = pl.BlockSpec((1, H_B, DK_B, DV_B), lambda bi: (bi, 0, 0, 0))
    return pl.pallas_call(
        _gdn_step_kernel,
        grid=(b,),
        in_specs=[pl.BlockSpec((1, DK_B, 2 * H_B), lambda bi: (bi, 0, 0)), tok(W_B), tok(LANES), sspec,
                  pl.BlockSpec((1, DV_B), lambda bi: (0, 0))],
        out_specs=[tok(W_B), sspec],
        out_shape=[jax.ShapeDtypeStruct((b, 1, W_B), F32),
                   jax.ShapeDtypeStruct((b, H_B, DK_B, DV_B), F32)],
        compiler_params=_cparams(("parallel",)),
        name="gdn_step",
    )(kq_cols, v, bg, s0, g_onorm_row)


def _out_stage_kernel(x_ref, oa_ref, ob_ref, p_ref, gmix_ref, wzg_ref, wpa_ref, wpb_ref, wo_ref,
                      gple_ref, wpg_ref, wple_ref, gfin_ref, y_ref, *, final):
    x = x_ref[...]
    h = _rms(x, gmix_ref[...]).astype(BF16)
    zg = jnp.dot(h, wzg_ref[...], preferred_element_type=F32)
    za = zg[:, :W_A]
    zb = zg[:, W_A:W_A + W_B]
    d = x.shape[1]
    ga = zg[:, W_A + W_B:W_A + W_B + d]
    gb = zg[:, W_A + W_B + d:]
    ya = _dot(oa_ref[...] * _silu(za), wpa_ref[...])
    yb = _dot(ob_ref[...] * _silu(zb), wpb_ref[...])
    mixed = jax.nn.sigmoid(ga) * ya + jax.nn.sigmoid(gb) * yb
    x = x + _dot(mixed, wo_ref[...])
    gate = jax.nn.sigmoid(_dot(_rms(x, gple_ref[...]), wpg_ref[...]))
    x = x + gate * _dot(p_ref[...], wple_ref[...])
    y_ref[...] = _rms(x, gfin_ref[...]) if final else x


def _out_stage(x2d, oa, ob, p2d, gmix, wzg, wpa, wpb, wo, gple, wpg, wple, gfin, tm, final):
    n, d = x2d.shape
    row = lambda w: pl.BlockSpec((tm, w), lambda i: (i, 0))
    const = lambda a: pl.BlockSpec(a.shape, lambda i: (0, 0))
    return pl.pallas_call(
        functools.partial(_out_stage_kernel, final=final),
        grid=(n // tm,),
        in_specs=[row(d), row(W_A), row(W_B), row(p2d.shape[1]), const(gmix), const(wzg), const(wpa),
                  const(wpb), const(wo), const(gple), const(wpg), const(wple), const(gfin)],
        out_specs=row(d),
        out_shape=jax.ShapeDtypeStruct((n, d), F32),
        compiler_params=_cparams(("parallel",)),
        name="out_stage",
    )(x2d, oa, ob, p2d, gmix, wzg, wpa, wpb, wo, gple, wpg, wple, gfin)


PAGES_PER_STEP = 32


def _page_scores_kernel(pt_ref, q_ref, *refs):
    k_refs, o_ref = refs[:-1], refs[-1]
    q_bd = q_ref[0].astype(BF16)
    for r, k_ref in enumerate(k_refs):
        o_ref[0, r] = _score_page(q_bd, k_ref[0, 0])


def _page_scores(q_bd, cache_kt, layer, pt_flat, n_pages):
    b = q_bd.shape[0]
    _, _, h, hd, page = cache_kt.shape
    nps = PAGES_PER_STEP if n_pages % PAGES_PER_STEP == 0 else 1
    in_specs = [pl.BlockSpec((1, h, h * hd), lambda bi, g, pt: (bi, 0, 0))]
    in_specs += [pl.BlockSpec((1, 1, h, hd, page),
                              lambda bi, g, pt, r=r: (layer, pt[bi * n_pages + g * nps + r], 0, 0, 0))
                 for r in range(nps)]
    return pl.pallas_call(
        _page_scores_kernel,
        grid_spec=pltpu.PrefetchScalarGridSpec(
            num_scalar_prefetch=1,
            grid=(b, n_pages // nps),
            in_specs=in_specs,
            out_specs=pl.BlockSpec((1, nps, h, page), lambda bi, g, pt: (bi, g, 0, 0)),
        ),
        out_shape=jax.ShapeDtypeStruct((b, n_pages, h, page), F32),
        compiler_params=_cparams(("parallel", "parallel")),
        name="page_scores",
    )(pt_flat, q_bd, *([cache_kt] * nps))


def _sample_select_kernel(p_ref, o_ref, *, nblk, ppb):
    tot = jnp.sum(p_ref[0], axis=2, keepdims=True)
    gate = jnp.sum(tot.reshape(nblk, ppb, H_A, 1), axis=1) * (1.0 / MOBA_BLOCK)
    blk_f = lax.broadcasted_iota(jnp.int32, (nblk, 1, 1), 0).astype(F32)
    for r, sel in enumerate(_top3(gate, blk_f, 0, float(nblk))):
        o_ref[0, r] = jnp.broadcast_to(jnp.minimum(sel[0], nblk - 1.0), (H_A, LANES)).astype(jnp.int32)


def _sample_select(scores, ppb):
    b, n_pages, h, page = scores.shape
    return pl.pallas_call(
        functools.partial(_sample_select_kernel, nblk=n_pages // ppb, ppb=ppb),
        grid=(b,),
        in_specs=[pl.BlockSpec((1, n_pages, h, page), lambda bi: (bi, 0, 0, 0))],
        out_specs=pl.BlockSpec((1, MOBA_TOPK, h, LANES), lambda bi: (bi, 0, 0, 0)),
        out_shape=jax.ShapeDtypeStruct((b, MOBA_TOPK, h, LANES), jnp.int32),
        compiler_params=_cparams(("parallel",)),
        name="sample_select",
    )(scores)


PAGED_HEADS_PER_STEP = 4


def _paged_attn_kernel(pt_ref, top_ref, q_ref, kn_ref, vn_ref, s_ref, *refs, n_s, ppb):
    v_refs, o_ref = refs[:-1], refs[-1]
    bi = pl.program_id(0)
    hg = pl.program_id(1)
    scale = HD_A ** -0.5
    lane = lax.broadcasted_iota(jnp.int32, (1, H_A), 1)
    row = lax.broadcasted_iota(jnp.int32, (H_A, 1), 0)
    out = jnp.zeros((HD_A, H_A), F32)
    for j in range(PAGED_HEADS_PER_STEP):
        h = hg * PAGED_HEADS_PER_STEP + j
        pick = lambda a, h=h: jnp.sum(jnp.where(row == h, a, 0.0), axis=0, keepdims=True)
        own = jnp.sum(pick(q_ref[0] * kn_ref[0]), axis=1, keepdims=True) * scale
        rows = []
        for s in range(n_s):
            page = top_ref[(bi * MOBA_TOPK + s // ppb) * H_A + h] * ppb + s % ppb
            rows.append(pick(s_ref[0, page]) * scale)
        m = own
        for r in rows:
            m = jnp.maximum(m, jnp.max(r, axis=1, keepdims=True))
        p_own = jnp.exp(own - m)
        ps = [jnp.exp(r - m) for r in rows]
        l = p_own + sum(jnp.sum(p, axis=1, keepdims=True) for p in ps)
        acc = sum(p * v_refs[j * n_s + s][0, 0, 0] for s, p in enumerate(ps))
        vn_col = jnp.sum(jnp.where(lane == h, vn_ref[0], 0.0), axis=1, keepdims=True)
        res = (jnp.sum(acc, axis=1, keepdims=True) + p_own * vn_col) / l
        out = jnp.where(lane == h, res, out)

    @pl.when(hg == 0)
    def _():
        o_ref[0] = out

    @pl.when(hg > 0)
    def _():
        o_ref[0] = o_ref[0] + out


def _paged_attn(q3, kn3, vn_t, scores, cache_vt, layer, pt_flat, top_flat, n_pages):
    b = q3.shape[0]
    _, _, h, hd, page = cache_vt.shape
    ppb = MOBA_BLOCK // page
    n_s = MOBA_TOPK * ppb

    hps = PAGED_HEADS_PER_STEP

    def v_spec(j, s):
        def index(bi, hg, pt, top):
            hi = hg * hps + j
            logical = top[(bi * MOBA_TOPK + s // ppb) * H_A + hi] * ppb + s % ppb
            return (layer, pt[bi * n_pages + logical], hi, 0, 0)
        return pl.BlockSpec((1, 1, 1, hd, page), index)

    tok = pl.BlockSpec((1, h, hd), lambda bi, hg, pt, top: (bi, 0, 0))
    tok_t = pl.BlockSpec((1, hd, h), lambda bi, hg, pt, top: (bi, 0, 0))
    return pl.pallas_call(
        functools.partial(_paged_attn_kernel, n_s=n_s, ppb=ppb),
        grid_spec=pltpu.PrefetchScalarGridSpec(
            num_scalar_prefetch=2,
            grid=(b, h // hps),
            in_specs=[tok, tok, tok_t,
                      pl.BlockSpec((1, n_pages, h, page), lambda bi, hg, pt, top: (bi, 0, 0, 0))]
            + [v_spec(j, s) for j in range(hps) for s in range(n_s)],
            out_specs=tok_t,
        ),
        out_shape=jax.ShapeDtypeStruct((b, hd, h), F32),
        compiler_params=_cparams(("parallel", "arbitrary")),
        name="paged_attn",
    )(pt_flat, top_flat, q3, kn3, vn_t, scores, *([cache_vt] * (hps * n_s)))


def _split_weights(w_in_l, d):
    o = 0
    parts = {}
    for name, n in (("qa", W_A), ("ka", W_A), ("va", W_A), ("za", W_A), ("qb", H_B * DK_B),
                    ("kb", H_B * DK_B), ("vb", W_B), ("zb", W_B), ("beta", H_B), ("alpha", H_B),
                    ("ga", d), ("gb", d)):
        parts[name] = w_in_l[:, o:o + n]
        o += n
    w_attn = jnp.concatenate([parts["qa"], parts["ka"], parts["va"]], axis=1).astype(BF16)
    w_qkv = jnp.concatenate([parts["qb"], parts["kb"], parts["vb"]], axis=1).astype(BF16)
    w_ba = jnp.concatenate([parts["beta"], parts["alpha"],
                            jnp.zeros((d, LANES - 2 * H_B), F32)], axis=1).astype(BF16)
    w_zg = jnp.concatenate([parts["za"], parts["zb"], parts["ga"], parts["gb"]], axis=1).astype(BF16)
    return w_attn, w_qkv, w_ba, w_zg


def _lane_row(vec, offset):
    out = jnp.zeros((1, LANES), F32)
    return out.at[0, offset:offset + vec.shape[0]].set(vec.astype(F32))


def _pick_tile(n, pref):
    t = pref
    while n % t:
        t //= 2
    return t


def kernel(x_prompt, x_sample, p_prompt, p_sample, cache_k, cache_v, page_table, state_gdn_s, state_gdn_conv, g_mix, w_in, conv_w, a_log, dt_bias, g_onorm, w_pa, w_pb, w_o, g_ple, w_ple_gate, w_ple, g_final):
    bp, tp, d = x_prompt.shape
    bs, ts, _ = x_sample.shape
    depth = w_in.shape[0]
    n_pages = page_table.shape[1]
    page = cache_k.shape[2]
    assert ts == 1 and tp % MOBA_BLOCK == 0 and (n_pages * page) % MOBA_BLOCK == 0
    assert tp % GDN_CHUNK == 0

    pos_p = jnp.arange(tp, dtype=jnp.int32)
    pos_s = jnp.full((bs,), n_pages * page, dtype=jnp.int32)
    tabs_p = _rope_tables(pos_p)
    tabs_s = _rope_tables(pos_s)
    pt_flat = page_table.reshape(-1).astype(jnp.int32)
    cache_kt = jnp.transpose(cache_k, (0, 1, 3, 4, 2))
    cache_vt = jnp.transpose(cache_v, (0, 1, 3, 4, 2))

    xp = x_prompt.reshape(bp * tp, d)
    xs = x_sample.reshape(bs, d)
    outs = {k: [] for k in ("kp", "vp", "sp", "cp", "ks", "vs", "ss", "cs")}
    for l in range(depth):
        final = l == depth - 1
        w_attn, w_qkv, w_ba, w_zg = _split_weights(w_in[l], d)
        gmix = g_mix[l].reshape(1, d)
        al_row = _lane_row(a_log[l], H_B)
        dtb_row = _lane_row(dt_bias[l], H_B)
        gon = g_onorm[l].reshape(1, DV_B)
        wpa, wpb, wo = w_pa[l].astype(BF16), w_pb[l].astype(BF16), w_o[l].astype(BF16)
        wpg, wple = w_ple_gate[l].astype(BF16), w_ple[l].astype(BF16)
        gple = g_ple[l].reshape(1, d)
        gfin = g_final.reshape(1, d)

        qa_s, ka_s, va_s = _attn_proj(xs, gmix, w_attn, tabs_s, bs)
        q3 = qa_s.reshape(bs, H_A, HD_A)
        q_b = (jnp.eye(H_A, dtype=F32)[None, :, :, None] * q3[:, None, :, :]).reshape(bs, H_A, W_A)

        qa, kb, kt, vt, kbar = _attn_proj_seq(xp, gmix, w_attn, tabs_p, _pick_tile(tp, 512), bp)
        conv0 = jnp.zeros((bp, SUBLANES, C_CONV), F32)
        qn, kn, vn, bg, tail = _gdn_proj(xp.reshape(bp, tp, d), gmix, w_qkv, w_ba, conv_w[l], conv0,
                                         al_row, dtb_row, _pick_tile(tp, 256), tp)
        oa, scores = _moba_prompt(qa.reshape(bp, tp, W_A), kb.reshape(bp, tp, W_A), vt,
                                  kbar.reshape(bp, tp // MOBA_BLOCK, W_A), (q_b, cache_kt, l, pt_flat, n_pages))
        s0 = jnp.zeros((bp, H_B, DK_B, DV_B), F32)
        ob, s_fin = _gdn_seq(qn, kn, vn, bg, s0, gon)
        xp = _out_stage(xp, oa.reshape(bp * tp, W_A), ob.reshape(bp * tp, W_B),
                        p_prompt[l].reshape(bp * tp, -1), gmix, w_zg, wpa, wpb, wo, gple, wpg, wple,
                        gfin, _pick_tile(bp * tp, 256), final)
        outs["kp"].append(jnp.transpose(kt, (0, 3, 1, 2)))
        outs["vp"].append(jnp.transpose(vt, (0, 3, 1, 2)))
        outs["sp"].append(s_fin.astype(state_gdn_s.dtype))
        outs["cp"].append(tail[:, SUBLANES - (CONV_K - 1):, :])

        conv0_s = jnp.transpose(state_gdn_conv[l].astype(F32), (1, 0, 2))
        qn_s, kn_s, vn_s, bg_s, tail_s = _gdn_tok(xs, gmix, w_qkv, w_ba, conv_w[l], conv0_s, al_row, dtb_row)
        if scores is None:
            scores = _page_scores(q_b, cache_kt, l, pt_flat, n_pages)
        top = _sample_select(scores, MOBA_BLOCK // page)
        top_flat = top[:, :, :, 0].reshape(-1)
        vn_t = jnp.transpose(va_s.reshape(bs, H_A, HD_A), (0, 2, 1))
        oa_t = _paged_attn(q3, ka_s.reshape(bs, H_A, HD_A), vn_t, scores, cache_vt, l, pt_flat, top_flat,
                           n_pages)
        oa_s = jnp.transpose(oa_t, (0, 2, 1))
        cols = lambda a: jnp.transpose(a.reshape(bs, H_B, DK_B), (0, 2, 1))
        kq_cols = jnp.concatenate([cols(kn_s), cols(qn_s)], axis=2)
        ob_s, s_fin_s = _gdn_step(kq_cols, vn_s.reshape(bs, 1, W_B), bg_s.reshape(bs, 1, LANES),
                                  state_gdn_s[l].astype(F32), gon)
        xs = _out_stage(xs, oa_s.reshape(bs, W_A), ob_s[:, 0, :], p_sample[l].reshape(bs, -1), gmix, w_zg,
                        wpa, wpb, wo, gple, wpg, wple, gfin, bs, final)
        outs["ks"].append(ka_s.reshape(bs, 1, H_A, HD_A))
        outs["vs"].append(va_s.reshape(bs, 1, H_A, HD_A))
        outs["ss"].append(s_fin_s.astype(state_gdn_s.dtype))
        outs["cs"].append(jnp.transpose(tail_s, (1, 0, 2)))

    y_prompt = xp.reshape(bp, tp, d)
    y_sample = xs.reshape(bs, ts, d)
    st = lambda k: jnp.stack(outs[k])
    return (y_prompt, y_sample, st("kp"), st("vp"), st("sp"), st("cp"),
            st("ks"), st("vs"), st("ss"), st("cs"))
```

```python
import functools
import math

import jax
import jax.numpy as jnp
from jax import lax
from jax.experimental import pallas as pl
from jax.experimental.pallas import tpu as pltpu

F32 = jnp.float32
BF16 = jnp.bfloat16
HIGHEST = lax.Precision.HIGHEST

H_A = 8
HD_A = 64
W_A = H_A * HD_A
ROT_DIM = HD_A // 4
ROPE_THETA = 500000.0
MOBA_BLOCK = 256
MOBA_TOPK = 3
H_B = 8
DK_B = 128
DV_B = 128
W_B = H_B * DV_B
CONV_K = 4
C_CONV = H_B * (2 * DK_B + DV_B)
GDN_CHUNK = 64
EPS = 1e-6

LANES = 128
SUBLANES = 8
MXU_DIM = 256
VMEM_LIMIT_BYTES = 56 * 1024 * 1024

NEG_BIG = -1e30
KV_GROUP = 8
PROJ_CHUNK = 512
V_ROWS = HD_A + 16


def _cparams(sem):
    return pltpu.CompilerParams(dimension_semantics=sem, vmem_limit_bytes=VMEM_LIMIT_BYTES)


def _rms(x, g):
    return x * lax.rsqrt(jnp.mean(x * x, axis=-1, keepdims=True) + EPS) * g


def _silu(x):
    return x * jax.nn.sigmoid(x)


def _dot(a, b):
    return jnp.dot(a.astype(BF16), b.astype(BF16), preferred_element_type=F32)


def _dot_nt(a, b):
    return lax.dot_general(a.astype(BF16), b.astype(BF16), (((1,), (1,)), ((), ())),
                           preferred_element_type=F32)


def _dot_tn(a, b):
    return lax.dot_general(a.astype(BF16), b.astype(BF16), (((0,), (0,)), ((), ())),
                           preferred_element_type=F32)


def _dot_hi(a, b):
    return jnp.dot(a, b, precision=HIGHEST, preferred_element_type=F32)


def _attn_qkv(x_ref, g_ref, w_ref, c_ref, sa_ref, sb_ref):
    h = _rms(x_ref[...], g_ref[...]).astype(BF16)
    y = jnp.dot(h, w_ref[...], preferred_element_type=F32)
    c, sa, sb = c_ref[...], sa_ref[...], sb_ref[...]
    half = ROT_DIM // 2
    slabs = lambda base: [y[:, base + s * LANES: base + (s + 1) * LANES] for s in range(W_A // LANES)]
    rope = lambda z: z * c + pltpu.roll(z, LANES - half, 1) * sa + pltpu.roll(z, half, 1) * sb
    return [rope(z) for z in slabs(0)], [rope(z) for z in slabs(W_A)], slabs(2 * W_A)


def _attn_proj_kernel(x_ref, g_ref, w_ref, c_ref, sa_ref, sb_ref, q_ref, k_ref, v_ref):
    qs, ks, vs = _attn_qkv(x_ref, g_ref, w_ref, c_ref, sa_ref, sb_ref)
    for s in range(W_A // LANES):
        q_ref[:, s * LANES:(s + 1) * LANES] = qs[s]
        k_ref[:, s * LANES:(s + 1) * LANES] = ks[s]
        v_ref[:, s * LANES:(s + 1) * LANES] = vs[s]


def _attn_proj_seq_kernel(x_ref, g_ref, w_ref, c_ref, sa_ref, sb_ref, q_ref, kb_ref, kt_ref, vt_ref, kbar_ref):
    qs, ks, vs = _attn_qkv(x_ref, g_ref, w_ref, c_ref, sa_ref, sb_ref)
    tm = x_ref.shape[0]
    hps = LANES // HD_A
    for s in range(W_A // LANES):
        q_ref[:, s * LANES:(s + 1) * LANES] = qs[s]
        kb_ref[:, s * LANES:(s + 1) * LANES] = ks[s].astype(BF16)
        kt_ref[0, s * hps:(s + 1) * hps] = ks[s].T.reshape(hps, HD_A, tm)
        vt_ref[0, s * hps:(s + 1) * hps] = vs[s].T.reshape(hps, HD_A, tm)
        kbar_ref[0, :, s * LANES:(s + 1) * LANES] = jnp.sum(
            ks[s].reshape(tm // MOBA_BLOCK, MOBA_BLOCK, LANES), axis=1) * (1.0 / MOBA_BLOCK)


def _rope_tables(pos):
    half = ROT_DIM // 2
    inv = jnp.power(ROPE_THETA, -jnp.arange(half, dtype=F32) / half)
    ang = pos.astype(F32)[:, None] * inv[None, :]
    cos, sin = jnp.cos(ang), jnp.sin(ang)
    t = pos.shape[0]
    ones = jnp.ones((t, HD_A - ROT_DIM), F32)
    zeros = jnp.zeros((t, HD_A - ROT_DIM), F32)
    zh = jnp.zeros((t, half), F32)
    c = jnp.concatenate([cos, cos, ones], axis=1)
    sa = jnp.concatenate([-sin, zh, zeros], axis=1)
    sb = jnp.concatenate([zh, sin, zeros], axis=1)
    rep = LANES // HD_A
    return tuple(jnp.tile(a, (1, rep)) for a in (c, sa, sb))


def _attn_proj(x2d, g, w, tabs, tm):
    n, d = x2d.shape
    t_tab = tabs[0].shape[0]
    period = t_tab // tm
    tab_spec = pl.BlockSpec((tm, LANES), lambda i: (i % period, 0))
    out_spec = pl.BlockSpec((tm, W_A), lambda i: (i, 0))
    out = jax.ShapeDtypeStruct((n, W_A), F32)
    return pl.pallas_call(
        _attn_proj_kernel,
        grid=(n // tm,),
        in_specs=[pl.BlockSpec((tm, d), lambda i: (i, 0)),
                  pl.BlockSpec((1, d), lambda i: (0, 0)),
                  pl.BlockSpec((d, 3 * W_A), lambda i: (0, 0)),
                  tab_spec, tab_spec, tab_spec],
        out_specs=[out_spec, out_spec, out_spec],
        out_shape=[out, out, out],
        compiler_params=_cparams(("parallel",)),
        name="attn_proj",
    )(x2d, g, w, *tabs)


def _attn_proj_seq(x2d, g, w, tabs, tm, batch):
    n, d = x2d.shape
    t = n // batch
    period = t // tm
    bpt = tm // MOBA_BLOCK
    tab_spec = pl.BlockSpec((tm, LANES), lambda i: (i % period, 0))
    row_spec = pl.BlockSpec((tm, W_A), lambda i: (i, 0))
    t_spec = pl.BlockSpec((1, H_A, HD_A, tm), lambda i: (i // period, 0, 0, i % period))
    t_shape = jax.ShapeDtypeStruct((batch, H_A, HD_A, t), F32)
    return pl.pallas_call(
        _attn_proj_seq_kernel,
        grid=(n // tm,),
        in_specs=[pl.BlockSpec((tm, d), lambda i: (i, 0)),
                  pl.BlockSpec((1, d), lambda i: (0, 0)),
                  pl.BlockSpec((d, 3 * W_A), lambda i: (0, 0)),
                  tab_spec, tab_spec, tab_spec],
        out_specs=[row_spec, row_spec, t_spec, t_spec, pl.BlockSpec((1, bpt, W_A), lambda i: (i, 0, 0))],
        out_shape=[jax.ShapeDtypeStruct((n, W_A), F32), jax.ShapeDtypeStruct((n, W_A), BF16), t_shape, t_shape,
                   jax.ShapeDtypeStruct((n // tm, bpt, W_A), F32)],
        compiler_params=_cparams(("parallel",)),
        name="attn_proj_seq",
    )(x2d, g, w, *tabs)


def _gdn_proj_kernel(x_ref, g_ref, w_ref, wba_ref, cw_ref, c0_ref, al_ref, dtb_ref,
                     q_ref, k_ref, v_ref, bg_ref, tail_ref, ubuf, *, tm, t_valid, tail_row):
    i = pl.program_id(1)

    @pl.when(i == 0)
    def _():
        ubuf[0:SUBLANES, :] = c0_ref[0]

    @pl.when(i > 0)
    def _():
        ubuf[0:SUBLANES, :] = ubuf[tm:tm + SUBLANES, :]

    h = _rms(x_ref[0], g_ref[...]).astype(BF16)
    if t_valid % tm != 0:
        row = i * tm + lax.broadcasted_iota(jnp.int32, (tm, 1), 0)
        valid = (row < t_valid).astype(F32)
    else:
        valid = None

    def put(ref, s, val):
        ref[0, :, s * LANES:(s + 1) * LANES] = val if valid is None else val * valid

    cw = cw_ref[...]
    n_chunks = C_CONV // PROJ_CHUNK
    dot_chunk = lambda c: jnp.dot(h, w_ref[:, c * PROJ_CHUNK:(c + 1) * PROJ_CHUNK], preferred_element_type=F32)
    nxt = dot_chunk(0)
    for c in range(n_chunks):
        u = nxt
        if c + 1 < n_chunks:
            nxt = dot_chunk(c + 1)
        cols = slice(c * PROJ_CHUNK, (c + 1) * PROJ_CHUNK)
        ubuf[SUBLANES:SUBLANES + tm, cols] = u
        conv = u * cw[CONV_K - 1:CONV_K, cols]
        for j in range(1, CONV_K):
            conv = conv + ubuf[SUBLANES - j:SUBLANES - j + tm, cols] * cw[CONV_K - 1 - j:CONV_K - j, cols]
        act = _silu(conv)
        for s in range(PROJ_CHUNK // LANES):
            slab = c * (PROJ_CHUNK // LANES) + s
            z = act[:, s * LANES:(s + 1) * LANES]
            if slab < 2 * H_B:
                z = z * lax.rsqrt(jnp.sum(z * z, axis=-1, keepdims=True) + EPS)
            put((q_ref, k_ref, v_ref)[slab // H_B], slab % H_B, z)

    ba = jnp.dot(h, wba_ref[...], preferred_element_type=F32)
    lane = lax.broadcasted_iota(jnp.int32, (1, LANES), 1)
    z = ba + dtb_ref[...]
    softplus = jnp.maximum(z, 0.0) + jnp.log1p(jnp.exp(-jnp.abs(z)))
    bg = jnp.where(lane < H_B, jax.nn.sigmoid(ba), -jnp.exp(al_ref[...]) * softplus)
    bg_ref[0] = bg if valid is None else bg * valid
    tail_ref[0] = ubuf[tail_row:tail_row + SUBLANES, :]


def _gdn_proj(x3d, g, w_qkv, w_ba, conv_w, conv0_pad, al_row, dtb_row, tm, t_valid):
    b, t, d = x3d.shape
    n_tiles = -(-t_valid // tm)
    tail_row = t_valid - (n_tiles - 1) * tm
    kern = functools.partial(_gdn_proj_kernel, tm=tm, t_valid=t_valid, tail_row=tail_row)
    seq = lambda w: pl.BlockSpec((1, tm, w), lambda bi, i: (bi, i, 0))
    const = lambda r, c: pl.BlockSpec((r, c), lambda bi, i: (0, 0))
    return pl.pallas_call(
        kern,
        grid=(b, n_tiles),
        in_specs=[seq(d), const(1, d), const(d, C_CONV), const(d, LANES), const(CONV_K, C_CONV),
                  pl.BlockSpec((1, SUBLANES, C_CONV), lambda bi, i: (bi, 0, 0)),
                  const(1, LANES), const(1, LANES)],
        out_specs=[seq(H_B * DK_B), seq(H_B * DK_B), seq(W_B), seq(LANES),
                   pl.BlockSpec((1, SUBLANES, C_CONV), lambda bi, i: (bi, 0, 0))],
        out_shape=[jax.ShapeDtypeStruct((b, n_tiles * tm, H_B * DK_B), F32),
                   jax.ShapeDtypeStruct((b, n_tiles * tm, H_B * DK_B), F32),
                   jax.ShapeDtypeStruct((b, n_tiles * tm, W_B), F32),
                   jax.ShapeDtypeStruct((b, n_tiles * tm, LANES), F32),
                   jax.ShapeDtypeStruct((b, SUBLANES, C_CONV), F32)],
        scratch_shapes=[pltpu.VMEM((tm + 2 * SUBLANES, C_CONV), F32)],
        compiler_params=_cparams(("parallel", "arbitrary")),
        name="gdn_proj",
    )(x3d, g, w_qkv, w_ba, conv_w, conv0_pad, al_row, dtb_row)


def _gdn_tok_kernel(x_ref, g_ref, w_ref, wba_ref, cw_ref, c0_ref, al_ref, dtb_ref,
                    q_ref, k_ref, v_ref, bg_ref, tail_ref):
    h = _rms(x_ref[...], g_ref[...]).astype(BF16)
    cw = cw_ref[...]
    for c in range(C_CONV // PROJ_CHUNK):
        cols = slice(c * PROJ_CHUNK, (c + 1) * PROJ_CHUNK)
        u = jnp.dot(h, w_ref[:, cols], preferred_element_type=F32)
        conv = u * cw[CONV_K - 1:CONV_K, cols]
        for r in range(CONV_K - 1):
            conv = conv + c0_ref[r, :, cols] * cw[r:r + 1, cols]
            if r > 0:
                tail_ref[r - 1, :, cols] = c0_ref[r, :, cols]
        tail_ref[CONV_K - 2, :, cols] = u
        act = _silu(conv)
        for s in range(PROJ_CHUNK // LANES):
            slab = c * (PROJ_CHUNK // LANES) + s
            z = act[:, s * LANES:(s + 1) * LANES]
            if slab < 2 * H_B:
                z = z * lax.rsqrt(jnp.sum(z * z, axis=-1, keepdims=True) + EPS)
            (q_ref, k_ref, v_ref)[slab // H_B][:, (slab % H_B) * LANES:(slab % H_B + 1) * LANES] = z
    ba = jnp.dot(h, wba_ref[...], preferred_element_type=F32)
    lane = lax.broadcasted_iota(jnp.int32, (1, LANES), 1)
    z = ba + dtb_ref[...]
    softplus = jnp.maximum(z, 0.0) + jnp.log1p(jnp.exp(-jnp.abs(z)))
    bg_ref[...] = jnp.where(lane < H_B, jax.nn.sigmoid(ba), -jnp.exp(al_ref[...]) * softplus)


def _gdn_tok(x2d, g, w_qkv, w_ba, conv_w, c0, al_row, dtb_row):
    b, d = x2d.shape
    full = lambda *shape: pl.BlockSpec(shape, lambda i: (0,) * len(shape))
    return pl.pallas_call(
        _gdn_tok_kernel,
        grid=(1,),
        in_specs=[full(b, d), full(1, d), full(d, C_CONV), full(d, LANES), full(CONV_K, C_CONV),
                  full(CONV_K - 1, b, C_CONV), full(1, LANES), full(1, LANES)],
        out_specs=[full(b, H_B * DK_B), full(b, H_B * DK_B), full(b, W_B), full(b, LANES),
                   full(CONV_K - 1, b, C_CONV)],
        out_shape=[jax.ShapeDtypeStruct((b, H_B * DK_B), F32), jax.ShapeDtypeStruct((b, H_B * DK_B), F32),
                   jax.ShapeDtypeStruct((b, W_B), F32), jax.ShapeDtypeStruct((b, LANES), F32),
                   jax.ShapeDtypeStruct((CONV_K - 1, b, C_CONV), F32)],
        compiler_params=_cparams(("arbitrary",)),
        name="gdn_tok",
    )(x2d, g, w_qkv, w_ba, conv_w, c0, al_row, dtb_row)


def _score_page(q_bd, page_tile):
    kt = page_tile.reshape(q_bd.shape[1], page_tile.shape[-1]).astype(BF16)
    return jnp.dot(q_bd, kt, preferred_element_type=F32)


def _top3(gate, idx, axis, big):
    sels = []
    g = gate
    for _ in range(MOBA_TOPK):
        m = jnp.max(g, axis=axis, keepdims=True)
        ii = jnp.min(jnp.where(g == m, idx, big), axis=axis, keepdims=True)
        sels.append(ii)
        g = jnp.where(idx == ii, -jnp.inf, g)
    return sels


def _moba_prompt_kernel(pt_ref, q_ref, k_ref, vt_ref, kbar_ref, *rest, nb, n_pg):
    if n_pg:
        qb_ref, page_refs, (o_ref, sc_ref, vt_scr) = rest[0], rest[1:1 + n_pg], rest[1 + n_pg:]
    else:
        o_ref, vt_scr = rest
    i = pl.program_id(2)
    tq = MOBA_BLOCK
    n_heads = LANES // HD_A

    @pl.when(i == 0)
    def _():
        ones = jnp.ones((V_ROWS - HD_A, MOBA_BLOCK), BF16)
        for j in range(vt_scr.shape[0]):
            for hh in range(n_heads):
                if j < nb:
                    vt_scr[j, hh, 0:HD_A, :] = vt_ref[0, hh, :, j * MOBA_BLOCK:(j + 1) * MOBA_BLOCK].astype(BF16)
                    vt_scr[j, hh, HD_A:V_ROWS, :] = ones
                else:
                    vt_scr[j, hh] = jnp.zeros((V_ROWS, MOBA_BLOCK), BF16)

    def k_block(j):
        start = pl.multiple_of(jnp.minimum(j, nb - 1) * MOBA_BLOCK, MOBA_BLOCK)
        return k_ref[0, pl.ds(start, MOBA_BLOCK), :]

    q_t = q_ref[0].T
    kbar = kbar_ref[0]
    dim_head = lax.broadcasted_iota(jnp.int32, (LANES, 1), 0) // HD_A
    blk = lax.broadcasted_iota(jnp.int32, (nb, 1), 0)
    blk_f = blk.astype(F32)
    key_i = lax.broadcasted_iota(jnp.int32, (tq, tq), 0)
    qry_i = lax.broadcasted_iota(jnp.int32, (tq, tq), 1)
    heads = range(n_heads)

    def attend(j0, n_blk, keep, prev):
        raws = [[jnp.dot(kg, qs[hh], preferred_element_type=F32) for hh in heads]
                for kg in [k_block(j0 + g) for g in range(n_blk)]]
        ms = [None if prev is None else prev[hh][0] for hh in heads]
        m_at = [[None] * n_blk for _ in heads]
        pvs = [[None] * n_blk for _ in heads]
        for g in range(n_blk):
            for hh in heads:
                s = jnp.where(keep[hh][g], raws[g][hh].astype(BF16), NEG_BIG)
                m = jnp.max(s, axis=0, keepdims=True)
                if ms[hh] is not None:
                    m = jnp.maximum(ms[hh], m)
                p = jnp.exp2(s - m)
                pvs[hh][g] = jnp.dot(vt_scr[j0 + g, hh], p, preferred_element_type=F32)
                m_at[hh][g] = ms[hh] = m
        out = []
        for hh in heads:
            m_fin = ms[hh]
            rescale = lambda m_old: jnp.exp2(m_old.astype(F32) - m_fin.astype(F32))
            accl = None if prev is None else rescale(prev[hh][0]) * prev[hh][1]
            for g in range(n_blk):
                term = pvs[hh][g] if g == n_blk - 1 else rescale(m_at[hh][g]) * pvs[hh][g]
                accl = term if accl is None else accl + term
            out.append((m_fin, accl))
        return out

    qts = [jnp.where(dim_head == hh, q_t, 0.0) for hh in heads]
    qs = [(qts[hh] * (HD_A ** -0.5 * math.log2(math.e))).astype(BF16) for hh in heads]
    causal = key_i <= qry_i
    state0 = attend(i, 1, [[causal]] * n_heads, None)
    sels = []
    for hh in heads:
        gate = jnp.where(blk < i, _dot_hi(kbar, qts[hh]), -jnp.inf)
        sels.append(_top3(gate, blk_f, 0, float(nb)))
    for r in range(n_pg):
        sc_ref[0, r] = _score_page(qb_ref[0].astype(BF16), page_refs[r][0, 0])

    def body(t, carry):
        j0 = t * KV_GROUP
        keep = []
        for hh in heads:
            i1, i2, i3 = sels[hh]
            jfs = [(j0 + g).astype(F32) for g in range(KV_GROUP)]
            keep.append([(i1 == jf) | (i2 == jf) | (i3 == jf) for jf in jfs])
        state = attend(j0, KV_GROUP, keep, [carry[2 * hh:2 * hh + 2] for hh in heads])
        return tuple(x for st in state for x in st)

    carry = lax.fori_loop(0, (i + KV_GROUP - 1) // KV_GROUP, body, tuple(x for st in state0 for x in st))
    o_t = jnp.concatenate([carry[2 * hh + 1][:HD_A] / carry[2 * hh + 1][HD_A:HD_A + 1] for hh in heads],
                          axis=0)
    o_ref[0] = o_t.T


MAX_RIDE_PAGES = 16


def _moba_prompt(q3d, kb3d, vt4d, kbar, paged=None):
    b, t, w = q3d.shape
    nb = t // MOBA_BLOCK
    nb_pad = -(-nb // KV_GROUP) * KV_GROUP
    n_pairs = w // LANES
    hpp = LANES // HD_A
    n_steps = b * n_pairs * nb
    n_pg, pt_flat = 0, jnp.zeros((1,), jnp.int32)
    if paged is not None:
        q_b, cache_kt, layer, pt_flat_p, n_pages = paged
        bs = q_b.shape[0]
        _, _, h, hd, page = cache_kt.shape
        per_step = (bs * n_pages) // n_steps
        if (bs * n_pages) % n_steps == 0 and 0 < per_step <= MAX_RIDE_PAGES and n_pages % per_step == 0:
            n_pg, pt_flat = per_step, pt_flat_p
    qspec = pl.BlockSpec((1, MOBA_BLOCK, LANES), lambda bi, hp, i, pt: (bi, i, hp))
    in_specs = [qspec,
                pl.BlockSpec((1, t, LANES), lambda bi, hp, i, pt: (bi, 0, hp)),
                pl.BlockSpec((1, hpp, HD_A, t), lambda bi, hp, i, pt: (bi, hp, 0, 0)),
                pl.BlockSpec((1, nb, LANES), lambda bi, hp, i, pt: (bi, 0, hp))]
    out_specs = [qspec]
    out_shape = [jax.ShapeDtypeStruct((b, t, w), F32)]
    operands = [q3d, kb3d, vt4d, kbar]
    if n_pg:
        spr = n_pages // n_pg
        flat = lambda bi, hp, i: (bi * n_pairs + hp) * nb + i
        in_specs.append(pl.BlockSpec((1, h, h * hd), lambda bi, hp, i, pt: (flat(bi, hp, i) // spr, 0, 0)))
        in_specs += [pl.BlockSpec((1, 1, h, hd, page),
                                  lambda bi, hp, i, pt, r=r: (layer, pt[(flat(bi, hp, i) // spr) * n_pages
                                                                        + (flat(bi, hp, i) % spr) * n_pg + r],
                                                              0, 0, 0))
                     for r in range(n_pg)]
        out_specs.append(pl.BlockSpec((1, n_pg, h, page),
                                      lambda bi, hp, i, pt: (flat(bi, hp, i) // spr, flat(bi, hp, i) % spr, 0, 0)))
        out_shape.append(jax.ShapeDtypeStruct((bs, n_pages, h, page), F32))
        operands += [q_b] + [cache_kt] * n_pg
    res = pl.pallas_call(
        functools.partial(_moba_prompt_kernel, nb=nb, n_pg=n_pg),
        grid_spec=pltpu.PrefetchScalarGridSpec(
            num_scalar_prefetch=1,
            grid=(b, n_pairs, nb),
            in_specs=in_specs,
            out_specs=out_specs,
            scratch_shapes=[pltpu.VMEM((nb_pad, hpp, V_ROWS, MOBA_BLOCK), BF16)],
        ),
        out_shape=out_shape,
        compiler_params=_cparams(("arbitrary", "arbitrary", "arbitrary")),
        name="moba_prompt",
    )(pt_flat, *operands)
    return res[0], (res[1] if n_pg else None)


def _gdn_seq_kernel(q_ref, k_ref, v_ref, bg_ref, s0_ref, gn_ref, o_ref, s_out_ref, s_scr, *, rows):
    step = pl.program_id(1)
    n_steps = pl.num_programs(1)
    cc = GDN_CHUNK
    ncs = rows // cc

    @pl.when(step == 0)
    def _():
        s_scr[...] = s0_ref[0]

    bg = bg_ref[0]
    gn = gn_ref[...]
    ri = lax.broadcasted_iota(jnp.int32, (rows, rows), 0)
    ci = lax.broadcasted_iota(jnp.int32, (rows, rows), 1)
    same = (ri // cc) == (ci // cc)
    sum_mats = jnp.concatenate([(same & (ci <= ri)).astype(BF16), same.astype(BF16)], axis=0)
    b_hi = bg.astype(BF16)
    r_1 = bg - b_hi.astype(F32)
    b_mid = r_1.astype(BF16)
    b_lo = (r_1 - b_mid.astype(F32)).astype(BF16)
    sums = sum(jnp.dot(sum_mats, piece, preferred_element_type=F32) for piece in (b_hi, b_mid, b_lo))
    gc = sums[:rows]
    gl = sums[rows:]
    gc_t = gc.T
    e_gc = jnp.exp(gc)
    e_tail = jnp.exp(gl - gc)
    e_tot = jnp.exp(gl)
    pr = lax.broadcasted_iota(jnp.int32, (cc, rows), 0)
    pl_i = lax.broadcasted_iota(jnp.int32, (cc, rows), 1)
    pc = pl_i % cc
    lane_blk = pl_i // cc
    causal_p = pr >= pc
    strict_p = pr > pc

    def pack(full):
        out = full[(ncs - 1) * cc:ncs * cc]
        for c in range(ncs - 2, -1, -1):
            out = jnp.where(lane_blk == c, full[c * cc:(c + 1) * cc], out)
        return out

    def pack_col(col):
        out = col[(ncs - 1) * cc:ncs * cc]
        for c in range(ncs - 2, -1, -1):
            out = jnp.where(lane_blk == c, col[c * cc:(c + 1) * cc], out)
        return out

    def bdiag(p):
        return jnp.where(same, jnp.concatenate([p] * ncs, axis=0), 0.0)

    heads = range(H_B)
    col = lambda a, h: a[:, H_B + h:H_B + h + 1]
    qs = [q_ref[0, :, h * DK_B:(h + 1) * DK_B] * (DK_B ** -0.5) for h in heads]
    ks = [k_ref[0, :, h * DK_B:(h + 1) * DK_B] for h in heads]
    kbetas = [ks[h] * bg[:, h:h + 1] for h in heads]
    fulls = [_dot_nt(jnp.concatenate([kbetas[h], qs[h]], axis=0), ks[h]) for h in heads]
    decays = [jnp.exp(jnp.where(causal_p, pack_col(col(gc, h)) - gc_t[H_B + h:H_B + h + 1, :], -jnp.inf))
              for h in heads]
    attns = [jnp.where(causal_p, pack(fulls[h][rows:]) * decays[h], 0.0) for h in heads]
    es = [-jnp.where(strict_p, pack(fulls[h][:rows]) * decays[h], 0.0) for h in heads]
    pws = [_dot(es[h], bdiag(es[h])) for h in heads]
    for _ in range(int(math.log2(cc)) - 2):
        rs_ = [_dot(jnp.concatenate([es[h], pws[h]], axis=0), bdiag(pws[h])) for h in heads]
        es = [es[h] + pws[h] + rs_[h][:cc] for h in heads]
        pws = [rs_[h][cc:] for h in heads]
    es = [es[h] + pws[h] + _dot(es[h], bdiag(pws[h])) for h in heads]
    rhss = [jnp.concatenate([v_ref[0, :, h * DV_B:(h + 1) * DV_B] * bg[:, h:h + 1],
                             kbetas[h] * col(e_gc, h)], axis=1) for h in heads]
    sols = [rhss[h] + _dot(bdiag(es[h]), rhss[h]) for h in heads]
    q_decs = [qs[h] * col(e_gc, h) for h in heads]
    k_tails = [ks[h] * col(e_tail, h) for h in heads]
    ss = [s_scr[h] for h in heads]
    for c in range(ncs):
        rs = slice(c * cc, (c + 1) * cc)
        wqs = [_dot(jnp.concatenate([sols[h][rs, DV_B:], q_decs[h][rs]], axis=0), ss[h]) for h in heads]
        v_news = [sols[h][rs, :DV_B] - wqs[h][:cc] for h in heads]
        os_ = [wqs[h][cc:] + _dot(attns[h][:, rs], v_news[h]) for h in heads]
        kvs = [_dot_tn(k_tails[h][rs], v_news[h]) for h in heads]
        ss = [ss[h] * e_tot[c * cc:c * cc + 1, H_B + h:H_B + h + 1] + kvs[h] for h in heads]
        for h in heads:
            o_ref[0, rs, h * DV_B:(h + 1) * DV_B] = _rms(os_[h], gn)
    for h in heads:
        s_scr[h] = ss[h]


    @pl.when(step == n_steps - 1)
    def _():
        s_out_ref[0] = s_scr[...]


def _gdn_seq(q, k, v, bg, s0, g_onorm_row):
    b, t, _ = q.shape
    rows = MXU_DIM
    while t % rows:
        rows //= 2
    seq = lambda w: pl.BlockSpec((1, rows, w), lambda bi, c: (bi, c, 0))
    sspec = pl.BlockSpec((1, H_B, DK_B, DV_B), lambda bi, c: (bi, 0, 0, 0))
    return pl.pallas_call(
        functools.partial(_gdn_seq_kernel, rows=rows),
        grid=(b, t // rows),
        in_specs=[seq(H_B * DK_B), seq(H_B * DK_B), seq(W_B), seq(LANES), sspec,
                  pl.BlockSpec((1, DV_B), lambda bi, c: (0, 0))],
        out_specs=[seq(W_B), sspec],
        out_shape=[jax.ShapeDtypeStruct((b, t, W_B), F32),
                   jax.ShapeDtypeStruct((b, H_B, DK_B, DV_B), F32)],
        scratch_shapes=[pltpu.VMEM((H_B, DK_B, DV_B), F32)],
        compiler_params=_cparams(("parallel", "arbitrary")),
        name="gdn_seq",
    )(q, k, v, bg, s0, g_onorm_row)


def _gdn_step_kernel(kq_ref, v_ref, bg_ref, s0_ref, gn_ref, o_ref, s_out_ref):
    bg = bg_ref[0]
    gn = gn_ref[...]
    kq = kq_ref[0]
    for h in range(H_B):
        beta = bg[:, h:h + 1]
        eg = jnp.exp(bg[:, H_B + h:H_B + h + 1])
        k_col = kq[:, h:h + 1]
        q_col = kq[:, H_B + h:H_B + h + 1] * (DK_B ** -0.5)
        v_row = v_ref[0, :, h * DV_B:(h + 1) * DV_B]
        s0 = s0_ref[0, h]
        k_s = jnp.sum(k_col * s0, axis=0, keepdims=True)
        q_s = jnp.sum(q_col * s0, axis=0, keepdims=True)
        v_new = beta * (v_row - eg * k_s)
        qk = jnp.sum(q_col * k_col, axis=0, keepdims=True)
        o = eg * q_s + qk * v_new
        s_out_ref[0, h] = s0 * eg + k_col * v_new
        o_ref[0, :, h * DV_B:(h + 1) * DV_B] = _rms(o, gn)


def _gdn_step(kq_cols, v, bg, s0, g_onorm_row):
    b = v.shape[0]
    tok = lambda w: pl.BlockSpec((1, 1, w), lambda bi: (bi, 0, 0))
    sspec = pl.BlockSpec((1, H_B, DK_B, DV_B), lambda bi: (bi, 0, 0, 0))
    return pl.pallas_call(
        _gdn_step_kernel,
        grid=(b,),
        in_specs=[pl.BlockSpec((1, DK_B, 2 * H_B), lambda bi: (bi, 0, 0)), tok(W_B), tok(LANES), sspec,
                  pl.BlockSpec((1, DV_B), lambda bi: (0, 0))],
        out_specs=[tok(W_B), sspec],
        out_shape=[jax.ShapeDtypeStruct((b, 1, W_B), F32),
                   jax.ShapeDtypeStruct((b, H_B, DK_B, DV_B), F32)],
        compiler_params=_cparams(("parallel",)),
        name="gdn_step",
    )(kq_cols, v, bg, s0, g_onorm_row)


def _out_stage_kernel(x_ref, oa_ref, ob_ref, p_ref, gmix_ref, wzg_ref, wpa_ref, wpb_ref, wo_ref,
                      gple_ref, wpg_ref, wple_ref, gfin_ref, y_ref, *, final):
    x = x_ref[...]
    h = _rms(x, gmix_ref[...]).astype(BF16)
    zg = jnp.dot(h, wzg_ref[...], preferred_element_type=F32)
    za = zg[:, :W_A]
    zb = zg[:, W_A:W_A + W_B]
    d = x.shape[1]
    ga = zg[:, W_A + W_B:W_A + W_B + d]
    gb = zg[:, W_A + W_B + d:]
    ya = _dot(oa_ref[...] * _silu(za), wpa_ref[...])
    yb = _dot(ob_ref[...] * _silu(zb), wpb_ref[...])
    mixed = jax.nn.sigmoid(ga) * ya + jax.nn.sigmoid(gb) * yb
    x = x + _dot(mixed, wo_ref[...])
    gate = jax.nn.sigmoid(_dot(_rms(x, gple_ref[...]), wpg_ref[...]))
    x = x + gate * _dot(p_ref[...], wple_ref[...])
    y_ref[...] = _rms(x, gfin_ref[...]) if final else x


def _out_stage(x2d, oa, ob, p2d, gmix, wzg, wpa, wpb, wo, gple, wpg, wple, gfin, tm, final):
    n, d = x2d.shape
    row = lambda w: pl.BlockSpec((tm, w), lambda i: (i, 0))
    const = lambda a: pl.BlockSpec(a.shape, lambda i: (0, 0))
    return pl.pallas_call(
        functools.partial(_out_stage_kernel, final=final),
        grid=(n // tm,),
        in_specs=[row(d), row(W_A), row(W_B), row(p2d.shape[1]), const(gmix), const(wzg), const(wpa),
                  const(wpb), const(wo), const(gple), const(wpg), const(wple), const(gfin)],
        out_specs=row(d),
        out_shape=jax.ShapeDtypeStruct((n, d), F32),
        compiler_params=_cparams(("parallel",)),
        name="out_stage",
    )(x2d, oa, ob, p2d, gmix, wzg, wpa, wpb, wo, gple, wpg, wple, gfin)


PAGES_PER_STEP = 32


def _page_scores_kernel(pt_ref, q_ref, *refs):
    k_refs, o_ref = refs[:-1], refs[-1]
    q_bd = q_ref[0].astype(BF16)
    for r, k_ref in enumerate(k_refs):
        o_ref[0, r] = _score_page(q_bd, k_ref[0, 0])


def _page_scores(q_bd, cache_kt, layer, pt_flat, n_pages):
    b = q_bd.shape[0]
    _, _, h, hd, page = cache_kt.shape
    nps = PAGES_PER_STEP if n_pages % PAGES_PER_STEP == 0 else 1
    in_specs = [pl.BlockSpec((1, h, h * hd), lambda bi, g, pt: (bi, 0, 0))]
    in_specs += [pl.BlockSpec((1, 1, h, hd, page),
                              lambda bi, g, pt, r=r: (layer, pt[bi * n_pages + g * nps + r], 0, 0, 0))
                 for r in range(nps)]
    return pl.pallas_call(
        _page_scores_kernel,
        grid_spec=pltpu.PrefetchScalarGridSpec(
            num_scalar_prefetch=1,
            grid=(b, n_pages // nps),
            in_specs=in_specs,
            out_specs=pl.BlockSpec((1, nps, h, page), lambda bi, g, pt: (bi, g, 0, 0)),
        ),
        out_shape=jax.ShapeDtypeStruct((b, n_pages, h, page), F32),
        compiler_params=_cparams(("parallel", "parallel")),
        name="page_scores",
    )(pt_flat, q_bd, *([cache_kt] * nps))


def _sample_select_kernel(p_ref, o_ref, *, nblk, ppb):
    tot = jnp.sum(p_ref[0], axis=2, keepdims=True)
    gate = jnp.sum(tot.reshape(nblk, ppb, H_A, 1), axis=1) * (1.0 / MOBA_BLOCK)
    blk_f = lax.broadcasted_iota(jnp.int32, (nblk, 1, 1), 0).astype(F32)
    for r, sel in enumerate(_top3(gate, blk_f, 0, float(nblk))):
        o_ref[0, r] = jnp.broadcast_to(jnp.minimum(sel[0], nblk - 1.0), (H_A, LANES)).astype(jnp.int32)


def _sample_select(scores, ppb):
    b, n_pages, h, page = scores.shape
    return pl.pallas_call(
        functools.partial(_sample_select_kernel, nblk=n_pages // ppb, ppb=ppb),
        grid=(b,),
        in_specs=[pl.BlockSpec((1, n_pages, h, page), lambda bi: (bi, 0, 0, 0))],
        out_specs=pl.BlockSpec((1, MOBA_TOPK, h, LANES), lambda bi: (bi, 0, 0, 0)),
        out_shape=jax.ShapeDtypeStruct((b, MOBA_TOPK, h, LANES), jnp.int32),
        compiler_params=_cparams(("parallel",)),
        name="sample_select",
    )(scores)


PAGED_HEADS_PER_STEP = 4


def _paged_attn_kernel(pt_ref, top_ref, q_ref, kn_ref, vn_ref, s_ref, *refs, n_s, ppb):
    v_refs, o_ref = refs[:-1], refs[-1]
    bi = pl.program_id(0)
    hg = pl.program_id(1)
    scale = HD_A ** -0.5
    lane = lax.broadcasted_iota(jnp.int32, (1, H_A), 1)
    row = lax.broadcasted_iota(jnp.int32, (H_A, 1), 0)
    out = jnp.zeros((HD_A, H_A), F32)
    for j in range(PAGED_HEADS_PER_STEP):
        h = hg * PAGED_HEADS_PER_STEP + j
        pick = lambda a, h=h: jnp.sum(jnp.where(row == h, a, 0.0), axis=0, keepdims=True)
        own = jnp.sum(pick(q_ref[0] * kn_ref[0]), axis=1, keepdims=True) * scale
        rows = []
        for s in range(n_s):
            page = top_ref[(bi * MOBA_TOPK + s // ppb) * H_A + h] * ppb + s % ppb
            rows.append(pick(s_ref[0, page]) * scale)
        m = own
        for r in rows:
            m = jnp.maximum(m, jnp.max(r, axis=1, keepdims=True))
        p_own = jnp.exp(own - m)
        ps = [jnp.exp(r - m) for r in rows]
        l = p_own + sum(jnp.sum(p, axis=1, keepdims=True) for p in ps)
        acc = sum(p * v_refs[j * n_s + s][0, 0, 0] for s, p in enumerate(ps))
        vn_col = jnp.sum(jnp.where(lane == h, vn_ref[0], 0.0), axis=1, keepdims=True)
        res = (jnp.sum(acc, axis=1, keepdims=True) + p_own * vn_col) / l
        out = jnp.where(lane == h, res, out)

    @pl.when(hg == 0)
    def _():
        o_ref[0] = out

    @pl.when(hg > 0)
    def _():
        o_ref[0] = o_ref[0] + out


def _paged_attn(q3, kn3, vn_t, scores, cache_vt, layer, pt_flat, top_flat, n_pages):
    b = q3.shape[0]
    _, _, h, hd, page = cache_vt.shape
    ppb = MOBA_BLOCK // page
    n_s = MOBA_TOPK * ppb

    hps = PAGED_HEADS_PER_STEP

    def v_spec(j, s):
        def index(bi, hg, pt, top):
            hi = hg * hps + j
            logical = top[(bi * MOBA_TOPK + s // ppb) * H_A + hi] * ppb + s % ppb
            return (layer, pt[bi * n_pages + logical], hi, 0, 0)
        return pl.BlockSpec((1, 1, 1, hd, page), index)

    tok = pl.BlockSpec((1, h, hd), lambda bi, hg, pt, top: (bi, 0, 0))
    tok_t = pl.BlockSpec((1, hd, h), lambda bi, hg, pt, top: (bi, 0, 0))
    return pl.pallas_call(
        functools.partial(_paged_attn_kernel, n_s=n_s, ppb=ppb),
        grid_spec=pltpu.PrefetchScalarGridSpec(
            num_scalar_prefetch=2,
            grid=(b, h // hps),
            in_specs=[tok, tok, tok_t,
                      pl.BlockSpec((1, n_pages, h, page), lambda bi, hg, pt, top: (bi, 0, 0, 0))]
            + [v_spec(j, s) for j in range(hps) for s in range(n_s)],
            out_specs=tok_t,
        ),
        out_shape=jax.ShapeDtypeStruct((b, hd, h), F32),
        compiler_params=_cparams(("parallel", "arbitrary")),
        name="paged_attn",
    )(pt_flat, top_flat, q3, kn3, vn_t, scores, *([cache_vt] * (hps * n_s)))


def _split_weights(w_in_l, d):
    o = 0
    parts = {}
    for name, n in (("qa", W_A), ("ka", W_A), ("va", W_A), ("za", W_A), ("qb", H_B * DK_B),
                    ("kb", H_B * DK_B), ("vb", W_B), ("zb", W_B), ("beta", H_B), ("alpha", H_B),
                    ("ga", d), ("gb", d)):
        parts[name] = w_in_l[:, o:o + n]
        o += n
    w_attn = jnp.concatenate([parts["qa"], parts["ka"], parts["va"]], axis=1).astype(BF16)
    w_qkv = jnp.concatenate([parts["qb"], parts["kb"], parts["vb"]], axis=1).astype(BF16)
    w_ba = jnp.concatenate([parts["beta"], parts["alpha"],
                            jnp.zeros((d, LANES - 2 * H_B), F32)], axis=1).astype(BF16)
    w_zg = jnp.concatenate([parts["za"], parts["zb"], parts["ga"], parts["gb"]], axis=1).astype(BF16)
    return w_attn, w_qkv, w_ba, w_zg


def _lane_row(vec, offset):
    out = jnp.zeros((1, LANES), F32)
    return out.at[0, offset:offset + vec.shape[0]].set(vec.astype(F32))


def _pick_tile(n, pref):
    t = pref
    while n % t:
        t //= 2
    return t


def kernel(x_prompt, x_sample, p_prompt, p_sample, cache_k, cache_v, page_table, state_gdn_s, state_gdn_conv, g_mix, w_in, conv_w, a_log, dt_bias, g_onorm, w_pa, w_pb, w_o, g_ple, w_ple_gate, w_ple, g_final):
    bp, tp, d = x_prompt.shape
    bs, ts, _ = x_sample.shape
    depth = w_in.shape[0]
    n_pages = page_table.shape[1]
    page = cache_k.shape[2]
    assert ts == 1 and tp % MOBA_BLOCK == 0 and (n_pages * page) % MOBA_BLOCK == 0
    assert tp % GDN_CHUNK == 0

    pos_p = jnp.arange(tp, dtype=jnp.int32)
    pos_s = jnp.full((bs,), n_pages * page, dtype=jnp.int32)
    tabs_p = _rope_tables(pos_p)
    tabs_s = _rope_tables(pos_s)
    pt_flat = page_table.reshape(-1).astype(jnp.int32)
    cache_kt = jnp.transpose(cache_k, (0, 1, 3, 4, 2))
    cache_vt = jnp.transpose(cache_v, (0, 1, 3, 4, 2))

    xp = x_prompt.reshape(bp * tp, d)
    xs = x_sample.reshape(bs, d)
    outs = {k: [] for k in ("kp", "vp", "sp", "cp", "ks", "vs", "ss", "cs")}
    for l in range(depth):
        final = l == depth - 1
        w_attn, w_qkv, w_ba, w_zg = _split_weights(w_in[l], d)
        gmix = g_mix[l].reshape(1, d)
        al_row = _lane_row(a_log[l], H_B)
        dtb_row = _lane_row(dt_bias[l], H_B)
        gon = g_onorm[l].reshape(1, DV_B)
        wpa, wpb, wo = w_pa[l].astype(BF16), w_pb[l].astype(BF16), w_o[l].astype(BF16)
        wpg, wple = w_ple_gate[l].astype(BF16), w_ple[l].astype(BF16)
        gple = g_ple[l].reshape(1, d)
        gfin = g_final.reshape(1, d)

        qa_s, ka_s, va_s = _attn_proj(xs, gmix, w_attn, tabs_s, bs)
        q3 = qa_s.reshape(bs, H_A, HD_A)
        q_b = (jnp.eye(H_A, dtype=F32)[None, :, :, None] * q3[:, None, :, :]).reshape(bs, H_A, W_A)

        qa, kb, kt, vt, kbar = _attn_proj_seq(xp, gmix, w_attn, tabs_p, _pick_tile(tp, 512), bp)
        conv0 = jnp.zeros((bp, SUBLANES, C_CONV), F32)
        qn, kn, vn, bg, tail = _gdn_proj(xp.reshape(bp, tp, d), gmix, w_qkv, w_ba, conv_w[l], conv0,
                                         al_row, dtb_row, _pick_tile(tp, 256), tp)
        oa, scores = _moba_prompt(qa.reshape(bp, tp, W_A), kb.reshape(bp, tp, W_A), vt,
                                  kbar.reshape(bp, tp // MOBA_BLOCK, W_A), (q_b, cache_kt, l, pt_flat, n_pages))
        s0 = jnp.zeros((bp, H_B, DK_B, DV_B), F32)
        ob, s_fin = _gdn_seq(qn, kn, vn, bg, s0, gon)
        xp = _out_stage(xp, oa.reshape(bp * tp, W_A), ob.reshape(bp * tp, W_B),
                        p_prompt[l].reshape(bp * tp, -1), gmix, w_zg, wpa, wpb, wo, gple, wpg, wple,
                        gfin, _pick_tile(bp * tp, 256), final)
        outs["kp"].append(jnp.transpose(kt, (0, 3, 1, 2)))
        outs["vp"].append(jnp.transpose(vt, (0, 3, 1, 2)))
        outs["sp"].append(s_fin.astype(state_gdn_s.dtype))
        outs["cp"].append(tail[:, SUBLANES - (CONV_K - 1):, :])

        conv0_s = jnp.transpose(state_gdn_conv[l].astype(F32), (1, 0, 2))
        qn_s, kn_s, vn_s, bg_s, tail_s = _gdn_tok(xs, gmix, w_qkv, w_ba, conv_w[l], conv0_s, al_row, dtb_row)
        if scores is None:
            scores = _page_scores(q_b, cache_kt, l, pt_flat, n_pages)
        top = _sample_select(scores, MOBA_BLOCK // page)
        top_flat = top[:, :, :, 0].reshape(-1)
        vn_t = jnp.transpose(va_s.reshape(bs, H_A, HD_A), (0, 2, 1))
        oa_t = _paged_attn(q3, ka_s.reshape(bs, H_A, HD_A), vn_t, scores, cache_vt, l, pt_flat, top_flat,
                           n_pages)
        oa_s = jnp.transpose(oa_t, (0, 2, 1))
        cols = lambda a: jnp.transpose(a.reshape(bs, H_B, DK_B), (0, 2, 1))
        kq_cols = jnp.concatenate([cols(kn_s), cols(qn_s)], axis=2)
        ob_s, s_fin_s = _gdn_step(kq_cols, vn_s.reshape(bs, 1, W_B), bg_s.reshape(bs, 1, LANES),
                                  state_gdn_s[l].astype(F32), gon)
        xs = _out_stage(xs, oa_s.reshape(bs, W_A), ob_s[:, 0, :], p_sample[l].reshape(bs, -1), gmix, w_zg,
                        wpa, wpb, wo, gple, wpg, wple, gfin, bs, final)
        outs["ks"].append(ka_s.reshape(bs, 1, H_A, HD_A))
        outs["vs"].append(va_s.reshape(bs, 1, H_A, HD_A))
        outs["ss"].append(s_fin_s.astype(state_gdn_s.dtype))
        outs["cs"].append(jnp.transpose(tail_s, (1, 0, 2)))

    y_prompt = xp.reshape(bp, tp, d)
    y_sample = xs.reshape(bs, ts, d)
    st = lambda k: jnp.stack(outs[k])
    return (y_prompt, y_sample, st("kp"), st("vp"), st("sp"), st("cp"),
            st("ks"), st("vs"), st("ss"), st("cs"))
```
